```python
import math
import jax, jax.numpy as jnp
from jax import lax
import numpy as np

D_MODEL = 1024
BATCH = 4
SEQ = 4096
DEPTH = 2
DEC_BATCH = 32
DEC_SEQ = 4
PAST_LEN = 16384
PAGE_SIZE = 128

HEAD_DIM = 64
DA_HEADS = D_MODEL // (4 * HEAD_DIM)
DA_LAMBDA_INIT = 0.8 - 0.6 * math.exp(-0.3 * 0)
NSA_HEADS = D_MODEL // (2 * HEAD_DIM)
NSA_GROUPS = 2
CMP_BLOCK = 32
CMP_STRIDE = 16
CMP_HIDDEN = 2 * HEAD_DIM
SEL_BLOCK = 64
SEL_TOPK = 16
WINDOW = 512
FOX_HEADS = D_MODEL // HEAD_DIM
FOX_KV_HEADS = 4
FOX_BIAS_INIT = 2.0
N_BUCKETS = 32
MAX_DISTANCE = 128
D_FF = 2816
N_EXPERTS = 8
TOP_K = 2
D_FF_EXPERT = 2816
MOE_BLOCK = 128
Q_BLOCK = 128
LN_EPS = 1e-5
RMS_EPS = 1e-5
ALPHA = (2.0 * DEPTH) ** 0.25
BETA = (8.0 * DEPTH) ** -0.25
NEG_INF = -1e30
FORCED_BONUS = 1e4
POOL_NUM = 5
POOL_DEN = 4
L0_SPLITS = [DA_HEADS * 2 * HEAD_DIM, DA_HEADS * 2 * HEAD_DIM, DA_HEADS * 2 * HEAD_DIM,
             NSA_HEADS * HEAD_DIM, 2 * NSA_GROUPS * HEAD_DIM, 2 * NSA_GROUPS * HEAD_DIM,
             2 * NSA_GROUPS * HEAD_DIM, 3 * NSA_HEADS]
L0_IN = 3 * DA_HEADS * 2 * HEAD_DIM + NSA_HEADS * HEAD_DIM + 6 * NSA_GROUPS * HEAD_DIM + 3 * NSA_HEADS
L0_MIX = DA_HEADS * 2 * HEAD_DIM + NSA_HEADS * HEAD_DIM
L1_SPLITS = [FOX_HEADS * HEAD_DIM, FOX_KV_HEADS * HEAD_DIM, FOX_KV_HEADS * HEAD_DIM, FOX_HEADS]
L1_IN = FOX_HEADS * HEAD_DIM + 2 * FOX_KV_HEADS * HEAD_DIM + FOX_HEADS
L1_MIX = FOX_HEADS * HEAD_DIM

kernel_name = 'diffnsa_fox_moe_deepnorm_step'


def layer_norm(x, g, b):
    xf = x.astype(jnp.float32)
    mu = jnp.mean(xf, -1, keepdims=True)
    var = jnp.mean(jnp.square(xf - mu), -1, keepdims=True)
    return ((xf - mu) * lax.rsqrt(var + LN_EPS) * g + b).astype(x.dtype)


def rms_norm(x, w):
    xf = x.astype(jnp.float32)
    return (xf * lax.rsqrt(jnp.mean(xf * xf, -1, keepdims=True) + RMS_EPS) * w).astype(x.dtype)


def masked_softmax(s, mask):
    return jax.nn.softmax(jnp.where(mask, s, NEG_INF), axis=-1) * mask


def t5_bucket(dist):
    max_exact = N_BUCKETS // 2
    d = jnp.maximum(dist, 0)
    df = jnp.maximum(d, 1).astype(jnp.float32)
    large = max_exact + (jnp.log(df / max_exact) / math.log(MAX_DISTANCE / max_exact)
                         * (N_BUCKETS - max_exact)).astype(jnp.int32)
    return jnp.where(d < max_exact, d, jnp.minimum(large, N_BUCKETS - 1))


def sweep_query_blocks(fn, q_arrays, q_pos):
    tq = q_pos.shape[0]
    bq = Q_BLOCK if tq % Q_BLOCK == 0 else tq
    nb = tq // bq
    blocked = tuple(jnp.moveaxis(a.reshape(a.shape[0], nb, bq, *a.shape[2:]), 1, 0) for a in q_arrays)
    out = lax.map(lambda args: fn(*args), blocked + (q_pos.reshape(nb, bq),))
    out = jnp.moveaxis(out, 0, 1)
    return out.reshape(out.shape[0], tq, *out.shape[3:])


def gather_pages(pool, page_table):
    g = pool[page_table]
    return g.reshape(g.shape[0], g.shape[1] * g.shape[2], *pool.shape[2:])


def diff_attention(q, k1, k2, v, q_pos, k_pos, lam, bias_tbl):
    scale = HEAD_DIM ** -0.5

    def block(qb, pb):
        dist = pb[:, None] - k_pos[None, :]
        mask = dist >= 0
        bias = jnp.moveaxis(bias_tbl[t5_bucket(dist)], -1, 0).astype(jnp.float32)
        s1 = jnp.einsum('bqhd,bkhd->bhqk', qb[..., 0, :], k1).astype(jnp.float32) * scale + bias
        s2 = jnp.einsum('bqhd,bkhd->bhqk', qb[..., 1, :], k2).astype(jnp.float32) * scale + bias
        a = masked_softmax(s1, mask) - lam * masked_softmax(s2, mask)
        return jnp.einsum('bhqk,bkhe->bqhe', a.astype(v.dtype), v)

    return sweep_query_blocks(block, (q,), q_pos)


def nsa_compress(k, pos_emb, w1, w2):
    b, l = k.shape[:2]
    n_chunks = l // CMP_STRIDE
    ratio = CMP_BLOCK // CMP_STRIDE
    nc = n_chunks - ratio + 1
    chunks = k[:, :n_chunks * CMP_STRIDE].reshape(b, n_chunks, CMP_STRIDE, *k.shape[2:])
    win = jnp.concatenate([chunks[:, i:i + nc] for i in range(ratio)], axis=2)
    win = win + pos_emb[:, None, :]
    flat = jnp.moveaxis(win, 3, 2).reshape(b, nc, k.shape[2], CMP_BLOCK * HEAD_DIM)
    return jax.nn.gelu(flat @ w1) @ w2


def nsa_attention(q, gates, kc, vc, c_end, ks, vs, kw, vw, q_pos, w_p0, bias_tbl):
    b, l = ks.shape[:2]
    r = NSA_HEADS // NSA_GROUPS
    scale = HEAD_DIM ** -0.5
    nbs = -(-l // SEL_BLOCK)
    pad = nbs * SEL_BLOCK - l
    k_eff = min(SEL_TOPK, nbs)
    ks_b = jnp.moveaxis(jnp.pad(ks, ((0, 0), (0, pad), (0, 0), (0, 0))).reshape(b, nbs, SEL_BLOCK, NSA_GROUPS, HEAD_DIM), 3, 1)
    vs_b = jnp.moveaxis(jnp.pad(vs, ((0, 0), (0, pad), (0, 0), (0, 0))).reshape(b, nbs, SEL_BLOCK, NSA_GROUPS, HEAD_DIM), 3, 1)
    ci = jnp.arange(kc.shape[1]) * CMP_STRIDE
    sj = jnp.arange(nbs) * SEL_BLOCK
    overlap = ((ci[:, None] < sj[None, :] + SEL_BLOCK) & (ci[:, None] + CMP_BLOCK > sj[None, :])).astype(jnp.float32)
    kw_pad = jnp.pad(kw, ((0, 0), (WINDOW, 0), (0, 0), (0, 0)))
    vw_pad = jnp.pad(vw, ((0, 0), (WINDOW, 0), (0, 0), (0, 0)))
    tbl_g = bias_tbl.astype(jnp.float32).reshape(N_BUCKETS, NSA_GROUPS, r)
    gidx = jnp.arange(NSA_GROUPS)[None, :, None, None, None]
    take_blocks = jax.vmap(jax.vmap(lambda kb, ib: kb[ib]))
    blk_j = jnp.arange(nbs)

    def block(qb, gb, pb):
        bq = pb.shape[0]
        qg = qb.reshape(b, bq, NSA_GROUPS, r, HEAD_DIM)
        dist_c = pb[:, None] - c_end[None, :]
        bias_c = jnp.transpose(tbl_g[t5_bucket(dist_c)], (2, 3, 0, 1))
        s_c = jnp.einsum('bqgrd,bcgd->bgrqc', qg, kc).astype(jnp.float32) * scale + bias_c
        p_c = masked_softmax(s_c, dist_c >= 0)
        o_c = jnp.einsum('bgrqc,bcgd->bqgrd', p_c.astype(vc.dtype), vc)
        imp = jnp.einsum('bgrqc,cj->bgqj', p_c, overlap)
        cur = pb // SEL_BLOCK
        valid = blk_j[None, :] <= cur[:, None]
        forced = valid & ((blk_j[None, :] == 0) | (blk_j[None, :] >= cur[:, None] - 1))
        score = jnp.where(valid, imp + jnp.where(forced, FORCED_BONUS, 0.0), -1.0)
        top_score, idx = lax.top_k(score, k_eff)
        blk_ok = top_score >= 0.0
        ksel = take_blocks(ks_b, idx)
        vsel = take_blocks(vs_b, idx)
        pos_s = idx[..., None] * SEL_BLOCK + jnp.arange(SEL_BLOCK)
        dist_s = pb[:, None, None] - pos_s
        mask_s = (dist_s >= 0) & blk_ok[..., None]
        bias_s = jnp.moveaxis(tbl_g[t5_bucket(dist_s), gidx], -1, 2)
        s_s = jnp.einsum('bqgrd,bgqnkd->bgrqnk', qg, ksel).astype(jnp.float32) * scale + bias_s
        p_s = masked_softmax(s_s.reshape(b, NSA_GROUPS, r, bq, k_eff * SEL_BLOCK),
                             mask_s[:, :, None].reshape(b, NSA_GROUPS, 1, bq, k_eff * SEL_BLOCK))
        o_s = jnp.einsum('bgrqk,bgqkd->bqgrd', p_s.astype(vs.dtype),
                         vsel.reshape(b, NSA_GROUPS, bq, k_eff * SEL_BLOCK, HEAD_DIM))
        start = pb[0] - w_p0
        kwb = lax.dynamic_slice_in_dim(kw_pad, start, WINDOW + bq, axis=1)
        vwb = lax.dynamic_slice_in_dim(vw_pad, start, WINDOW + bq, axis=1)
        pos_w = pb[0] - WINDOW + jnp.arange(WINDOW + bq)
        dist_w = pb[:, None] - pos_w[None, :]
        mask_w = (dist_w >= 0) & (dist_w < WINDOW) & (pos_w >= w_p0)[None, :]
        bias_w = jnp.transpose(tbl_g[t5_bucket(dist_w)], (2, 3, 0, 1))
        s_w = jnp.einsum('bqgrd,bkgd->bgrqk', qg, kwb).astype(jnp.float32) * scale + bias_w
        p_w = masked_softmax(s_w, mask_w)
        o_w = jnp.einsum('bgrqk,bkgd->bqgrd', p_w.astype(vw.dtype), vwb)
        g = jax.nn.sigmoid(gb.astype(jnp.float32)).reshape(b, bq, NSA_GROUPS, r, 3)
        o = g[..., 0:1] * o_c + g[..., 1:2] * o_s + g[..., 2:3] * o_w
        return o.reshape(b, bq, NSA_HEADS, HEAD_DIM).astype(qb.dtype)

    return sweep_query_blocks(block, (q, gates), q_pos)


def forgetting_attention(q, k, v, c_q, c_k, q_pos, k_pos):
    b = q.shape[0]
    r = FOX_HEADS // FOX_KV_HEADS
    scale = HEAD_DIM ** -0.5
    ck = jnp.moveaxis(c_k.reshape(b, -1, FOX_KV_HEADS, r), 1, -1)

    def block(qb, cqb, pb):
        bq = pb.shape[0]
        qg = qb.reshape(b, bq, FOX_KV_HEADS, r, HEAD_DIM)
        decay = jnp.moveaxis(cqb.reshape(b, bq, FOX_KV_HEADS, r), 1, -1)[..., None] - ck[..., None, :]
        s = jnp.einsum('bqgrd,bkgd->bgrqk', qg, k).astype(jnp.float32) * scale + decay
        p = masked_softmax(s, pb[:, None] >= k_pos[None, :])
        o = jnp.einsum('bgrqk,bkgd->bqgrd', p.astype(v.dtype), v)
        return o.reshape(b, bq, FOX_HEADS, HEAD_DIM)

    return sweep_query_blocks(block, (q, c_q), q_pos)


def mixer_diff_nsa(x, pos0, past, w_in, w_out, da_lambda, da_norm, cmp_pos, cmp_w1, cmp_w2, rel_bias):
    b, t, _ = x.shape
    pieces = jnp.split(x @ w_in, [int(s) for s in np.cumsum(L0_SPLITS)[:-1]], axis=-1)
    q_da, k_da, v_da, q_ns, cmp_kv, sel_kv, win_kv, gate = pieces
    new_diff = jnp.concatenate([k_da.reshape(b, t, DA_HEADS, 2 * HEAD_DIM),
                                v_da.reshape(b, t, DA_HEADS, 2 * HEAD_DIM)], axis=-1)
    new_cmp = cmp_kv.reshape(b, t, 2, NSA_GROUPS, HEAD_DIM)
    new_sel = sel_kv.reshape(b, t, 2, NSA_GROUPS, HEAD_DIM)
    new_win = win_kv.reshape(b, t, 2, NSA_GROUPS, HEAD_DIM)
    if past is None:
        full_diff, full_cmp, full_sel, full_win = new_diff, new_cmp, new_sel, new_win
        w_p0 = pos0
    else:
        p_diff, p_cmp, p_sel, p_win = past
        full_diff = jnp.concatenate([p_diff, new_diff], axis=1)
        full_cmp = jnp.concatenate([p_cmp, new_cmp], axis=1)
        full_sel = jnp.concatenate([p_sel, new_sel], axis=1)
        full_win = jnp.concatenate([p_win, new_win], axis=1)
        w_p0 = pos0 - p_win.shape[1]
    k_pos = jnp.arange(full_diff.shape[1])
    q_pos = pos0 + jnp.arange(t)
    lp = da_lambda.astype(jnp.float32)
    lam = jnp.exp(jnp.sum(lp[0] * lp[1])) - jnp.exp(jnp.sum(lp[2] * lp[3])) + DA_LAMBDA_INIT
    o_da = diff_attention(q_da.reshape(b, t, DA_HEADS, 2, HEAD_DIM), full_diff[..., :HEAD_DIM],
                          full_diff[..., HEAD_DIM:2 * HEAD_DIM], full_diff[..., 2 * HEAD_DIM:],
                          q_pos, k_pos, lam, rel_bias[:, :DA_HEADS])
    o_da = rms_norm(o_da, da_norm) * (1.0 - DA_LAMBDA_INIT)
    kc = nsa_compress(full_cmp[:, :, 0], cmp_pos[0], cmp_w1[0], cmp_w2[0])
    vc = nsa_compress(full_cmp[:, :, 1], cmp_pos[1], cmp_w1[1], cmp_w2[1])
    c_end = jnp.arange(kc.shape[1]) * CMP_STRIDE + CMP_BLOCK - 1
    o_ns = nsa_attention(q_ns.reshape(b, t, NSA_HEADS, HEAD_DIM), gate.reshape(b, t, NSA_HEADS, 3),
                         kc, vc, c_end, full_sel[:, :, 0], full_sel[:, :, 1],
                         full_win[:, :, 0], full_win[:, :, 1], q_pos, w_p0, rel_bias[:, DA_HEADS:])
    out = jnp.concatenate([o_da.reshape(b, t, -1), o_ns.reshape(b, t, -1)], axis=-1) @ w_out
    keep = min(WINDOW, full_win.shape[1])
    return out, (new_diff, new_cmp, new_sel, full_win[:, full_win.shape[1] - keep:])


def mixer_fox(x, pos0, past, w_in, b_f, w_out):
    b, t, _ = x.shape
    q, k, v, f_logit = jnp.split(x @ w_in, [int(s) for s in np.cumsum(L1_SPLITS)[:-1]], axis=-1)
    new_kv = jnp.stack([k.reshape(b, t, FOX_KV_HEADS, HEAD_DIM), v.reshape(b, t, FOX_KV_HEADS, HEAD_DIM)], axis=2)
    new_logf = jax.nn.log_sigmoid((f_logit + b_f).astype(jnp.float32))
    if past is None:
        full_kv, full_logf = new_kv, new_logf
    else:
        full_kv = jnp.concatenate([past[0], new_kv], axis=1)
        full_logf = jnp.concatenate([past[1].astype(jnp.float32), new_logf], axis=1)
    c = jnp.cumsum(full_logf, axis=1)
    k_pos = jnp.arange(full_kv.shape[1])
    q_pos = pos0 + jnp.arange(t)
    o = forgetting_attention(q.reshape(b, t, FOX_HEADS, HEAD_DIM), full_kv[:, :, 0], full_kv[:, :, 1],
                             c[:, c.shape[1] - t:], c, q_pos, k_pos)
    return o.reshape(b, t, -1) @ w_out, (new_kv, new_logf)


def swiglu(x, w_gate, w_up, w_down):
    return (jax.nn.silu(x @ w_gate) * (x @ w_up)) @ w_down


def moe_swiglu(x, w_router, w_gate, w_up, w_down):
    b, t, d = x.shape
    xt = x.reshape(-1, d)
    n_assign = xt.shape[0] * TOP_K
    logits = (xt @ w_router).astype(jnp.float32)
    top_logit, top_e = lax.top_k(logits, TOP_K)
    gate = jax.nn.softmax(top_logit, axis=-1)
    flat_e = top_e.reshape(-1)
    order = jnp.argsort(flat_e)
    e_sorted = flat_e[order]
    tok_sorted = (jnp.arange(n_assign) // TOP_K)[order]
    g_sorted = gate.reshape(-1)[order]
    counts = jnp.bincount(flat_e, length=N_EXPERTS)
    padded = (counts + MOE_BLOCK - 1) // MOE_BLOCK * MOE_BLOCK
    start = jnp.cumsum(counts) - counts
    pad_end = jnp.cumsum(padded)
    pad_start = pad_end - padded
    dest = pad_start[e_sorted] + jnp.arange(n_assign) - start[e_sorted]
    n_blocks = -(-n_assign // MOE_BLOCK) + N_EXPERTS
    buf = jnp.zeros((n_blocks * MOE_BLOCK, d), x.dtype).at[dest].set(xt[tok_sorted])
    block_e = jnp.minimum(jnp.searchsorted(pad_end, jnp.arange(n_blocks) * MOE_BLOCK, side='right'), N_EXPERTS - 1)

    def expert_block(args):
        xb, e = args
        return swiglu(xb, w_gate[e], w_up[e], w_down[e])

    out = lax.map(expert_block, (buf.reshape(n_blocks, MOE_BLOCK, d), block_e)).reshape(-1, d)
    y_assign = out[dest] * g_sorted[:, None].astype(x.dtype)
    return jnp.zeros_like(xt).at[tok_sorted].add(y_assign).reshape(b, t, d)


def trunk(x, pos0, past_ab, past_c, p):
    states = []
    for layer in range(DEPTH):
        if layer % 2 == 0:
            h, st = mixer_diff_nsa(x, pos0, past_ab, p['l0_w_in'], p['l0_w_out'], p['da_lambda'], p['da_norm'],
                                   p['nsa_cmp_pos'], p['nsa_cmp_w1'], p['nsa_cmp_w2'], p['rel_bias'])
            x = layer_norm(ALPHA * x + h, p['ln_g'][2 * layer], p['ln_b'][2 * layer])
            f = swiglu(x, p['ffn_w_gate'], p['ffn_w_up'], p['ffn_w_down'])
        else:
            h, st = mixer_fox(x, pos0, past_c, p['l1_w_in'], p['fox_b_f'], p['l1_w_out'])
            x = layer_norm(ALPHA * x + h, p['ln_g'][2 * layer], p['ln_b'][2 * layer])
            f = moe_swiglu(x, p['moe_router'], p['moe_w_gate'], p['moe_w_up'], p['moe_w_down'])
        x = layer_norm(ALPHA * x + f, p['ln_g'][2 * layer + 1], p['ln_b'][2 * layer + 1])
        states.append(st)
    return x, states


def setup_inputs(seed: int = 0) -> dict:
    key = jax.random.key(seed)
    ks = jax.random.split(key, 32)
    f32 = jnp.float32

    def nrm(k, shape, scale):
        return jax.random.normal(k, shape, f32) * scale

    n_pages = PAST_LEN // PAGE_SIZE
    n_phys = (POOL_NUM * DEC_BATCH * n_pages + POOL_DEN - 1) // POOL_DEN
    w_buf = min(WINDOW, PAST_LEN)
    page_table = jax.random.permutation(ks[0], n_phys)[:DEC_BATCH * n_pages].reshape(DEC_BATCH, n_pages).astype(jnp.int32)
    return {
        'x_prompt': nrm(ks[1], (BATCH, SEQ, D_MODEL), 1.0),
        'x_sample': nrm(ks[2], (DEC_BATCH, DEC_SEQ, D_MODEL), 1.0),
        'cache_diff_kv': nrm(ks[3], (n_phys, PAGE_SIZE, DA_HEADS, 4 * HEAD_DIM), 1.0),
        'cache_nsa_cmp': nrm(ks[4], (n_phys, PAGE_SIZE, 2, NSA_GROUPS, HEAD_DIM), 1.0),
        'cache_nsa_sel': nrm(ks[5], (n_phys, PAGE_SIZE, 2, NSA_GROUPS, HEAD_DIM), 1.0),
        'state_nsa_win': nrm(ks[6], (DEC_BATCH, w_buf, 2, NSA_GROUPS, HEAD_DIM), 1.0),
        'cache_fox_kv': nrm(ks[7], (n_phys, PAGE_SIZE, 2, FOX_KV_HEADS, HEAD_DIM), 1.0),
        'cache_fox_logf': jax.nn.log_sigmoid(FOX_BIAS_INIT + nrm(ks[8], (n_phys, PAGE_SIZE, FOX_HEADS), 1.0)),
        'page_table': page_table,
        'rel_bias': nrm(ks[9], (N_BUCKETS, DA_HEADS + NSA_HEADS), 0.5),
        'l0_w_in': nrm(ks[10], (D_MODEL, L0_IN), D_MODEL ** -0.5),
        'l0_w_out': nrm(ks[11], (L0_MIX, D_MODEL), L0_MIX ** -0.5 * BETA),
        'da_lambda': nrm(ks[12], (4, HEAD_DIM), 0.1),
        'da_norm': 1.0 + nrm(ks[13], (2 * HEAD_DIM,), 0.02),
        'nsa_cmp_pos': nrm(ks[14], (2, CMP_BLOCK, HEAD_DIM), 0.1),
        'nsa_cmp_w1': nrm(ks[15], (2, CMP_BLOCK * HEAD_DIM, CMP_HIDDEN), (CMP_BLOCK * HEAD_DIM) ** -0.5),
        'nsa_cmp_w2': nrm(ks[16], (2, CMP_HIDDEN, HEAD_DIM), CMP_HIDDEN ** -0.5),
        'ffn_w_gate': nrm(ks[17], (D_MODEL, D_FF), D_MODEL ** -0.5),
        'ffn_w_up': nrm(ks[18], (D_MODEL, D_FF), D_MODEL ** -0.5),
        'ffn_w_down': nrm(ks[19], (D_FF, D_MODEL), D_FF ** -0.5 * BETA),
        'l1_w_in': nrm(ks[20], (D_MODEL, L1_IN), D_MODEL ** -0.5),
        'fox_b_f': FOX_BIAS_INIT + nrm(ks[21], (FOX_HEADS,), 0.1),
        'l1_w_out': nrm(ks[22], (L1_MIX, D_MODEL), L1_MIX ** -0.5 * BETA),
        'moe_router': nrm(ks[23], (D_MODEL, N_EXPERTS), D_MODEL ** -0.5),
        'moe_w_gate': nrm(ks[24], (N_EXPERTS, D_MODEL, D_FF_EXPERT), D_MODEL ** -0.5),
        'moe_w_up': nrm(ks[25], (N_EXPERTS, D_MODEL, D_FF_EXPERT), D_MODEL ** -0.5),
        'moe_w_down': nrm(ks[26], (N_EXPERTS, D_FF_EXPERT, D_MODEL), D_FF_EXPERT ** -0.5 * BETA),
        'ln_g': 1.0 + nrm(ks[27], (2 * DEPTH, D_MODEL), 0.02),
        'ln_b': nrm(ks[28], (2 * DEPTH, D_MODEL), 0.02),
    }


def reference(x_prompt, x_sample, cache_diff_kv, cache_nsa_cmp, cache_nsa_sel, state_nsa_win,
              cache_fox_kv, cache_fox_logf, page_table, rel_bias, l0_w_in, l0_w_out, da_lambda, da_norm,
              nsa_cmp_pos, nsa_cmp_w1, nsa_cmp_w2, ffn_w_gate, ffn_w_up, ffn_w_down, l1_w_in, fox_b_f,
              l1_w_out, moe_router, moe_w_gate, moe_w_up, moe_w_down, ln_g, ln_b):
    p = dict(rel_bias=rel_bias, l0_w_in=l0_w_in, l0_w_out=l0_w_out, da_lambda=da_lambda, da_norm=da_norm,
             nsa_cmp_pos=nsa_cmp_pos, nsa_cmp_w1=nsa_cmp_w1, nsa_cmp_w2=nsa_cmp_w2, ffn_w_gate=ffn_w_gate,
             ffn_w_up=ffn_w_up, ffn_w_down=ffn_w_down, l1_w_in=l1_w_in, fox_b_f=fox_b_f, l1_w_out=l1_w_out,
             moe_router=moe_router, moe_w_gate=moe_w_gate, moe_w_up=moe_w_up, moe_w_down=moe_w_down,
             ln_g=ln_g, ln_b=ln_b)
    y_prompt, st_p = trunk(x_prompt, 0, None, None, p)
    past_len = page_table.shape[1] * cache_diff_kv.shape[1]
    past_ab = (gather_pages(cache_diff_kv, page_table), gather_pages(cache_nsa_cmp, page_table),
               gather_pages(cache_nsa_sel, page_table), state_nsa_win)
    past_c = (gather_pages(cache_fox_kv, page_table), gather_pages(cache_fox_logf, page_table))
    y_sample, st_s = trunk(x_sample, past_len, past_ab, past_c, p)
    (diff_p, cmp_p, sel_p, win_p), (fkv_p, flf_p) = st_p
    (diff_s, cmp_s, sel_s, win_s), (fkv_s, flf_s) = st_s
    return (y_prompt, y_sample, diff_p, diff_s, cmp_p, cmp_s, sel_p, sel_s, win_p, win_s, fkv_p, fkv_s, flf_p, flf_s)
```

```python
import functools
import math

import numpy as np
import jax
import jax.numpy as jnp
from jax import lax
from jax.experimental import pallas as pl
from jax.experimental.pallas import tpu as pltpu

F32, BF16, I32 = jnp.float32, jnp.bfloat16, jnp.int32

HEAD_DIM = 64
DA_HEADS = 4
NSA_HEADS = 8
NSA_GROUPS = 2
NSA_R = NSA_HEADS // NSA_GROUPS
CMP_BLOCK = 32
CMP_STRIDE = 16
CMP_HIDDEN = 2 * HEAD_DIM
SEL_BLOCK = 64
SEL_TOPK = 16
WINDOW = 512
FOX_HEADS = 16
FOX_KV_HEADS = 4
FOX_R = FOX_HEADS // FOX_KV_HEADS
N_BUCKETS = 32
MAX_DISTANCE = 128
N_EXPERTS = 8
TOP_K = 2
MOE_BLOCK = 128
LN_EPS = 1e-5
RMS_EPS = 1e-5
DEPTH = 2
ALPHA = (2.0 * DEPTH) ** 0.25
DA_LAMBDA_INIT = 0.8 - 0.6 * math.exp(-0.3 * 0)
NEG_INF = -1e30
FORCED_BONUS = 1e4
SCALE = HEAD_DIM ** -0.5

PAGE = 128
BQ = 128
BK = 256
VMEM_LIMIT = 56 * 1024 * 1024


def _t5_thresholds():
    d = np.arange(0, 4 * MAX_DISTANCE)
    df = np.maximum(d, 1).astype(np.float32)
    max_exact = N_BUCKETS // 2
    large = max_exact + (np.log(df / np.float32(max_exact)) / np.float32(math.log(MAX_DISTANCE / max_exact))
                         * np.float32(N_BUCKETS - max_exact)).astype(np.int32)
    bucket = np.where(d < max_exact, d, np.minimum(large, N_BUCKETS - 1))
    return [int(np.argmax(bucket >= b)) for b in range(1, N_BUCKETS)]


T5_THR = _t5_thresholds()
T5_FAR = T5_THR[-1]


def _t5_bias(d, val):
    out = jnp.zeros(d.shape, F32) + val(0)
    for b in range(1, N_BUCKETS):
        out = jnp.where(d >= T5_THR[b - 1], val(b), out)
    return out


def _dot(a, b):
    return jnp.dot(a, b, preferred_element_type=F32)


def _dot_nt(a, b):
    return lax.dot_general(a, b, (((1,), (1,)), ((), ())), preferred_element_type=F32)


def _split3(x):
    hi = x.astype(BF16)
    r1 = x - hi.astype(F32)
    mid = r1.astype(BF16)
    lo = (r1 - mid.astype(F32)).astype(BF16)
    return hi, mid, lo


def _layer_norm(z, g, b):
    mu = jnp.mean(z, -1, keepdims=True)
    zc = z - mu
    var = jnp.mean(zc * zc, -1, keepdims=True)
    return zc * lax.rsqrt(var + LN_EPS) * g + b


def _online_update(s, v, m_ref, l_ref, acc_ref):
    m_prev = m_ref[...]
    m_new = jnp.maximum(m_prev, jnp.max(s, -1, keepdims=True))
    alpha = jnp.exp(m_prev - m_new)
    p = jnp.exp(s - m_new)
    l_ref[...] = alpha * l_ref[...] + jnp.sum(p, -1, keepdims=True)
    acc_ref[...] = alpha * acc_ref[...] + _dot(p.astype(BF16), v)
    m_ref[...] = m_new


def _init_state(m_ref, l_ref, acc_ref):
    m_ref[...] = jnp.full(m_ref.shape, NEG_INF, F32)
    l_ref[...] = jnp.zeros(l_ref.shape, F32)
    acc_ref[...] = jnp.zeros(acc_ref.shape, F32)


def _row_tile(n):
    for t in (512, 384, 256, 128, 64, 32, 16, 8):
        if n % t == 0:
            return t
    raise ValueError(f"row count {n} is not a multiple of 8")


def _params(sem):
    return pltpu.CompilerParams(dimension_semantics=sem, vmem_limit_bytes=VMEM_LIMIT)


def _const_spec(shape):
    nd = len(shape)
    return pl.BlockSpec(shape, lambda *a: (0,) * nd, pipeline_mode=pl.Buffered(1))


def _proj_kernel(x_ref, w_ref, o_ref, ob_ref):
    y = _dot(x_ref[...].astype(BF16), w_ref[...])
    o_ref[...] = y
    ob_ref[...] = y.astype(BF16)


def _proj(x, w):
    n, k = x.shape
    c = w.shape[1]
    tm = _row_tile(n)
    return pl.pallas_call(
        _proj_kernel, name="proj",
        out_shape=(jax.ShapeDtypeStruct((n, c), F32), jax.ShapeDtypeStruct((n, c), BF16)),
        grid=(n // tm,),
        in_specs=[pl.BlockSpec((tm, k), lambda i: (i, 0)), _const_spec((k, c))],
        out_specs=(pl.BlockSpec((tm, c), lambda i: (i, 0)), pl.BlockSpec((tm, c), lambda i: (i, 0))),
        compiler_params=_params(("parallel",)),
    )(x, w)


def _rowmm_kernel(x_ref, w_ref, o_ref):
    o_ref[...] = _dot(x_ref[...].astype(BF16), w_ref[...])


def _rowmm(x, w):
    n, k = x.shape
    c = w.shape[1]
    tm = _row_tile(n)
    tm = min(tm, 256)
    return pl.pallas_call(
        _rowmm_kernel, name="rowmm",
        out_shape=jax.ShapeDtypeStruct((n, c), F32),
        grid=(n // tm,),
        in_specs=[pl.BlockSpec((tm, k), lambda i: (i, 0)), _const_spec((k, c))],
        out_specs=pl.BlockSpec((tm, c), lambda i: (i, 0)),
        compiler_params=_params(("parallel",)),
    )(x, w)


def _mix_ln_kernel(*refs, n_in):
    a_refs = refs[:n_in]
    w_ref, x_ref, g_ref, b_ref, y_ref = refs[n_in:]
    acc = None
    off = 0
    for a in a_refs:
        k = a.shape[1]
        t = _dot(a[...], w_ref[off:off + k, :])
        acc = t if acc is None else acc + t
        off += k
    y_ref[...] = _layer_norm(ALPHA * x_ref[...] + acc, g_ref[...], b_ref[...])


def _mix_ln(a_list, w, x, g, b):
    n, dm = x.shape
    tm = _row_tile(n)
    in_specs = [pl.BlockSpec((tm, a.shape[1]), lambda i: (i, 0)) for a in a_list]
    in_specs += [_const_spec(w.shape), pl.BlockSpec((tm, dm), lambda i: (i, 0)),
                 _const_spec((1, dm)), _const_spec((1, dm))]
    return pl.pallas_call(
        functools.partial(_mix_ln_kernel, n_in=len(a_list)), name="mix_ln",
        out_shape=jax.ShapeDtypeStruct((n, dm), F32),
        grid=(n // tm,), in_specs=in_specs,
        out_specs=pl.BlockSpec((tm, dm), lambda i: (i, 0)),
        compiler_params=_params(("parallel",)),
    )(*a_list, w, x, g.reshape(1, dm), b.reshape(1, dm))


def _ffn_ln_kernel(x_ref, wg_ref, wu_ref, wd_ref, g_ref, b_ref, y_ref):
    x = x_ref[...]
    xb = x.astype(BF16)
    h = jax.nn.silu(_dot(xb, wg_ref[...])) * _dot(xb, wu_ref[...])
    f = _dot(h.astype(BF16), wd_ref[...])
    y_ref[...] = _layer_norm(ALPHA * x + f, g_ref[...], b_ref[...])


def _ffn_ln(x, wg, wu, wd, g, b):
    n, dm = x.shape
    tm = min(_row_tile(n), 384)
    return pl.pallas_call(
        _ffn_ln_kernel, name="ffn_ln",
        out_shape=jax.ShapeDtypeStruct((n, dm), F32),
        grid=(n // tm,),
        in_specs=[pl.BlockSpec((tm, dm), lambda i: (i, 0)), _const_spec(wg.shape), _const_spec(wu.shape),
                  _const_spec(wd.shape), _const_spec((1, dm)), _const_spec((1, dm))],
        out_specs=pl.BlockSpec((tm, dm), lambda i: (i, 0)),
        compiler_params=_params(("parallel",)),
    )(x, wg, wu, wd, g.reshape(1, dm), b.reshape(1, dm))


def _add_ln_kernel(x_ref, f_ref, g_ref, b_ref, y_ref):
    y_ref[...] = _layer_norm(ALPHA * x_ref[...] + f_ref[...], g_ref[...], b_ref[...])


def _add_ln(x, f, g, b):
    n, dm = x.shape
    tm = _row_tile(n)
    return pl.pallas_call(
        _add_ln_kernel, name="add_ln",
        out_shape=jax.ShapeDtypeStruct((n, dm), F32),
        grid=(n // tm,),
        in_specs=[pl.BlockSpec((tm, dm), lambda i: (i, 0)), pl.BlockSpec((tm, dm), lambda i: (i, 0)),
                  _const_spec((1, dm)), _const_spec((1, dm))],
        out_specs=pl.BlockSpec((tm, dm), lambda i: (i, 0)),
        compiler_params=_params(("parallel",)),
    )(x, f, g.reshape(1, dm), b.reshape(1, dm))


def _bias_tiles_kernel(rel_ref, o_ref, *, nt, window, head0):
    h = pl.program_id(0) + head0
    ii = lax.broadcasted_iota(I32, (BQ, BK), 0)
    jj = lax.broadcasted_iota(I32, (BQ, BK), 1)
    for t in range(nt):
        d = t * BQ + ii - jj
        bias = _t5_bias(d, lambda b: rel_ref[b, h])
        bad = d < 0
        if window:
            bad = bad | (d >= WINDOW)
        o_ref[0, t] = jnp.where(bad, NEG_INF, bias)


def _bias_tiles(rel_bias, nt, window, head0, nheads):
    return pl.pallas_call(
        functools.partial(_bias_tiles_kernel, nt=nt, window=window, head0=head0), name="bias_tiles",
        out_shape=jax.ShapeDtypeStruct((nheads, nt, BQ, BK), F32),
        grid=(nheads,),
        in_specs=[pl.BlockSpec(memory_space=pltpu.SMEM)],
        out_specs=pl.BlockSpec((1, nt, BQ, BK), lambda h: (h, 0, 0, 0)),
        compiler_params=_params(("parallel",)),
    )(rel_bias)


def _da_lambda(lp):
    return (jnp.exp(jnp.sum(lp[0:1] * lp[1:2], -1, keepdims=True))
            - jnp.exp(jnp.sum(lp[2:3] * lp[3:4], -1, keepdims=True)) + DA_LAMBDA_INIT)


def _diff_finish_math(o1, o2, lp, nw):
    a = o1 - _da_lambda(lp) * o2
    return a * lax.rsqrt(jnp.mean(a * a, -1, keepdims=True) + RMS_EPS) * nw * (1.0 - DA_LAMBDA_INIT)


def _diff_kernel(rel_ref, lam_ref, nw_ref, bt_ref, q_ref, k_ref, v_ref, o_ref, m_ref, l_ref, acc_ref):
    h = pl.program_id(1)
    i = pl.program_id(2)
    jd = lax.shift_right_logical(i, 1)
    t0 = lax.bitwise_and(i, 1)
    q = q_ref[...]
    lane = lax.broadcasted_iota(I32, q.shape, 1)
    zero = jnp.zeros_like(q)
    qs = jnp.concatenate([jnp.where(lane < HEAD_DIM, q, zero), jnp.where(lane >= HEAD_DIM, q, zero)], 0)
    qs = qs * jnp.asarray(SCALE, BF16)
    _init_state(m_ref, l_ref, acc_ref)
    far_bias = rel_ref[N_BUCKETS - 1, h]

    def tile(j, bias):
        off = pl.multiple_of(j * BK, BK)
        s = _dot_nt(qs, k_ref[pl.ds(off, BK), :])
        if bias is None:
            s = s + far_bias
        else:
            s = s + jnp.concatenate([bias, bias], 0)
        _online_update(s, v_ref[pl.ds(off, BK), :], m_ref, l_ref, acc_ref)

    def far_body(j, c):
        tile(j, None)
        return c

    lax.fori_loop(0, jnp.maximum(jd - 1, 0), far_body, 0)

    @pl.when(jd >= 1)
    def _():
        tile(jd - 1, bt_ref[0, t0 + 2])

    tile(jd, bt_ref[0, t0])
    o = acc_ref[...] / l_ref[...]
    r = _diff_finish_math(o[:BQ], o[BQ:], lam_ref[...], nw_ref[...])
    o_ref[...] = r.astype(o_ref.dtype)


def _diff_prompt(p0b, rel_bias, da_lambda, da_norm, bt, b, t, col_q, col_kv):
    nq = t // BQ
    return pl.pallas_call(
        _diff_kernel, name="diff_prompt",
        out_shape=jax.ShapeDtypeStruct((b * t, DA_HEADS * 2 * HEAD_DIM), BF16),
        grid=(b, DA_HEADS, nq),
        in_specs=[pl.BlockSpec(memory_space=pltpu.SMEM),
                  _const_spec((4, HEAD_DIM)), _const_spec((1, 2 * HEAD_DIM)),
                  pl.BlockSpec((1, 4, BQ, BK), lambda bi, h, i: (h, 0, 0, 0)),
                  pl.BlockSpec((BQ, 128), lambda bi, h, i: (bi * nq + i, col_q + h)),
                  pl.BlockSpec((t, 128), lambda bi, h, i: (bi, col_kv + 2 * h)),
                  pl.BlockSpec((t, 128), lambda bi, h, i: (bi, col_kv + 2 * h + 1))],
        out_specs=pl.BlockSpec((BQ, 128), lambda bi, h, i: (bi * nq + i, h)),
        scratch_shapes=[pltpu.VMEM((2 * BQ, 1), F32), pltpu.VMEM((2 * BQ, 1), F32),
                        pltpu.VMEM((2 * BQ, 2 * HEAD_DIM), F32)],
        compiler_params=_params(("parallel", "parallel", "arbitrary")),
    )(rel_bias, da_lambda, da_norm.reshape(1, -1), bt, p0b, p0b, p0b)


def _diff_finish_kernel(o1_ref, o2_ref, lam_ref, nw_ref, o_ref):
    o_ref[...] = _diff_finish_math(o1_ref[...], o2_ref[...], lam_ref[...], nw_ref[...]).astype(o_ref.dtype)


def _diff_finish(o1, o2, da_lambda, da_norm):
    n, w = o1.shape
    return pl.pallas_call(
        _diff_finish_kernel, name="diff_finish",
        out_shape=jax.ShapeDtypeStruct((n, w), BF16),
    )(o1, o2, da_lambda, da_norm.reshape(1, -1))


def _gqa_kernel(*refs, mode):
    if mode == "fox":
        q_ref, k_ref, v_ref, cq_ref, ck_ref, o_ref, m_ref, l_ref, acc_ref = refs
    elif mode == "sel":
        rel_ref, bt_ref, q_ref, k_ref, v_ref, memb_ref, o_ref, m_ref, l_ref, acc_ref = refs
    else:
        bt_ref, q_ref, k_ref, v_ref, oc_ref, os_ref, gate_ref, o_ref, m_ref, l_ref, acc_ref = refs
    g = pl.program_id(1)
    i = pl.program_id(2)
    jd = lax.shift_right_logical(i, 1)
    t0 = lax.bitwise_and(i, 1)
    nr = q_ref.shape[1]
    qs = q_ref[0].reshape(nr * BQ, HEAD_DIM) * jnp.asarray(SCALE, BF16)
    _init_state(m_ref, l_ref, acc_ref)

    def tile(j, tt, diag):
        off = pl.multiple_of(j * BK, BK)
        s = _dot_nt(qs, k_ref[0, 0, pl.ds(off, BK), :])
        if mode == "fox":
            ck = ck_ref[0, 0, :, pl.ds(off, BK)]
            parts = []
            for r in range(nr):
                sr = s[r * BQ:(r + 1) * BQ] + (cq_ref[0, 0, :, r:r + 1] - ck[r:r + 1, :])
                parts.append(sr)
            s = jnp.concatenate(parts, 0)
            if diag:
                ii = lax.broadcasted_iota(I32, (BQ, BK), 0)
                jj = lax.broadcasted_iota(I32, (BQ, BK), 1)
                ok = (t0 * BQ + ii - jj) >= 0
                s = jnp.where(jnp.concatenate([ok] * nr, 0), s, NEG_INF)
        else:
            extra = None
            if mode == "sel":
                jb = lax.broadcasted_iota(I32, (memb_ref.shape[3], BK), 0)
                kk = lax.broadcasted_iota(I32, (memb_ref.shape[3], BK), 1)
                e = (jb == j * (BK // SEL_BLOCK) + lax.shift_right_logical(kk, 6)).astype(BF16)
                extra = (_dot(memb_ref[0, 0], e) - 1.0) * (-NEG_INF)
            parts = []
            for r in range(nr):
                if tt is None:
                    br = rel_ref[N_BUCKETS - 1, DA_HEADS + g * nr + r]
                else:
                    br = bt_ref[r, tt]
                if extra is not None:
                    br = br + extra
                parts.append(s[r * BQ:(r + 1) * BQ] + br)
            s = jnp.concatenate(parts, 0)
        _online_update(s, v_ref[0, 0, pl.ds(off, BK), :], m_ref, l_ref, acc_ref)

    if mode in ("fox", "sel"):
        n_far = jd if mode == "fox" else jnp.maximum(jd - 1, 0)

        def far_body(j, c):
            tile(j, None, False)
            return c

        lax.fori_loop(0, n_far, far_body, 0)
    if mode == "win":
        @pl.when(jd >= 2)
        def _():
            tile(jd - 2, t0 + 4, False)
    if mode in ("sel", "win"):
        @pl.when(jd >= 1)
        def _():
            tile(jd - 1, t0 + 2, False)
    tile(jd, t0, True)

    o = acc_ref[...] / l_ref[...]
    if mode == "fox":
        o_ref[...] = jnp.concatenate([o[r * BQ:(r + 1) * BQ] for r in range(nr)], 1).astype(o_ref.dtype)
    elif mode == "sel":
        o_ref[0] = o.reshape(nr, BQ, HEAD_DIM)
    else:
        gt = jax.nn.sigmoid(gate_ref[0, 0])
        parts = []
        for r in range(nr):
            comb = (gt[:, 3 * r:3 * r + 1] * oc_ref[0, r] + gt[:, 3 * r + 1:3 * r + 2] * os_ref[0, r]
                    + gt[:, 3 * r + 2:3 * r + 3] * o[r * BQ:(r + 1) * BQ])
            parts.append(comb)
        o_ref[...] = jnp.concatenate(parts, 1).astype(o_ref.dtype)


def _gqa_prompt(mode, q, k, v, **kw):
    b, hq, t, _ = q.shape
    hk = k.shape[1]
    nr = hq // hk
    nq = t // BQ
    q_spec = pl.BlockSpec((1, nr, BQ, HEAD_DIM), lambda bi, g, i: (bi, g, i, 0))
    kv_spec = pl.BlockSpec((1, 1, t, HEAD_DIM), lambda bi, g, i: (bi, g, 0, 0))
    smem = pl.BlockSpec(memory_space=pltpu.SMEM)
    row_out = pl.BlockSpec((BQ, nr * HEAD_DIM), lambda bi, g, i: (bi * nq + i, g))
    if mode == "fox":
        in_specs = [q_spec, kv_spec, kv_spec,
                    pl.BlockSpec((1, 1, BQ, nr), lambda bi, g, i: (bi, g, i, 0)),
                    pl.BlockSpec((1, 1, nr, t), lambda bi, g, i: (bi, g, 0, 0))]
        args = (q, k, v, kw["cq"], kw["ck"])
        out_shape = jax.ShapeDtypeStruct((b * t, hq * HEAD_DIM), BF16)
        out_spec = row_out
    elif mode == "sel":
        nbs = kw["memb"].shape[3]
        in_specs = [smem, pl.BlockSpec((nr, 4, BQ, BK), lambda bi, g, i: (1 + g, 0, 0, 0)),
                    q_spec, kv_spec, kv_spec,
                    pl.BlockSpec((1, 1, BQ, nbs), lambda bi, g, i: (bi, g, i, 0))]
        args = (kw["rel_bias"], kw["bt"], q, k, v, kw["memb"])
        out_shape = jax.ShapeDtypeStruct((b, hq, t, HEAD_DIM), F32)
        out_spec = q_spec
    else:
        in_specs = [pl.BlockSpec((nr, 6, BQ, BK), lambda bi, g, i: (g, 0, 0, 0)),
                    q_spec, kv_spec, kv_spec, q_spec, q_spec,
                    pl.BlockSpec((1, 1, BQ, 3 * nr), lambda bi, g, i: (bi, g, i, 0))]
        args = (kw["bt"], q, k, v, kw["oc"], kw["os"], kw["gate"])
        out_shape = jax.ShapeDtypeStruct((b * t, hq * HEAD_DIM), BF16)
        out_spec = row_out
    return pl.pallas_call(
        functools.partial(_gqa_kernel, mode=mode), name="gqa_" + mode,
        out_shape=out_shape, grid=(b, hk, nq), in_specs=in_specs, out_specs=out_spec,
        scratch_shapes=[pltpu.VMEM((nr * BQ, 1), F32), pltpu.VMEM((nr * BQ, 1), F32),
                        pltpu.VMEM((nr * BQ, HEAD_DIM), F32)],
        compiler_params=_params(("parallel", "parallel", "arbitrary")),
    )(*args)


def _cmp_weight(w1):
    w1r = w1.reshape(2, 2, CMP_STRIDE, HEAD_DIM, CMP_HIDDEN)
    w1c = w1r[jnp.array([0, 0, 1, 1])]
    wb = jnp.einsum("chrdn,ce->rcdehn", w1c, jnp.eye(4, dtype=w1.dtype))
    return wb.reshape(CMP_STRIDE * 4 * HEAD_DIM, 4 * 2 * CMP_HIDDEN).astype(BF16)


def _cmp_finish_kernel(part_ref, pos_ref, w1_ref, w2_ref, kc_ref, vc_ref):
    nch = part_ref.shape[1]
    row = lax.broadcasted_iota(I32, (nch, HEAD_DIM), 0)
    for kv in range(2):
        posw = _dot(pos_ref[kv], w1_ref[kv])[0:1]
        for g in range(NSA_GROUPS):
            c = kv * NSA_GROUPS + g
            a = part_ref[0, :, 256 * c:256 * c + 128]
            bm = part_ref[0, :, 256 * c + 128:256 * c + 256]
            hid = a + pltpu.roll(bm, nch - 1, 0) + posw
            y = _dot(jax.nn.gelu(hid).astype(BF16), w2_ref[kv])
            y = jnp.where(row < nch - 1, y, 0.0).astype(BF16)
            if kv == 0:
                kc_ref[0, g] = y
            else:
                vc_ref[0, g] = y


def _cmp_finish(part, cmp_pos, w1, w2):
    b, nch, _ = part.shape
    pos8 = jnp.broadcast_to(cmp_pos.reshape(2, 1, CMP_BLOCK * HEAD_DIM), (2, 8, CMP_BLOCK * HEAD_DIM)).astype(BF16)
    out = jax.ShapeDtypeStruct((b, NSA_GROUPS, nch, HEAD_DIM), BF16)
    o_spec = pl.BlockSpec((1, NSA_GROUPS, nch, HEAD_DIM), lambda bi: (bi, 0, 0, 0))
    return pl.pallas_call(
        _cmp_finish_kernel, name="cmp_finish",
        out_shape=(out, out), grid=(b,),
        in_specs=[pl.BlockSpec((1, nch, 1024), lambda bi: (bi, 0, 0)), _const_spec(pos8.shape),
                  _const_spec(w1.shape), _const_spec(w2.shape)],
        out_specs=(o_spec, o_spec),
        compiler_params=_params(("parallel",)),
    )(part, pos8, w1.astype(BF16), w2.astype(BF16))


def _paged_mm_kernel(pt_ref, *refs, npg):
    pages = refs[:npg]
    w_ref, o_ref = refs[npg:]
    x = jnp.concatenate([p[0] for p in pages], 0).astype(BF16)
    o_ref[...] = _dot(x, w_ref[...])


def _paged_mm(pool, pt_flat, w, npg=16):
    n = pt_flat.shape[0]
    _, r, k = pool.shape
    c = w.shape[1]
    npg = math.gcd(npg, n)
    specs = [pl.BlockSpec((1, r, k), (lambda s, pt, p=p: (pt[s * npg + p], 0, 0))) for p in range(npg)]
    grid_spec = pltpu.PrefetchScalarGridSpec(
        num_scalar_prefetch=1, grid=(n // npg,),
        in_specs=specs + [pl.BlockSpec((k, c), lambda s, pt: (0, 0), pipeline_mode=pl.Buffered(1))],
        out_specs=pl.BlockSpec((npg * r, c), lambda s, pt: (s, 0)))
    return pl.pallas_call(
        functools.partial(_paged_mm_kernel, npg=npg), name="paged_mm",
        out_shape=jax.ShapeDtypeStruct((n * r, c), F32), grid_spec=grid_spec,
        compiler_params=_params(("parallel",)),
    )(pt_flat, *([pool] * npg), w)


def _overlap_t(nbs, ncp):
    jb = lax.broadcasted_iota(I32, (nbs, ncp), 0) * SEL_BLOCK
    ci = lax.broadcasted_iota(I32, (nbs, ncp), 1) * CMP_STRIDE
    return ((ci < jb + SEL_BLOCK) & (ci + CMP_BLOCK > jb)).astype(BF16)


def _cmp_prompt_kernel(rel_ref, q_ref, kc_ref, vc_ref, oc_ref, memb_ref, *, nbs, k_eff):
    g = pl.program_id(1)
    i = pl.program_id(2)
    nr = q_ref.shape[1]
    ncp = kc_ref.shape[2]
    qs = q_ref[0].reshape(nr * BQ, HEAD_DIM) * jnp.asarray(SCALE, BF16)
    s = _dot_nt(qs, kc_ref[0, 0])
    pos = i * BQ + lax.broadcasted_iota(I32, (BQ, ncp), 0)
    d = pos - (lax.broadcasted_iota(I32, (BQ, ncp), 1) * CMP_STRIDE + CMP_BLOCK - 1)
    ok = d >= 0
    okf = ok.astype(F32)
    p_parts = []
    for r in range(nr):
        h = DA_HEADS + g * nr + r
        sr = jnp.where(ok, s[r * BQ:(r + 1) * BQ] + _t5_bias(d, lambda b: rel_ref[b, h]), NEG_INF)
        e = jnp.exp(sr - jnp.max(sr, -1, keepdims=True))
        p_parts.append(e / jnp.sum(e, -1, keepdims=True) * okf)
    p = jnp.concatenate(p_parts, 0)
    oc_ref[0] = _dot(p.astype(BF16), vc_ref[0, 0]).reshape(nr, BQ, HEAD_DIM)
    psum = p_parts[0]
    for r in range(1, nr):
        psum = psum + p_parts[r]
    ot = _overlap_t(nbs, ncp)
    imp = None
    for part in _split3(psum):
        t = _dot_nt(ot, part)
        imp = t if imp is None else imp + t
    jrow = lax.broadcasted_iota(I32, (nbs, BQ), 0)
    cur = lax.shift_right_logical(i * BQ + lax.broadcasted_iota(I32, (nbs, BQ), 1), 6)
    valid = jrow <= cur
    forced = valid & ((jrow == 0) | (jrow >= cur - 1))
    score = jnp.where(valid, imp + jnp.where(forced, FORCED_BONUS, 0.0), -1.0)
    rank = jnp.zeros((nbs, BQ), I32)
    for ii in range(nbs):
        row = score[ii:ii + 1, :]
        ahead = (row > score) | ((row == score) & (jrow > ii))
        rank = rank + ahead.astype(I32)
    sel_t = ((rank < k_eff) & (score >= 0.0)).astype(BF16)
    eye = (lax.broadcasted_iota(I32, (BQ, BQ), 0) == lax.broadcasted_iota(I32, (BQ, BQ), 1)).astype(BF16)
    memb_ref[0, 0] = _dot_nt(eye, sel_t).astype(memb_ref.dtype)


def _cmp_prompt(q, kc, vc, rel_bias):
    b, hq, t, _ = q.shape
    nr = hq // NSA_GROUPS
    nq = t // BQ
    ncp = kc.shape[2]
    nbs = -(-t // SEL_BLOCK)
    q_spec = pl.BlockSpec((1, nr, BQ, HEAD_DIM), lambda bi, g, i: (bi, g, i, 0))
    c_spec = pl.BlockSpec((1, 1, ncp, HEAD_DIM), lambda bi, g, i: (bi, g, 0, 0))
    return pl.pallas_call(
        functools.partial(_cmp_prompt_kernel, nbs=nbs, k_eff=min(SEL_TOPK, nbs)), name="cmp_prompt",
        out_shape=(jax.ShapeDtypeStruct((b, hq, t, HEAD_DIM), F32),
                   jax.ShapeDtypeStruct((b, NSA_GROUPS, t, nbs), BF16)),
        grid=(b, NSA_GROUPS, nq),
        in_specs=[pl.BlockSpec(memory_space=pltpu.SMEM), q_spec, c_spec, c_spec],
        out_specs=(q_spec, pl.BlockSpec((1, 1, BQ, nbs), lambda bi, g, i: (bi, g, i, 0))),
        compiler_params=_params(("parallel", "parallel", "parallel")),
    )(rel_bias, q, kc, vc)


def _cmp_sample_kernel(q_ref, rowtbl_ref, kc_ref, vc_ref, oc_ref, memb_ref, *, qpos0, nbs, k_eff):
    ncp = kc_ref.shape[2]
    mb = memb_ref.shape[2]
    rows = NSA_R * 8
    qi = lax.bitwise_and(lax.broadcasted_iota(I32, (rows, ncp), 0), 7)
    d = qpos0 + qi - (lax.broadcasted_iota(I32, (rows, ncp), 1) * CMP_STRIDE + CMP_BLOCK - 1)
    ok = d >= 0
    jb = lax.broadcasted_iota(I32, (ncp, mb), 1) * SEL_BLOCK
    ci = lax.broadcasted_iota(I32, (ncp, mb), 0) * CMP_STRIDE
    ov = ((ci < jb + SEL_BLOCK) & (ci + CMP_BLOCK > jb)).astype(BF16)
    jl = lax.broadcasted_iota(I32, (8, mb), 1)
    cur = lax.shift_right_logical(qpos0 + lax.broadcasted_iota(I32, (8, mb), 0), 6)
    valid = (jl <= cur) & (jl < nbs)
    forced = valid & ((jl == 0) | (jl >= cur - 1))
    for g in range(NSA_GROUPS):
        s = _dot_nt(q_ref[0, g * rows:(g + 1) * rows], kc_ref[0, g])
        tbl = rowtbl_ref[g * rows:(g + 1) * rows]
        sm = jnp.where(ok, s + _t5_bias(d, lambda b: tbl[:, b:b + 1]), NEG_INF)
        e = jnp.exp(sm - jnp.max(sm, -1, keepdims=True))
        p = e / jnp.sum(e, -1, keepdims=True) * ok.astype(F32)
        oc_ref[0, g * rows:(g + 1) * rows] = _dot(p.astype(BF16), vc_ref[0, g])
        psum = p[0:8]
        for r in range(1, NSA_R):
            psum = psum + p[8 * r:8 * r + 8]
        imp = None
        for part in _split3(psum):
            t = _dot(part, ov)
            imp = t if imp is None else imp + t
        score = jnp.where(valid, imp + jnp.where(forced, FORCED_BONUS, 0.0), -1.0)
        score = jnp.where(jl < nbs, score, -2.0)
        rank = jnp.zeros((8, mb), I32)
        for ii in range(nbs):
            col = score[:, ii:ii + 1]
            ahead = (col > score) | ((col == score) & (jl > ii))
            rank = rank + ahead.astype(I32)
        memb_ref[0, g * 8:(g + 1) * 8] = ((rank < k_eff) & (score >= 0.0)).astype(F32)


def _cmp_sample(qrows, rowtbl, kc, vc, qpos0, nbs, mb):
    b = qrows.shape[0]
    ncp = kc.shape[2]
    nrows = NSA_GROUPS * NSA_R * 8
    c_spec = pl.BlockSpec((1, NSA_GROUPS, ncp, HEAD_DIM), lambda bi: (bi, 0, 0, 0))
    return pl.pallas_call(
        functools.partial(_cmp_sample_kernel, qpos0=qpos0, nbs=nbs, k_eff=min(SEL_TOPK, nbs)),
        name="cmp_sample",
        out_shape=(jax.ShapeDtypeStruct((b, nrows, HEAD_DIM), F32),
                   jax.ShapeDtypeStruct((b, NSA_GROUPS * 8, mb), F32)),
        grid=(b,),
        in_specs=[pl.BlockSpec((1, nrows, HEAD_DIM), lambda bi: (bi, 0, 0)), _const_spec(rowtbl.shape),
                  c_spec, c_spec],
        out_specs=(pl.BlockSpec((1, nrows, HEAD_DIM), lambda bi: (bi, 0, 0)),
                   pl.BlockSpec((1, NSA_GROUPS * 8, mb), lambda bi: (bi, 0, 0))),
        compiler_params=_params(("parallel",)),
    )(qrows, rowtbl, kc, vc)


def _decode_kernel(pt_ref, *refs, mode, npg, n_pages, kpos0, qpos0):
    q_ref, rowq_ref = refs[0], refs[1]
    k = 2
    if mode == "fox":
        cq_ref, ck_ref, cktail_ref = refs[k:k + 3]
        k += 3
    else:
        rowtbl_ref = refs[k]
        k += 1
        if mode == "sel":
            memb_ref = refs[k]
            k += 1
    pages = refs[k:k + npg]
    tail_ref, o_ref, m_ref, l_ref, acc_ref = refs[k + npg:]
    step = pl.program_id(1)
    nr = q_ref.shape[1]
    q = q_ref[0]
    rowpos = qpos0 + rowq_ref[...]
    lane = lax.broadcasted_iota(I32, (nr, PAGE), 1)

    @pl.when(step == 0)
    def _():
        _init_state(m_ref, l_ref, acc_ref)

    def do_page(page, kp0, blk0, ck16):
        kb = page.astype(BF16)
        s = _dot_nt(q, kb)
        d = rowpos - (kp0 + lane)
        bad = d < 0
        if mode == "fox":
            s = s + (cq_ref[0] - jnp.concatenate([ck16] * (nr // FOX_HEADS), 0))
        else:
            tbl = rowtbl_ref[...]
            s = s + _t5_bias(d, lambda b: tbl[:, b:b + 1])
            if mode == "win":
                bad = bad | (d >= WINDOW)
            if mode == "sel":
                mb = memb_ref.shape[2]
                jb = lax.broadcasted_iota(I32, (mb, PAGE), 0)
                kk = lax.broadcasted_iota(I32, (mb, PAGE), 1)
                e = (jb == blk0 + lax.shift_right_logical(kk, 6)).astype(BF16)
                bad = bad | (_dot(memb_ref[0], e) < 0.5)
        s = jnp.where(bad, NEG_INF, s)
        _online_update(s, kb, m_ref, l_ref, acc_ref)

    for p in range(npg):
        pg = step * npg + p
        ck16 = ck_ref[0, :, p * PAGE:(p + 1) * PAGE] if mode == "fox" else None
        do_page(pages[p][0], kpos0 + pg * PAGE, pg * (PAGE // SEL_BLOCK), ck16)

    @pl.when(step == pl.num_programs(1) - 1)
    def _():
        do_page(tail_ref[0], qpos0, n_pages * (PAGE // SEL_BLOCK), cktail_ref[0] if mode == "fox" else None)
        o_ref[0] = acc_ref[...] / l_ref[...]


def _decode(mode, qm, rowq, pool, pt_flat, tail, n_pages, npg, kpos0, qpos0, **kw):
    b, nr, w = qm.shape
    n_steps = n_pages // npg
    in_specs = [pl.BlockSpec((1, nr, w), lambda bi, s, pt: (bi, 0, 0)),
                pl.BlockSpec((nr, 1), lambda bi, s, pt: (0, 0))]
    args = [qm, rowq]
    if mode == "fox":
        in_specs += [pl.BlockSpec((1, nr, 1), lambda bi, s, pt: (bi, 0, 0)),
                     pl.BlockSpec((1, FOX_HEADS, npg * PAGE), lambda bi, s, pt: (bi, 0, s)),
                     pl.BlockSpec((1, FOX_HEADS, PAGE), lambda bi, s, pt: (bi, 0, n_pages))]
        args += [kw["cq"], kw["ck"], kw["ck"]]
    else:
        in_specs.append(pl.BlockSpec((nr, N_BUCKETS), lambda bi, s, pt: (0, 0)))
        args.append(kw["rowtbl"])
        if mode == "sel":
            mb = kw["memb"].shape[2]
            in_specs.append(pl.BlockSpec((1, nr, mb), lambda bi, s, pt: (bi, 0, 0)))
            args.append(kw["memb"])
    in_specs += [pl.BlockSpec((1, PAGE, w), (lambda bi, s, pt, p=p: (pt[bi * n_pages + s * npg + p], 0, 0)))
                 for p in range(npg)]
    args += [pool] * npg
    in_specs.append(pl.BlockSpec((1, PAGE, w), lambda bi, s, pt: (bi, 0, 0)))
    args.append(tail)
    grid_spec = pltpu.PrefetchScalarGridSpec(
        num_scalar_prefetch=1, grid=(b, n_steps), in_specs=in_specs,
        out_specs=pl.BlockSpec((1, nr, w), lambda bi, s, pt: (bi, 0, 0)),
        scratch_shapes=[pltpu.VMEM((nr, 1), F32), pltpu.VMEM((nr, 1), F32), pltpu.VMEM((nr, w), F32)])
    return pl.pallas_call(
        functools.partial(_decode_kernel, mode=mode, npg=npg, n_pages=n_pages, kpos0=kpos0, qpos0=qpos0),
        name="decode_" + mode,
        out_shape=jax.ShapeDtypeStruct((b, nr, w), F32), grid_spec=grid_spec,
        compiler_params=_params(("parallel", "arbitrary")),
    )(pt_flat, *args)


def _nsa_combine_kernel(oc_ref, os_ref, ow_ref, gate_ref, o_ref):
    gt = jax.nn.sigmoid(gate_ref[...])
    o = gt[:, 0:1] * oc_ref[...] + gt[:, 1:2] * os_ref[...] + gt[:, 2:3] * ow_ref[...]
    o_ref[...] = o.astype(o_ref.dtype)


def _nsa_combine(oc, os_, ow, gate):
    return pl.pallas_call(
        _nsa_combine_kernel, name="nsa_combine",
        out_shape=jax.ShapeDtypeStruct(oc.shape, BF16),
    )(oc, os_, ow, gate)


def _cumsum_kernel(pt_ref, bf_ref, cin_ref, *refs, npg, apply_logsig):
    pages = refs[:npg]
    lf_ref, c_ref, carry_ref = refs[npg:]

    @pl.when(pl.program_id(1) == 0)
    def _():
        carry_ref[...] = cin_ref[0]

    tri = (lax.broadcasted_iota(I32, (PAGE, PAGE), 0) >= lax.broadcasted_iota(I32, (PAGE, PAGE), 1)).astype(BF16)
    car = carry_ref[...]
    for p in range(npg):
        x = pages[p][0]
        if apply_logsig:
            z = x + bf_ref[...]
            x = jnp.minimum(z, 0.0) - jnp.log1p(jnp.exp(-jnp.abs(z)))
        lf_ref[0, p * PAGE:(p + 1) * PAGE] = x
        cum = car
        for part in _split3(x):
            cum = cum + _dot(tri, part)
        c_ref[0, p * PAGE:(p + 1) * PAGE] = cum
        car = cum[PAGE - 1:PAGE]
    carry_ref[...] = car


def _cumsum_pages(pool, pt_flat, b, n_pages, b_f, carry_in, apply_logsig, npg=8):
    h = pool.shape[2]
    npg = math.gcd(npg, n_pages)
    in_specs = [pl.BlockSpec((1, h), lambda bi, s, pt: (0, 0)),
                pl.BlockSpec((1, 1, h), lambda bi, s, pt: (bi, 0, 0))]
    in_specs += [pl.BlockSpec((1, PAGE, h), (lambda bi, s, pt, p=p: (pt[bi * n_pages + s * npg + p], 0, 0)))
                 for p in range(npg)]
    o_spec = pl.BlockSpec((1, npg * PAGE, h), lambda bi, s, pt: (bi, s, 0))
    out = jax.ShapeDtypeStruct((b, n_pages * PAGE, h), F32)
    grid_spec = pltpu.PrefetchScalarGridSpec(
        num_scalar_prefetch=1, grid=(b, n_pages // npg), in_specs=in_specs, out_specs=(o_spec, o_spec),
        scratch_shapes=[pltpu.VMEM((1, h), F32)])
    return pl.pallas_call(
        functools.partial(_cumsum_kernel, npg=npg, apply_logsig=apply_logsig), name="cumsum_pages",
        out_shape=(out, out), grid_spec=grid_spec,
        compiler_params=_params(("parallel", "arbitrary")),
    )(pt_flat, b_f.reshape(1, h), carry_in, *([pool] * npg))


def _router_kernel(x_ref, w_ref, o_ref):
    x = x_ref[...]
    w = w_ref[...]
    xh = x.astype(BF16)
    xl = (x - xh.astype(F32)).astype(BF16)
    wh = w.astype(BF16)
    wl = (w - wh.astype(F32)).astype(BF16)
    logits = _dot(xh, wh) + _dot(xl, wh) + _dot(xh, wl)
    lane = lax.broadcasted_iota(I32, logits.shape, 1)
    big = logits.shape[1]
    lg = jnp.where(lane < N_EXPERTS, logits, -jnp.inf)
    m1 = jnp.max(lg, -1, keepdims=True)
    i1 = jnp.min(jnp.where(lg == m1, lane, big), -1, keepdims=True)
    lg2 = jnp.where(lane == i1, -jnp.inf, lg)
    m2 = jnp.max(lg2, -1, keepdims=True)
    i2 = jnp.min(jnp.where(lg2 == m2, lane, big), -1, keepdims=True)
    e2 = jnp.exp(m2 - m1)
    g1 = 1.0 / (1.0 + e2)
    g2 = e2 / (1.0 + e2)
    out = jnp.where(lane == 0, i1.astype(F32), jnp.where(lane == 1, i2.astype(F32),
                    jnp.where(lane == 2, g1, jnp.where(lane == 3, g2, 0.0))))
    o_ref[...] = out


def _router(x, w_router):
    n, dm = x.shape
    tm = _row_tile(n)
    wp = jnp.zeros((dm, 128), F32).at[:, :N_EXPERTS].set(w_router)
    return pl.pallas_call(
        _router_kernel, name="router",
        out_shape=jax.ShapeDtypeStruct((n, 128), F32),
        grid=(n // tm,),
        in_specs=[pl.BlockSpec((tm, dm), lambda i: (i, 0)), _const_spec((dm, 128))],
        out_specs=pl.BlockSpec((tm, 128), lambda i: (i, 0)),
        compiler_params=_params(("parallel",)),
    )(x, wp)


def _expert_kernel(be_ref, nb_ref, x_ref, wg_ref, wu_ref, wd_ref, o_ref):
    i = pl.program_id(0)

    @pl.when(i < nb_ref[0])
    def _():
        xb = x_ref[...]
        h = jax.nn.silu(_dot(xb, wg_ref[0])) * _dot(xb, wu_ref[0])
        o_ref[...] = _dot(h.astype(BF16), wd_ref[0])

    @pl.when(i >= nb_ref[0])
    def _():
        o_ref[...] = jnp.zeros(o_ref.shape, o_ref.dtype)


def _experts(buf, block_e, n_used, wg, wu, wd):
    nrow, dm = buf.shape
    nb = nrow // MOE_BLOCK
    dff = wg.shape[2]
    grid_spec = pltpu.PrefetchScalarGridSpec(
        num_scalar_prefetch=2, grid=(nb,),
        in_specs=[pl.BlockSpec((MOE_BLOCK, dm), lambda i, be, nu: (i, 0)),
                  pl.BlockSpec((1, dm, dff), lambda i, be, nu: (be[i], 0, 0)),
                  pl.BlockSpec((1, dm, dff), lambda i, be, nu: (be[i], 0, 0)),
                  pl.BlockSpec((1, dff, dm), lambda i, be, nu: (be[i], 0, 0))],
        out_specs=pl.BlockSpec((MOE_BLOCK, dm), lambda i, be, nu: (i, 0)))
    return pl.pallas_call(
        _expert_kernel, name="experts",
        out_shape=jax.ShapeDtypeStruct((nrow, dm), F32), grid_spec=grid_spec,
        compiler_params=_params(("arbitrary",)),
    )(block_e, n_used, buf, wg, wu, wd)


def _moe(x, w_router, wg, wu, wd):
    n, dm = x.shape
    route = _router(x, w_router)
    top_e = route[:, 0:2].astype(I32)
    gate = route[:, 2:4]
    n_assign = n * TOP_K
    flat_e = top_e.reshape(-1)
    order = jnp.argsort(flat_e)
    e_sorted = flat_e[order]
    tok_sorted = order // TOP_K
    g_sorted = gate.reshape(-1)[order]
    counts = jnp.bincount(flat_e, length=N_EXPERTS)
    padded = (counts + MOE_BLOCK - 1) // MOE_BLOCK * MOE_BLOCK
    start = jnp.cumsum(counts) - counts
    pad_end = jnp.cumsum(padded)
    pad_start = pad_end - padded
    dest = pad_start[e_sorted] + jnp.arange(n_assign) - start[e_sorted]
    n_blocks = -(-n_assign // MOE_BLOCK) + N_EXPERTS
    block_e = jnp.minimum(jnp.searchsorted(pad_end, jnp.arange(n_blocks) * MOE_BLOCK, side="right"),
                          N_EXPERTS - 1).astype(I32)
    n_used = (pad_end[-1] // MOE_BLOCK).astype(I32).reshape(1)
    buf = jnp.zeros((n_blocks * MOE_BLOCK, dm), BF16).at[dest].set(x.astype(BF16)[tok_sorted])
    out = _experts(buf, block_e, n_used, wg.astype(BF16), wu.astype(BF16), wd.astype(BF16))
    y_assign = out[dest] * g_sorted[:, None]
    return jnp.zeros((n, dm), F32).at[tok_sorted].add(y_assign)


C0_QDA, C0_DIFF, C0_QNS, C0_CMP, C0_SEL, C0_WIN, C0_GATE, C0 = 0, 512, 1536, 2048, 2304, 2560, 2816, 2944
C1_Q, C1_KV, C1_F, C1 = 0, 1024, 1536, 1664


def _l0_weight(w_in):
    k_off, v_off = 512, 1024
    cols = list(range(0, 512))
    for h in range(DA_HEADS):
        cols += list(range(k_off + 128 * h, k_off + 128 * (h + 1)))
        cols += list(range(v_off + 128 * h, v_off + 128 * (h + 1)))
    cols += list(range(1536, 2840))
    w = w_in[:, np.asarray(cols)]
    return jnp.pad(w, ((0, 0), (0, C0 - w.shape[1]))).astype(BF16)


def _pad_rows(x, rows):
    return jnp.pad(x, ((0, 0), (0, rows - x.shape[1])) + ((0, 0),) * (x.ndim - 2))


def kernel(x_prompt, x_sample, cache_diff_kv, cache_nsa_cmp, cache_nsa_sel, state_nsa_win, cache_fox_kv,
           cache_fox_logf, page_table, rel_bias, l0_w_in, l0_w_out, da_lambda, da_norm, nsa_cmp_pos,
           nsa_cmp_w1, nsa_cmp_w2, ffn_w_gate, ffn_w_up, ffn_w_down, l1_w_in, fox_b_f, l1_w_out,
           moe_router, moe_w_gate, moe_w_up, moe_w_down, ln_g, ln_b):
    b, t, dm = x_prompt.shape
    bs, ts, _ = x_sample.shape
    n_pages = page_table.shape[1]
    n_phys = cache_diff_kv.shape[0]
    past = n_pages * PAGE
    np_, ns = b * t, bs * ts
    pt_flat = page_table.reshape(-1).astype(I32)
    x0 = jnp.concatenate([x_prompt.reshape(np_, dm), x_sample.reshape(ns, dm)], 0)

    p0, p0b = _proj(x0, _l0_weight(l0_w_in))
    bt_c = _bias_tiles(rel_bias, 4, False, 0, DA_HEADS + NSA_HEADS)
    bt_w = _bias_tiles(rel_bias, 6, True, DA_HEADS, NSA_HEADS)
    rel_t = rel_bias.T

    o_da_p = _diff_prompt(p0b, rel_bias, da_lambda, da_norm, bt_c, b, t, C0_QDA // 128, C0_DIFF // 128)

    def head_major(cols, width, heads):
        return p0b[:np_, cols:cols + width].reshape(b, t, heads, HEAD_DIM).transpose(0, 2, 1, 3)

    q_ns_p = head_major(C0_QNS, 512, NSA_HEADS)
    wbig = _cmp_weight(nsa_cmp_w1)
    part_p = _rowmm(p0[:np_, C0_CMP:C0_CMP + 256].reshape(np_ // CMP_STRIDE, CMP_STRIDE * 256), wbig)
    kc_p, vc_p = _cmp_finish(part_p.reshape(b, t // CMP_STRIDE, 1024), nsa_cmp_pos, nsa_cmp_w1, nsa_cmp_w2)
    oc_p, memb_p = _cmp_prompt(q_ns_p, kc_p, vc_p, rel_bias)
    sel_p = head_major(C0_SEL, 256, 2 * NSA_GROUPS)
    os_p = _gqa_prompt("sel", q_ns_p, sel_p[:, :NSA_GROUPS], sel_p[:, NSA_GROUPS:],
                       rel_bias=rel_bias, bt=bt_c, memb=memb_p)
    win_p = head_major(C0_WIN, 256, 2 * NSA_GROUPS)
    gate_p = p0[:np_, C0_GATE:C0_GATE + 3 * NSA_HEADS].reshape(b, t, NSA_GROUPS, 3 * NSA_R).transpose(0, 2, 1, 3)
    o_ns_p = _gqa_prompt("win", q_ns_p, win_p[:, :NSA_GROUPS], win_p[:, NSA_GROUPS:],
                         bt=bt_w, oc=oc_p, os=os_p, gate=gate_p)

    qpos0 = past
    q_idx = np.arange(ts)
    qd = p0b[np_:, C0_QDA:C0_QDA + 512].reshape(bs, ts, DA_HEADS, 2, HEAD_DIM).transpose(0, 3, 2, 1, 4)
    slot_w = jnp.asarray(np.eye(4)[:2], BF16)
    qm_d = jnp.einsum("bwhqd,hH,ws->bwhqHsd", qd, jnp.eye(DA_HEADS, dtype=BF16), slot_w)
    qm_d = (qm_d * jnp.asarray(SCALE, BF16)).reshape(bs, 2 * DA_HEADS * ts, DA_HEADS * 4 * HEAD_DIM)
    rows_d = [(w, h, q) for w in range(2) for h in range(DA_HEADS) for q in range(ts)]
    rowq_d = jnp.asarray(np.array([r[2] for r in rows_d], np.int32).reshape(-1, 1))
    rowtbl_d = rel_t[np.array([r[1] for r in rows_d])]
    tail_d = _pad_rows(p0[np_:, C0_DIFF:C0_DIFF + 1024].reshape(bs, ts, 1024), PAGE)
    acc_d = _decode("diff", qm_d, rowq_d, cache_diff_kv.reshape(n_phys, PAGE, 1024), pt_flat, tail_d,
                    n_pages, 8, 0, qpos0, rowtbl=rowtbl_d)
    acc_d = acc_d.reshape(bs, 2, DA_HEADS, ts, DA_HEADS, 2, 2 * HEAD_DIM)
    o12 = jnp.einsum("bwhqHsd,hH->bwhqsd", acc_d, jnp.eye(DA_HEADS, dtype=F32))[:, :, :, :, 1]
    o_da_s = _diff_finish(o12[:, 0].reshape(bs * DA_HEADS * ts, 128), o12[:, 1].reshape(bs * DA_HEADS * ts, 128),
                          da_lambda, da_norm)
    o_da_s = o_da_s.reshape(bs, DA_HEADS, ts, 128).transpose(0, 2, 1, 3).reshape(ns, 512)

    part_s = _paged_mm(cache_nsa_cmp.reshape(n_phys, PAGE // CMP_STRIDE, CMP_STRIDE * 256), pt_flat, wbig)
    l_tot = past + ts
    nch_s = l_tot // CMP_STRIDE
    part_s = part_s.reshape(bs, past // CMP_STRIDE, 1024)
    if nch_s > past // CMP_STRIDE:
        raise NotImplementedError("new rows completing a compression chunk")
    kc_s, vc_s = _cmp_finish(part_s, nsa_cmp_pos, nsa_cmp_w1, nsa_cmp_w2)
    nbs_s = -(-l_tot // SEL_BLOCK)
    mb_s = -(-nbs_s // 128) * 128
    qn = p0b[np_:, C0_QNS:C0_QNS + 512].reshape(bs, ts, NSA_GROUPS, NSA_R, HEAD_DIM)
    qn = qn * jnp.asarray(SCALE, BF16)
    q_cmp = _pad_rows(qn.transpose(0, 2, 3, 1, 4).reshape(bs, NSA_HEADS, ts, HEAD_DIM).transpose(0, 2, 1, 3), 8)
    q_cmp = q_cmp.transpose(0, 2, 1, 3).reshape(bs, NSA_HEADS * 8, HEAD_DIM)
    rowtbl_c = rel_t[DA_HEADS + np.repeat(np.arange(NSA_HEADS), 8)]
    oc_s, memb_s = _cmp_sample(q_cmp, rowtbl_c, kc_s, vc_s, qpos0, nbs_s, mb_s)
    oc_s = oc_s.reshape(bs, NSA_HEADS, 8, HEAD_DIM)[:, :, :ts].transpose(0, 2, 1, 3)
    slot_g = jnp.asarray(np.eye(4)[:2], BF16)
    qm_n = jnp.einsum("bqgrd,gs->bqgrsd", qn, slot_g).reshape(bs, ts * NSA_HEADS, 4 * HEAD_DIM)
    rows_n = [(q, g, r) for q in range(ts) for g in range(NSA_GROUPS) for r in range(NSA_R)]
    rowq_n = jnp.asarray(np.array([r[0] for r in rows_n], np.int32).reshape(-1, 1))
    rowtbl_n = rel_t[DA_HEADS + np.array([r[1] * NSA_R + r[2] for r in rows_n])]
    memb_rows = memb_s.reshape(bs, NSA_GROUPS, 8, mb_s)[:, :, :ts].transpose(0, 2, 1, 3)
    memb_rows = jnp.broadcast_to(memb_rows[:, :, :, None, :], (bs, ts, NSA_GROUPS, NSA_R, mb_s))
    memb_rows = memb_rows.reshape(bs, ts * NSA_HEADS, mb_s).astype(BF16)
    tail_sel = _pad_rows(p0[np_:, C0_SEL:C0_SEL + 256].reshape(bs, ts, 256), PAGE)
    acc_sel = _decode("sel", qm_n, rowq_n, cache_nsa_sel.reshape(n_phys, PAGE, 256), pt_flat, tail_sel,
                      n_pages, 8, 0, qpos0, rowtbl=rowtbl_n, memb=memb_rows)
    w_buf = state_nsa_win.shape[1]
    tail_win = _pad_rows(p0[np_:, C0_WIN:C0_WIN + 256].reshape(bs, ts, 256), PAGE)
    n_wp = w_buf // PAGE
    acc_win = _decode("win", qm_n, rowq_n, state_nsa_win.reshape(bs * n_wp, PAGE, 256),
                      jnp.arange(bs * n_wp, dtype=I32), tail_win, n_wp, n_wp, qpos0 - w_buf, qpos0,
                      rowtbl=rowtbl_n)

    def pick_v(acc):
        a = acc.reshape(bs, ts, NSA_GROUPS, NSA_R, 4, HEAD_DIM)
        return jnp.einsum("bqgrsd,gs->bqgrd", a, jnp.asarray(np.eye(4)[2:], F32))

    gate_s = p0[np_:, C0_GATE:C0_GATE + 3 * NSA_HEADS].reshape(ns * NSA_HEADS, 3)
    o_ns_s = _nsa_combine(oc_s.reshape(ns * NSA_HEADS, HEAD_DIM), pick_v(acc_sel).reshape(ns * NSA_HEADS, HEAD_DIM),
                          pick_v(acc_win).reshape(ns * NSA_HEADS, HEAD_DIM), gate_s).reshape(ns, 512)

    o_da = jnp.concatenate([o_da_p, o_da_s], 0)
    o_ns = jnp.concatenate([o_ns_p, o_ns_s], 0)
    x1 = _mix_ln([o_da, o_ns], l0_w_out.astype(BF16), x0, ln_g[0], ln_b[0])
    x2 = _ffn_ln(x1, ffn_w_gate.astype(BF16), ffn_w_up.astype(BF16), ffn_w_down.astype(BF16), ln_g[1], ln_b[1])

    w1p = jnp.pad(l1_w_in, ((0, 0), (0, C1 - l1_w_in.shape[1]))).astype(BF16)
    p1, p1b = _proj(x2, w1p)
    zero_carry_p = jnp.zeros((b, 1, FOX_HEADS), F32)
    lf_p, c_p = _cumsum_pages(p1[:np_, C1_F:C1_F + FOX_HEADS].reshape(np_ // PAGE, PAGE, FOX_HEADS),
                              jnp.arange(np_ // PAGE, dtype=I32), b, t // PAGE, fox_b_f, zero_carry_p, True)
    q_fx_p = p1b[:np_, C1_Q:C1_Q + 1024].reshape(b, t, FOX_HEADS, HEAD_DIM).transpose(0, 2, 1, 3)
    kv_fx_p = p1b[:np_, C1_KV:C1_KV + 512].reshape(b, t, 2 * FOX_KV_HEADS, HEAD_DIM).transpose(0, 2, 1, 3)
    c4 = c_p.reshape(b, t, FOX_KV_HEADS, FOX_R)
    o_fx_p = _gqa_prompt("fox", q_fx_p, kv_fx_p[:, :FOX_KV_HEADS], kv_fx_p[:, FOX_KV_HEADS:],
                         cq=c4.transpose(0, 2, 1, 3), ck=c4.transpose(0, 2, 3, 1))

    _, c_past = _cumsum_pages(cache_fox_logf, pt_flat, bs, n_pages, fox_b_f,
                              jnp.zeros((bs, 1, FOX_HEADS), F32), False)
    f_new = _pad_rows(p1[np_:, C1_F:C1_F + FOX_HEADS].reshape(bs, ts, FOX_HEADS), PAGE)
    lf_s, c_new = _cumsum_pages(f_new, jnp.arange(bs, dtype=I32), bs, 1, fox_b_f, c_past[:, past - 1:past], True)
    ck_s = jnp.concatenate([c_past, c_new], 1).transpose(0, 2, 1)
    cq_s = c_new[:, :ts].reshape(bs, ts * FOX_HEADS, 1)
    qf = p1b[np_:, C1_Q:C1_Q + 1024].reshape(bs, ts, FOX_KV_HEADS, FOX_R, HEAD_DIM) * jnp.asarray(SCALE, BF16)
    slot_f = jnp.asarray(np.eye(2 * FOX_KV_HEADS)[:FOX_KV_HEADS], BF16)
    qm_f = jnp.einsum("bqgrd,gs->bqgrsd", qf, slot_f).reshape(bs, ts * FOX_HEADS, 2 * FOX_KV_HEADS * HEAD_DIM)
    rowq_f = jnp.asarray(np.repeat(q_idx, FOX_HEADS).astype(np.int32).reshape(-1, 1))
    tail_f = _pad_rows(p1[np_:, C1_KV:C1_KV + 512].reshape(bs, ts, 512), PAGE)
    acc_f = _decode("fox", qm_f, rowq_f, cache_fox_kv.reshape(n_phys, PAGE, 512), pt_flat, tail_f,
                    n_pages, 8, 0, qpos0, cq=cq_s, ck=ck_s)
    acc_f = acc_f.reshape(bs, ts, FOX_KV_HEADS, FOX_R, 2 * FOX_KV_HEADS, HEAD_DIM)
    o_fx_s = jnp.einsum("bqgrsd,gs->bqgrd", acc_f, jnp.asarray(np.eye(2 * FOX_KV_HEADS)[FOX_KV_HEADS:], F32))
    o_fx = jnp.concatenate([o_fx_p, o_fx_s.reshape(ns, 1024).astype(BF16)], 0)
    x3 = _mix_ln([o_fx], l1_w_out.astype(BF16), x2, ln_g[2], ln_b[2])

    f_moe = _moe(x3, moe_router, moe_w_gate, moe_w_up, moe_w_down)
    y = _add_ln(x3, f_moe, ln_g[3], ln_b[3])

    g = NSA_GROUPS
    diff_all = p0[:, C0_DIFF:C0_DIFF + 1024]
    cmp_all = p0[:, C0_CMP:C0_CMP + 256]
    sel_all = p0[:, C0_SEL:C0_SEL + 256]
    win_all = p0[:, C0_WIN:C0_WIN + 256]
    fkv_all = p1[:, C1_KV:C1_KV + 512]
    keep_p = min(WINDOW, t)
    win_p_out = win_all[:np_].reshape(b, t, 2, g, HEAD_DIM)[:, t - keep_p:]
    full_win_s = jnp.concatenate([state_nsa_win, win_all[np_:].reshape(bs, ts, 2, g, HEAD_DIM)], 1)
    keep_s = min(WINDOW, full_win_s.shape[1])
    return (y[:np_].reshape(b, t, dm), y[np_:].reshape(bs, ts, dm),
            diff_all[:np_].reshape(b, t, DA_HEADS, 4 * HEAD_DIM), diff_all[np_:].reshape(bs, ts, DA_HEADS, 4 * HEAD_DIM),
            cmp_all[:np_].reshape(b, t, 2, g, HEAD_DIM), cmp_all[np_:].reshape(bs, ts, 2, g, HEAD_DIM),
            sel_all[:np_].reshape(b, t, 2, g, HEAD_DIM), sel_all[np_:].reshape(bs, ts, 2, g, HEAD_DIM),
            win_p_out, full_win_s[:, full_win_s.shape[1] - keep_s:],
            fkv_all[:np_].reshape(b, t, 2, FOX_KV_HEADS, HEAD_DIM), fkv_all[np_:].reshape(bs, ts, 2, FOX_KV_HEADS, HEAD_DIM),
            lf_p.reshape(b, t, FOX_HEADS), lf_s[:, :ts])
```

```python
import functools
import math

import numpy as np
import jax
import jax.numpy as jnp
from jax import lax
from jax.experimental import pallas as pl
from jax.experimental.pallas import tpu as pltpu

F32, BF16, I32 = jnp.float32, jnp.bfloat16, jnp.int32

HEAD_DIM = 64
DA_HEADS = 4
NSA_HEADS = 8
NSA_GROUPS = 2
NSA_R = NSA_HEADS // NSA_GROUPS
CMP_BLOCK = 32
CMP_STRIDE = 16
CMP_HIDDEN = 2 * HEAD_DIM
SEL_BLOCK = 64
SEL_TOPK = 16
WINDOW = 512
FOX_HEADS = 16
FOX_KV_HEADS = 4
FOX_R = FOX_HEADS // FOX_KV_HEADS
N_BUCKETS = 32
MAX_DISTANCE = 128
N_EXPERTS = 8
TOP_K = 2
MOE_BLOCK = 128
LN_EPS = 1e-5
RMS_EPS = 1e-5
DEPTH = 2
ALPHA = (2.0 * DEPTH) ** 0.25
DA_LAMBDA_INIT = 0.8 - 0.6 * math.exp(-0.3 * 0)
NEG_INF = -1e30
FORCED_BONUS = 1e4
SCALE = HEAD_DIM ** -0.5

PAGE = 128
BQ = 128
BK = 256
VMEM_LIMIT = 56 * 1024 * 1024


def _t5_thresholds():
    d = np.arange(0, 4 * MAX_DISTANCE)
    df = np.maximum(d, 1).astype(np.float32)
    max_exact = N_BUCKETS // 2
    large = max_exact + (np.log(df / np.float32(max_exact)) / np.float32(math.log(MAX_DISTANCE / max_exact))
                         * np.float32(N_BUCKETS - max_exact)).astype(np.int32)
    bucket = np.where(d < max_exact, d, np.minimum(large, N_BUCKETS - 1))
    return [int(np.argmax(bucket >= b)) for b in range(1, N_BUCKETS)]


T5_THR = _t5_thresholds()
T5_FAR = T5_THR[-1]


def _t5_bias(d, val):
    out = jnp.zeros(d.shape, F32) + val(0)
    for b in range(1, N_BUCKETS):
        out = jnp.where(d >= T5_THR[b - 1], val(b), out)
    return out


def _dot(a, b):
    return jnp.dot(a, b, preferred_element_type=F32)


def _dot_nt(a, b):
    return lax.dot_general(a, b, (((1,), (1,)), ((), ())), preferred_element_type=F32)


def _split3(x):
    hi = x.astype(BF16)
    r1 = x - hi.astype(F32)
    mid = r1.astype(BF16)
    lo = (r1 - mid.astype(F32)).astype(BF16)
    return hi, mid, lo


def _layer_norm(z, g, b):
    mu = jnp.mean(z, -1, keepdims=True)
    zc = z - mu
    var = jnp.mean(zc * zc, -1, keepdims=True)
    return zc * lax.rsqrt(var + LN_EPS) * g + b


def _online_update(s, accum, m_ref, l_ref, acc_ref):
    m_prev = m_ref[...]
    m_new = jnp.maximum(m_prev, jnp.max(s, -1, keepdims=True))
    alpha = jnp.exp(m_prev - m_new)
    p = jnp.exp(s - m_new)
    l_ref[...] = alpha * l_ref[...] + jnp.sum(p, -1, keepdims=True)
    acc_ref[...] = alpha * acc_ref[...] + accum(p.astype(BF16))
    m_ref[...] = m_new


def _online_update_t(ss, vts, m_ref, l_ref, acc_ref):
    m_prev = m_ref[...]
    m_new = m_prev
    for s in ss:
        m_new = jnp.maximum(m_new, jnp.max(s, 0, keepdims=True))
    alpha = jnp.exp(m_prev - m_new)
    l_new = alpha * l_ref[...]
    acc_new = alpha * acc_ref[...]
    for s, vt in zip(ss, vts):
        p = jnp.exp(s - m_new)
        l_new = l_new + jnp.sum(p, 0, keepdims=True)
        acc_new = acc_new + _dot(vt, p.astype(BF16))
    l_ref[...] = l_new
    acc_ref[...] = acc_new
    m_ref[...] = m_new


def _init_state(m_ref, l_ref, acc_ref):
    m_ref[...] = jnp.full(m_ref.shape, NEG_INF, F32)
    l_ref[...] = jnp.zeros(l_ref.shape, F32)
    acc_ref[...] = jnp.zeros(acc_ref.shape, F32)


def _row_tile(n):
    for t in (512, 384, 256, 128, 64, 32, 16, 8):
        if n % t == 0:
            return t
    raise ValueError(f"row count {n} is not a multiple of 8")


def _params(sem):
    return pltpu.CompilerParams(dimension_semantics=sem, vmem_limit_bytes=VMEM_LIMIT)


def _const_spec(shape):
    nd = len(shape)
    return pl.BlockSpec(shape, lambda *a: (0,) * nd, pipeline_mode=pl.Buffered(1))


def _proj_kernel(x_ref, w_ref, o_ref, ob_ref):
    y = _dot(x_ref[...].astype(BF16), w_ref[...])
    o_ref[...] = y
    ob_ref[...] = y.astype(BF16)


def _proj(x, w):
    n, k = x.shape
    c = w.shape[1]
    tm = _row_tile(n)
    return pl.pallas_call(
        _proj_kernel, name="proj",
        out_shape=(jax.ShapeDtypeStruct((n, c), F32), jax.ShapeDtypeStruct((n, c), BF16)),
        grid=(n // tm,),
        in_specs=[pl.BlockSpec((tm, k), lambda i: (i, 0)), _const_spec((k, c))],
        out_specs=(pl.BlockSpec((tm, c), lambda i: (i, 0)), pl.BlockSpec((tm, c), lambda i: (i, 0))),
        compiler_params=_params(("parallel",)),
    )(x, w)


def _rowmm_kernel(x_ref, w_ref, o_ref):
    o_ref[...] = _dot(x_ref[...].astype(BF16), w_ref[...])


def _rowmm(x, w):
    n, k = x.shape
    c = w.shape[1]
    tm = _row_tile(n)
    tm = min(tm, 256)
    return pl.pallas_call(
        _rowmm_kernel, name="rowmm",
        out_shape=jax.ShapeDtypeStruct((n, c), F32),
        grid=(n // tm,),
        in_specs=[pl.BlockSpec((tm, k), lambda i: (i, 0)), _const_spec((k, c))],
        out_specs=pl.BlockSpec((tm, c), lambda i: (i, 0)),
        compiler_params=_params(("parallel",)),
    )(x, w)


def _mix_ln_kernel(*refs, n_in):
    a_refs = refs[:n_in]
    w_ref, x_ref, g_ref, b_ref, y_ref = refs[n_in:]
    acc = None
    off = 0
    for a in a_refs:
        k = a.shape[1]
        t = _dot(a[...], w_ref[off:off + k, :])
        acc = t if acc is None else acc + t
        off += k
    y_ref[...] = _layer_norm(ALPHA * x_ref[...] + acc, g_ref[...], b_ref[...])


def _mix_ln(a_list, w, x, g, b):
    n, dm = x.shape
    tm = _row_tile(n)
    in_specs = [pl.BlockSpec((tm, a.shape[1]), lambda i: (i, 0)) for a in a_list]
    in_specs += [_const_spec(w.shape), pl.BlockSpec((tm, dm), lambda i: (i, 0)),
                 _const_spec((1, dm)), _const_spec((1, dm))]
    return pl.pallas_call(
        functools.partial(_mix_ln_kernel, n_in=len(a_list)), name="mix_ln",
        out_shape=jax.ShapeDtypeStruct((n, dm), F32),
        grid=(n // tm,), in_specs=in_specs,
        out_specs=pl.BlockSpec((tm, dm), lambda i: (i, 0)),
        compiler_params=_params(("parallel",)),
    )(*a_list, w, x, g.reshape(1, dm), b.reshape(1, dm))


def _ffn_ln_kernel(x_ref, wg_ref, wu_ref, wd_ref, g_ref, b_ref, y_ref):
    x = x_ref[...]
    xb = x.astype(BF16)
    h = jax.nn.silu(_dot(xb, wg_ref[...])) * _dot(xb, wu_ref[...])
    f = _dot(h.astype(BF16), wd_ref[...])
    y_ref[...] = _layer_norm(ALPHA * x + f, g_ref[...], b_ref[...])


def _ffn_ln(x, wg, wu, wd, g, b):
    n, dm = x.shape
    tm = min(_row_tile(n), 384)
    return pl.pallas_call(
        _ffn_ln_kernel, name="ffn_ln",
        out_shape=jax.ShapeDtypeStruct((n, dm), F32),
        grid=(n // tm,),
        in_specs=[pl.BlockSpec((tm, dm), lambda i: (i, 0)), _const_spec(wg.shape), _const_spec(wu.shape),
                  _const_spec(wd.shape), _const_spec((1, dm)), _const_spec((1, dm))],
        out_specs=pl.BlockSpec((tm, dm), lambda i: (i, 0)),
        compiler_params=_params(("parallel",)),
    )(x, wg, wu, wd, g.reshape(1, dm), b.reshape(1, dm))


def _add_ln_kernel(x_ref, f_ref, g_ref, b_ref, y_ref):
    y_ref[...] = _layer_norm(ALPHA * x_ref[...] + f_ref[...], g_ref[...], b_ref[...])


def _add_ln(x, f, g, b):
    n, dm = x.shape
    tm = _row_tile(n)
    return pl.pallas_call(
        _add_ln_kernel, name="add_ln",
        out_shape=jax.ShapeDtypeStruct((n, dm), F32),
        grid=(n // tm,),
        in_specs=[pl.BlockSpec((tm, dm), lambda i: (i, 0)), pl.BlockSpec((tm, dm), lambda i: (i, 0)),
                  _const_spec((1, dm)), _const_spec((1, dm))],
        out_specs=pl.BlockSpec((tm, dm), lambda i: (i, 0)),
        compiler_params=_params(("parallel",)),
    )(x, f, g.reshape(1, dm), b.reshape(1, dm))


def _bias_tiles_kernel(rel_ref, o_ref, *, nt, window, head0):
    h = pl.program_id(0) + head0
    kk = lax.broadcasted_iota(I32, (BK, BQ), 0)
    qq = lax.broadcasted_iota(I32, (BK, BQ), 1)
    for t in range(nt):
        d = t * BQ + qq - kk
        bias = _t5_bias(d, lambda b: rel_ref[b, h])
        bad = d < 0
        if window:
            bad = bad | (d >= WINDOW)
        o_ref[0, t] = jnp.where(bad, NEG_INF, bias)


def _bias_tiles(rel_bias, nt, window, head0, nheads):
    return pl.pallas_call(
        functools.partial(_bias_tiles_kernel, nt=nt, window=window, head0=head0), name="bias_tiles",
        out_shape=jax.ShapeDtypeStruct((nheads, nt, BK, BQ), F32),
        grid=(nheads,),
        in_specs=[pl.BlockSpec(memory_space=pltpu.SMEM)],
        out_specs=pl.BlockSpec((1, nt, BK, BQ), lambda h: (h, 0, 0, 0)),
        compiler_params=_params(("parallel",)),
    )(rel_bias)


def _da_lambda(lp):
    return (jnp.exp(jnp.sum(lp[0:1] * lp[1:2], -1, keepdims=True))
            - jnp.exp(jnp.sum(lp[2:3] * lp[3:4], -1, keepdims=True)) + DA_LAMBDA_INIT)


def _to_rows(xt):
    eye = (lax.broadcasted_iota(I32, (BQ, BQ), 0) == lax.broadcasted_iota(I32, (BQ, BQ), 1)).astype(BF16)
    return _dot_nt(eye, xt)


def _flash_kernel(*refs, mode):
    if mode == "diff":
        rel_ref, lam_ref, nw_ref, bt_ref, q_ref, k_ref, vt_ref, o_ref, m_ref, l_ref, acc_ref = refs
    elif mode == "fox":
        q_ref, k_ref, vt_ref, o_ref, m_ref, l_ref, acc_ref = refs
    elif mode == "sel":
        rel_ref, bt_ref, q_ref, k_ref, vt_ref, memb_ref, o_ref, m_ref, l_ref, acc_ref = refs
    else:
        bt_ref, q_ref, k_ref, vt_ref, oc_ref, os_ref, gate_ref, o_ref, m_ref, l_ref, acc_ref = refs
    g = pl.program_id(1)
    i = pl.program_id(2)
    jd = lax.shift_right_logical(i, 1)
    t0 = lax.bitwise_and(i, 1)
    if mode == "diff":
        nr = 2
        q = q_ref[...]
        lane = lax.broadcasted_iota(I32, q.shape, 1)
        zero = jnp.zeros_like(q)
        qs = jnp.concatenate([jnp.where(lane < HEAD_DIM, q, zero), jnp.where(lane >= HEAD_DIM, q, zero)], 0)
        qs = qs * jnp.asarray(SCALE, BF16)
    else:
        nr = q_ref.shape[1]
        qs = q_ref[0].reshape(nr * BQ, q_ref.shape[3])
        if mode != "fox":
            qs = qs * jnp.asarray(SCALE, BF16)
    _init_state(m_ref, l_ref, acc_ref)

    def scores(j, tt, diag):
        off = pl.multiple_of(j * BK, BK)
        if mode == "diff":
            kt = k_ref[pl.ds(off, BK), :]
        else:
            kt = k_ref[0, 0, pl.ds(off, BK), :]
        s = _dot_nt(kt, qs)
        if mode == "fox":
            if diag:
                kk = lax.broadcasted_iota(I32, (BK, BQ), 0)
                qq = lax.broadcasted_iota(I32, (BK, BQ), 1)
                ok = (t0 * BQ + qq - kk) >= 0
                s = jnp.where(jnp.concatenate([ok] * nr, 1), s, NEG_INF)
        else:
            extra = None
            if mode == "sel":
                nbs = memb_ref.shape[2]
                kk = lax.broadcasted_iota(I32, (BK, nbs), 0)
                jb = lax.broadcasted_iota(I32, (BK, nbs), 1)
                e = (jb == j * (BK // SEL_BLOCK) + lax.shift_right_logical(kk, 6)).astype(BF16)
                extra = (_dot(e, memb_ref[0, 0]) - 1.0) * (-NEG_INF)
            parts = []
            for r in range(nr):
                if tt is None:
                    br = rel_ref[N_BUCKETS - 1, g if mode == "diff" else DA_HEADS + g * nr + r]
                else:
                    br = bt_ref[0 if mode == "diff" else r, tt]
                if extra is not None:
                    br = br + extra
                parts.append(s[:, r * BQ:(r + 1) * BQ] + br)
            s = jnp.concatenate(parts, 1)
        return s, vt_ref[0, 0, :, pl.ds(off, BK)]

    def tiles(spec):
        ss, vts = zip(*[scores(*a) for a in spec])
        _online_update_t(ss, vts, m_ref, l_ref, acc_ref)

    if mode != "win":
        n_far = jd if mode == "fox" else jnp.maximum(jd - 1, 0)

        def far_body(j2, c):
            tiles([(2 * j2, None, False), (2 * j2 + 1, None, False)])
            return c

        lax.fori_loop(0, lax.shift_right_logical(n_far, 1), far_body, 0)

        @pl.when(lax.bitwise_and(n_far, 1) == 1)
        def _():
            tiles([(n_far - 1, None, False)])
    n_near = 1 if mode == "fox" else (3 if mode == "win" else 2)
    for cnt in range(1, n_near + 1):
        cond = (jd >= cnt - 1) if cnt == n_near else (jd == cnt - 1)

        @pl.when(cond)
        def _(cnt=cnt):
            tiles([(jd - a, t0 + 2 * a, a == 0) for a in reversed(range(cnt))])

    ot = acc_ref[...] / l_ref[...]
    if mode == "diff":
        a = ot[:, :BQ] - _da_lambda(lam_ref[...]) * ot[:, BQ:]
        r = a * lax.rsqrt(jnp.mean(a * a, 0, keepdims=True) + RMS_EPS) * nw_ref[...] * (1.0 - DA_LAMBDA_INIT)
        o_ref[...] = _to_rows(r.astype(BF16)).astype(o_ref.dtype)
    elif mode == "fox":
        o_ref[...] = jnp.concatenate([_to_rows(ot[:, r * BQ:(r + 1) * BQ].astype(BF16)) for r in range(nr)],
                                     1).astype(o_ref.dtype)
    elif mode == "sel":
        for r in range(nr):
            o_ref[0, r] = ot[:, r * BQ:(r + 1) * BQ]
    else:
        gt = jax.nn.sigmoid(gate_ref[0, 0])
        parts = []
        for r in range(nr):
            comb = (gt[3 * r:3 * r + 1] * oc_ref[0, r] + gt[3 * r + 1:3 * r + 2] * os_ref[0, r]
                    + gt[3 * r + 2:3 * r + 3] * ot[:, r * BQ:(r + 1) * BQ])
            parts.append(_to_rows(comb.astype(BF16)))
        o_ref[...] = jnp.concatenate(parts, 1).astype(o_ref.dtype)


def _flash_scratch(nr, dv):
    return [pltpu.VMEM((1, nr * BQ), F32), pltpu.VMEM((1, nr * BQ), F32), pltpu.VMEM((dv, nr * BQ), F32)]


def _diff_prompt(p0b, vt, rel_bias, da_lambda, da_norm, bt, b, t, col_q, col_k):
    nq = t // BQ
    dv = 2 * HEAD_DIM
    return pl.pallas_call(
        functools.partial(_flash_kernel, mode="diff"), name="diff_prompt",
        out_shape=jax.ShapeDtypeStruct((b * t, DA_HEADS * dv), BF16),
        grid=(b, DA_HEADS, nq),
        in_specs=[pl.BlockSpec(memory_space=pltpu.SMEM),
                  _const_spec((4, HEAD_DIM)), _const_spec((dv, 1)),
                  pl.BlockSpec((1, 4, BK, BQ), lambda bi, h, i: (h, 0, 0, 0)),
                  pl.BlockSpec((BQ, 128), lambda bi, h, i: (bi * nq + i, col_q + h)),
                  pl.BlockSpec((t, 128), lambda bi, h, i: (bi, col_k + 2 * h)),
                  pl.BlockSpec((1, 1, dv, t), lambda bi, h, i: (bi, h, 0, 0))],
        out_specs=pl.BlockSpec((BQ, dv), lambda bi, h, i: (bi * nq + i, h)),
        scratch_shapes=_flash_scratch(2, dv),
        compiler_params=_params(("parallel", "parallel", "arbitrary")),
    )(rel_bias, da_lambda, da_norm.reshape(-1, 1), bt, p0b, p0b, vt)


def _gqa_prompt(mode, q, k, vt, **kw):
    b, hq, t, dk = q.shape
    hk = k.shape[1]
    nr = hq // hk
    nq = t // BQ
    q_spec = pl.BlockSpec((1, nr, BQ, dk), lambda bi, g, i: (bi, g, i, 0))
    k_spec = pl.BlockSpec((1, 1, t, dk), lambda bi, g, i: (bi, g, 0, 0))
    vt_spec = pl.BlockSpec((1, 1, HEAD_DIM, t), lambda bi, g, i: (bi, g, 0, 0))
    ot_spec = pl.BlockSpec((1, nr, HEAD_DIM, BQ), lambda bi, g, i: (bi, g, 0, i))
    smem = pl.BlockSpec(memory_space=pltpu.SMEM)
    row_out = pl.BlockSpec((BQ, nr * HEAD_DIM), lambda bi, g, i: (bi * nq + i, g))
    if mode == "fox":
        in_specs = [q_spec, k_spec, vt_spec]
        args = (q, k, vt)
        out_shape = jax.ShapeDtypeStruct((b * t, hq * HEAD_DIM), BF16)
        out_spec = row_out
    elif mode == "sel":
        nbs = kw["memb"].shape[2]
        in_specs = [smem, pl.BlockSpec((nr, 4, BK, BQ), lambda bi, g, i: (1 + g, 0, 0, 0)),
                    q_spec, k_spec, vt_spec,
                    pl.BlockSpec((1, 1, nbs, BQ), lambda bi, g, i: (bi, g, 0, i))]
        args = (kw["rel_bias"], kw["bt"], q, k, vt, kw["memb"])
        out_shape = jax.ShapeDtypeStruct((b, hq, HEAD_DIM, t), F32)
        out_spec = ot_spec
    else:
        in_specs = [pl.BlockSpec((nr, 6, BK, BQ), lambda bi, g, i: (g, 0, 0, 0)),
                    q_spec, k_spec, vt_spec, ot_spec, ot_spec,
                    pl.BlockSpec((1, 1, 3 * nr, BQ), lambda bi, g, i: (bi, g, 0, i))]
        args = (kw["bt"], q, k, vt, kw["oc"], kw["os"], kw["gate"])
        out_shape = jax.ShapeDtypeStruct((b * t, hq * HEAD_DIM), BF16)
        out_spec = row_out
    return pl.pallas_call(
        functools.partial(_flash_kernel, mode=mode), name="gqa_" + mode,
        out_shape=out_shape, grid=(b, hk, nq), in_specs=in_specs, out_specs=out_spec,
        scratch_shapes=_flash_scratch(nr, HEAD_DIM),
        compiler_params=_params(("parallel", "parallel", "arbitrary")),
    )(*args)


def _diff_finish_kernel(o1_ref, o2_ref, lam_ref, nw_ref, o_ref):
    a = o1_ref[...] - _da_lambda(lam_ref[...]) * o2_ref[...]
    r = a * lax.rsqrt(jnp.mean(a * a, -1, keepdims=True) + RMS_EPS) * nw_ref[...] * (1.0 - DA_LAMBDA_INIT)
    o_ref[...] = r.astype(o_ref.dtype)


def _diff_finish(o1, o2, da_lambda, da_norm):
    n, w = o1.shape
    return pl.pallas_call(
        _diff_finish_kernel, name="diff_finish",
        out_shape=jax.ShapeDtypeStruct((n, w), BF16),
    )(o1, o2, da_lambda, da_norm.reshape(1, -1))


def _cmp_weight(w1):
    w1r = w1.reshape(2, 2, CMP_STRIDE, HEAD_DIM, CMP_HIDDEN)
    w1c = w1r[jnp.array([0, 0, 1, 1])]
    wb = jnp.einsum("chrdn,ce->rcdehn", w1c, jnp.eye(4, dtype=w1.dtype))
    return wb.reshape(CMP_STRIDE * 4 * HEAD_DIM, 4 * 2 * CMP_HIDDEN).astype(BF16)


def _cmp_finish_kernel(part_ref, pos_ref, w1_ref, w2_ref, kc_ref, vc_ref):
    nch = part_ref.shape[1]
    row = lax.broadcasted_iota(I32, (nch, HEAD_DIM), 0)
    for kv in range(2):
        posw = _dot(pos_ref[kv], w1_ref[kv])[0:1]
        for g in range(NSA_GROUPS):
            c = kv * NSA_GROUPS + g
            a = part_ref[0, :, 256 * c:256 * c + 128]
            bm = part_ref[0, :, 256 * c + 128:256 * c + 256]
            hid = a + pltpu.roll(bm, nch - 1, 0) + posw
            y = _dot(jax.nn.gelu(hid).astype(BF16), w2_ref[kv])
            y = jnp.where(row < nch - 1, y, 0.0).astype(BF16)
            if kv == 0:
                kc_ref[0, g] = y
            else:
                vc_ref[0, g] = y


def _cmp_finish(part, cmp_pos, w1, w2):
    b, nch, _ = part.shape
    pos8 = jnp.broadcast_to(cmp_pos.reshape(2, 1, CMP_BLOCK * HEAD_DIM), (2, 8, CMP_BLOCK * HEAD_DIM)).astype(BF16)
    out = jax.ShapeDtypeStruct((b, NSA_GROUPS, nch, HEAD_DIM), BF16)
    o_spec = pl.BlockSpec((1, NSA_GROUPS, nch, HEAD_DIM), lambda bi: (bi, 0, 0, 0))
    return pl.pallas_call(
        _cmp_finish_kernel, name="cmp_finish",
        out_shape=(out, out), grid=(b,),
        in_specs=[pl.BlockSpec((1, nch, 1024), lambda bi: (bi, 0, 0)), _const_spec(pos8.shape),
                  _const_spec(w1.shape), _const_spec(w2.shape)],
        out_specs=(o_spec, o_spec),
        compiler_params=_params(("parallel",)),
    )(part, pos8, w1.astype(BF16), w2.astype(BF16))


def _paged_mm_kernel(pt_ref, *refs, npg):
    pages = refs[:npg]
    w_ref, o_ref = refs[npg:]
    x = jnp.concatenate([p[0] for p in pages], 0).astype(BF16)
    o_ref[...] = _dot(x, w_ref[...])


def _paged_mm(pool, pt_flat, w, npg=16):
    n = pt_flat.shape[0]
    _, r, k = pool.shape
    c = w.shape[1]
    npg = math.gcd(npg, n)
    specs = [pl.BlockSpec((1, r, k), (lambda s, pt, p=p: (pt[s * npg + p], 0, 0))) for p in range(npg)]
    grid_spec = pltpu.PrefetchScalarGridSpec(
        num_scalar_prefetch=1, grid=(n // npg,),
        in_specs=specs + [pl.BlockSpec((k, c), lambda s, pt: (0, 0), pipeline_mode=pl.Buffered(1))],
        out_specs=pl.BlockSpec((npg * r, c), lambda s, pt: (s, 0)))
    return pl.pallas_call(
        functools.partial(_paged_mm_kernel, npg=npg), name="paged_mm",
        out_shape=jax.ShapeDtypeStruct((n * r, c), F32), grid_spec=grid_spec,
        compiler_params=_params(("parallel",)),
    )(pt_flat, *([pool] * npg), w)


def _overlap_t(nbs, ncp):
    jb = lax.broadcasted_iota(I32, (nbs, ncp), 0) * SEL_BLOCK
    ci = lax.broadcasted_iota(I32, (nbs, ncp), 1) * CMP_STRIDE
    return ((ci < jb + SEL_BLOCK) & (ci + CMP_BLOCK > jb)).astype(BF16)


def _cmp_prompt_kernel(rel_ref, q_ref, kc_ref, vc_ref, oc_ref, memb_ref, *, nbs, k_eff):
    g = pl.program_id(1)
    i = pl.program_id(2)
    nr = q_ref.shape[1]
    ncp = kc_ref.shape[2]
    qs = q_ref[0].reshape(nr * BQ, HEAD_DIM) * jnp.asarray(SCALE, BF16)
    s = _dot_nt(qs, kc_ref[0, 0])
    pos = i * BQ + lax.broadcasted_iota(I32, (BQ, ncp), 0)
    d = pos - (lax.broadcasted_iota(I32, (BQ, ncp), 1) * CMP_STRIDE + CMP_BLOCK - 1)
    ok = d >= 0
    okf = ok.astype(F32)
    p_parts = []
    for r in range(nr):
        h = DA_HEADS + g * nr + r
        sr = jnp.where(ok, s[r * BQ:(r + 1) * BQ] + _t5_bias(d, lambda b: rel_ref[b, h]), NEG_INF)
        e = jnp.exp(sr - jnp.max(sr, -1, keepdims=True))
        p_parts.append(e / jnp.sum(e, -1, keepdims=True) * okf)
    p = jnp.concatenate(p_parts, 0)
    oc_ref[0] = _dot(p.astype(BF16), vc_ref[0, 0]).reshape(nr, BQ, HEAD_DIM)
    psum = p_parts[0]
    for r in range(1, nr):
        psum = psum + p_parts[r]
    ot = _overlap_t(nbs, ncp)
    imp = None
    for part in _split3(psum):
        t = _dot_nt(ot, part)
        imp = t if imp is None else imp + t
    jrow = lax.broadcasted_iota(I32, (nbs, BQ), 0)
    cur = lax.shift_right_logical(i * BQ + lax.broadcasted_iota(I32, (nbs, BQ), 1), 6)
    valid = jrow <= cur
    forced = valid & ((jrow == 0) | (jrow >= cur - 1))
    score = jnp.where(valid, imp + jnp.where(forced, FORCED_BONUS, 0.0), -1.0)
    rank = jnp.zeros((nbs, BQ), I32)
    for ii in range(nbs):
        row = score[ii:ii + 1, :]
        ahead = (row > score) | ((row == score) & (jrow > ii))
        rank = rank + ahead.astype(I32)
    memb_ref[0, 0] = ((rank < k_eff) & (score >= 0.0)).astype(memb_ref.dtype)


def _cmp_prompt(q, kc, vc, rel_bias):
    b, hq, t, _ = q.shape
    nr = hq // NSA_GROUPS
    nq = t // BQ
    ncp = kc.shape[2]
    nbs = -(-t // SEL_BLOCK)
    q_spec = pl.BlockSpec((1, nr, BQ, HEAD_DIM), lambda bi, g, i: (bi, g, i, 0))
    c_spec = pl.BlockSpec((1, 1, ncp, HEAD_DIM), lambda bi, g, i: (bi, g, 0, 0))
    return pl.pallas_call(
        functools.partial(_cmp_prompt_kernel, nbs=nbs, k_eff=min(SEL_TOPK, nbs)), name="cmp_prompt",
        out_shape=(jax.ShapeDtypeStruct((b, hq, t, HEAD_DIM), F32),
                   jax.ShapeDtypeStruct((b, NSA_GROUPS, nbs, t), BF16)),
        grid=(b, NSA_GROUPS, nq),
        in_specs=[pl.BlockSpec(memory_space=pltpu.SMEM), q_spec, c_spec, c_spec],
        out_specs=(q_spec, pl.BlockSpec((1, 1, nbs, BQ), lambda bi, g, i: (bi, g, 0, i))),
        compiler_params=_params(("parallel", "parallel", "parallel")),
    )(rel_bias, q, kc, vc)


def _cmp_sample_kernel(q_ref, rowtbl_ref, kc_ref, vc_ref, oc_ref, memb_ref, *, qpos0, nbs, k_eff):
    ncp = kc_ref.shape[2]
    mb = memb_ref.shape[2]
    rows = NSA_R * 8
    qi = lax.bitwise_and(lax.broadcasted_iota(I32, (rows, ncp), 0), 7)
    d = qpos0 + qi - (lax.broadcasted_iota(I32, (rows, ncp), 1) * CMP_STRIDE + CMP_BLOCK - 1)
    ok = d >= 0
    jb = lax.broadcasted_iota(I32, (ncp, mb), 1) * SEL_BLOCK
    ci = lax.broadcasted_iota(I32, (ncp, mb), 0) * CMP_STRIDE
    ov = ((ci < jb + SEL_BLOCK) & (ci + CMP_BLOCK > jb)).astype(BF16)
    jl = lax.broadcasted_iota(I32, (8, mb), 1)
    cur = lax.shift_right_logical(qpos0 + lax.broadcasted_iota(I32, (8, mb), 0), 6)
    valid = (jl <= cur) & (jl < nbs)
    forced = valid & ((jl == 0) | (jl >= cur - 1))
    for g in range(NSA_GROUPS):
        s = _dot_nt(q_ref[0, g * rows:(g + 1) * rows], kc_ref[0, g])
        tbl = rowtbl_ref[g * rows:(g + 1) * rows]
        sm = jnp.where(ok, s + _t5_bias(d, lambda b: tbl[:, b:b + 1]), NEG_INF)
        e = jnp.exp(sm - jnp.max(sm, -1, keepdims=True))
        p = e / jnp.sum(e, -1, keepdims=True) * ok.astype(F32)
        oc_ref[0, g * rows:(g + 1) * rows] = _dot(p.astype(BF16), vc_ref[0, g])
        psum = p[0:8]
        for r in range(1, NSA_R):
            psum = psum + p[8 * r:8 * r + 8]
        imp = None
        for part in _split3(psum):
            t = _dot(part, ov)
            imp = t if imp is None else imp + t
        score = jnp.where(valid, imp + jnp.where(forced, FORCED_BONUS, 0.0), -1.0)
        score = jnp.where(jl < nbs, score, -2.0)
        rank = jnp.zeros((8, mb), I32)
        for ii in range(nbs):
            col = score[:, ii:ii + 1]
            ahead = (col > score) | ((col == score) & (jl > ii))
            rank = rank + ahead.astype(I32)
        memb_ref[0, g * 8:(g + 1) * 8] = ((rank < k_eff) & (score >= 0.0)).astype(F32)


def _cmp_sample(qrows, rowtbl, kc, vc, qpos0, nbs, mb):
    b = qrows.shape[0]
    ncp = kc.shape[2]
    nrows = NSA_GROUPS * NSA_R * 8
    c_spec = pl.BlockSpec((1, NSA_GROUPS, ncp, HEAD_DIM), lambda bi: (bi, 0, 0, 0))
    return pl.pallas_call(
        functools.partial(_cmp_sample_kernel, qpos0=qpos0, nbs=nbs, k_eff=min(SEL_TOPK, nbs)),
        name="cmp_sample",
        out_shape=(jax.ShapeDtypeStruct((b, nrows, HEAD_DIM), F32),
                   jax.ShapeDtypeStruct((b, NSA_GROUPS * 8, mb), F32)),
        grid=(b,),
        in_specs=[pl.BlockSpec((1, nrows, HEAD_DIM), lambda bi: (bi, 0, 0)), _const_spec(rowtbl.shape),
                  c_spec, c_spec],
        out_specs=(pl.BlockSpec((1, nrows, HEAD_DIM), lambda bi: (bi, 0, 0)),
                   pl.BlockSpec((1, NSA_GROUPS * 8, mb), lambda bi: (bi, 0, 0))),
        compiler_params=_params(("parallel",)),
    )(qrows, rowtbl, kc, vc)


def _decode_kernel(pt_ref, *refs, mode, npg, n_pages, pw, kpos0, qpos0, feat_major):
    q_ref, rowq_ref = refs[0], refs[1]
    k = 2
    if mode == "fox":
        cq_ref, ck_ref, cktail_ref = refs[k:k + 3]
        k += 3
    else:
        rowtbl_ref = refs[k]
        k += 1
        if mode == "sel":
            memb_ref = refs[k]
            k += 1
    pages = refs[k:k + npg]
    tail_ref, o_ref, m_ref, l_ref, acc_ref = refs[k + npg:]
    step = pl.program_id(1)
    nr = q_ref.shape[1]
    q = q_ref[0]
    rowpos = qpos0 + rowq_ref[...]

    @pl.when(step == 0)
    def _():
        _init_state(m_ref, l_ref, acc_ref)

    def process(page_list, kp0, blk0, ck):
        kbs = [pg.astype(BF16) for pg in page_list]
        if feat_major:
            half = kbs[0].shape[0] // 2
            ss = [_dot(q, kb[:half]) for kb in kbs]
            widths = [kb.shape[1] for kb in kbs]
        else:
            ss = [_dot_nt(q, kb) for kb in kbs]
            widths = [kb.shape[0] for kb in kbs]
        s = ss[0] if len(ss) == 1 else jnp.concatenate(ss, 1)
        n = s.shape[1]
        d = rowpos - (kp0 + lax.broadcasted_iota(I32, (nr, n), 1))
        bad = d < 0
        if mode == "fox":
            s = s + (cq_ref[0] - jnp.concatenate([ck] * (nr // FOX_HEADS), 0))
        else:
            tbl = rowtbl_ref[...]
            near = qpos0 - (kp0 + n - 1) < T5_FAR
            s = s + lax.cond(near, lambda: _t5_bias(d, lambda b: tbl[:, b:b + 1]),
                             lambda: jnp.broadcast_to(tbl[:, N_BUCKETS - 1:N_BUCKETS], d.shape))
            if mode == "win":
                bad = bad | (d >= WINDOW)
            if mode == "sel":
                mb = memb_ref.shape[2]
                jb = lax.broadcasted_iota(I32, (mb, n), 0)
                kk = lax.broadcasted_iota(I32, (mb, n), 1)
                e = (jb == blk0 + lax.shift_right_logical(kk, 6)).astype(BF16)
                bad = bad | (_dot(memb_ref[0], e) < 0.5)
        s = jnp.where(bad, NEG_INF, s)

        def accum(p):
            out = None
            off = 0
            for kb, w in zip(kbs, widths):
                pp = p[:, off:off + w]
                t = _dot_nt(pp, kb[half:]) if feat_major else _dot(pp, kb)
                out = t if out is None else out + t
                off += w
            return out

        _online_update(s, accum, m_ref, l_ref, acc_ref)

    first = step * npg
    process([pg[0] for pg in pages], kpos0 + first * pw, first * (pw // SEL_BLOCK),
            ck_ref[0] if mode == "fox" else None)

    @pl.when(step == pl.num_programs(1) - 1)
    def _():
        process([tail_ref[0]], qpos0, n_pages * (pw // SEL_BLOCK), cktail_ref[0] if mode == "fox" else None)
        o_ref[0] = acc_ref[...] / l_ref[...]


def _decode(mode, qm, rowq, pool, pt_flat, tail, n_pages, npg, kpos0, qpos0, feat_major, **kw):
    b, nr, _ = qm.shape
    n_steps = n_pages // npg
    pshape = pool.shape[1:]
    pw = pshape[1] if feat_major else pshape[0]
    accw = pshape[0] // 2 if feat_major else pshape[1]
    in_specs = [pl.BlockSpec((1,) + qm.shape[1:], lambda bi, s, pt: (bi, 0, 0)),
                pl.BlockSpec((nr, 1), lambda bi, s, pt: (0, 0))]
    args = [qm, rowq]
    if mode == "fox":
        in_specs += [pl.BlockSpec((1, nr, 1), lambda bi, s, pt: (bi, 0, 0)),
                     pl.BlockSpec((1, FOX_HEADS, npg * pw), lambda bi, s, pt: (bi, 0, s)),
                     pl.BlockSpec((1, FOX_HEADS, PAGE), lambda bi, s, pt: (bi, 0, (n_pages * pw) // PAGE))]
        args += [kw["cq"], kw["ck"], kw["ck"]]
    else:
        in_specs.append(pl.BlockSpec((nr, N_BUCKETS), lambda bi, s, pt: (0, 0)))
        args.append(kw["rowtbl"])
        if mode == "sel":
            mb = kw["memb"].shape[2]
            in_specs.append(pl.BlockSpec((1, nr, mb), lambda bi, s, pt: (bi, 0, 0)))
            args.append(kw["memb"])
    in_specs += [pl.BlockSpec((1,) + pshape, (lambda bi, s, pt, p=p: (pt[bi * n_pages + s * npg + p], 0, 0)))
                 for p in range(npg)]
    args += [pool] * npg
    in_specs.append(pl.BlockSpec((1,) + tail.shape[1:], lambda bi, s, pt: (bi, 0, 0)))
    args.append(tail)
    grid_spec = pltpu.PrefetchScalarGridSpec(
        num_scalar_prefetch=1, grid=(b, n_steps), in_specs=in_specs,
        out_specs=pl.BlockSpec((1, nr, accw), lambda bi, s, pt: (bi, 0, 0)),
        scratch_shapes=[pltpu.VMEM((nr, 1), F32), pltpu.VMEM((nr, 1), F32), pltpu.VMEM((nr, accw), F32)])
    return pl.pallas_call(
        functools.partial(_decode_kernel, mode=mode, npg=npg, n_pages=n_pages, pw=pw, kpos0=kpos0, qpos0=qpos0,
                          feat_major=feat_major),
        name="decode_" + mode,
        out_shape=jax.ShapeDtypeStruct((b, nr, accw), F32), grid_spec=grid_spec,
        compiler_params=_params(("parallel", "arbitrary")),
    )(pt_flat, *args)


def _nsa_combine_kernel(oc_ref, os_ref, ow_ref, gate_ref, o_ref):
    gt = jax.nn.sigmoid(gate_ref[...])
    o = gt[:, 0:1] * oc_ref[...] + gt[:, 1:2] * os_ref[...] + gt[:, 2:3] * ow_ref[...]
    o_ref[...] = o.astype(o_ref.dtype)


def _nsa_combine(oc, os_, ow, gate):
    return pl.pallas_call(
        _nsa_combine_kernel, name="nsa_combine",
        out_shape=jax.ShapeDtypeStruct(oc.shape, BF16),
    )(oc, os_, ow, gate)


def _cumsum_kernel(pt_ref, bf_ref, cin_ref, *refs, npg, apply_logsig):
    pages = refs[:npg]
    lf_ref, c_ref, carry_ref = refs[npg:]

    @pl.when(pl.program_id(1) == 0)
    def _():
        carry_ref[...] = cin_ref[0]

    tri = (lax.broadcasted_iota(I32, (PAGE, PAGE), 0) <= lax.broadcasted_iota(I32, (PAGE, PAGE), 1)).astype(BF16)
    car = carry_ref[...]
    for p in range(npg):
        x = pages[p][0]
        if apply_logsig:
            z = x + bf_ref[...]
            x = jnp.minimum(z, 0.0) - jnp.log1p(jnp.exp(-jnp.abs(z)))
        lf_ref[0, :, p * PAGE:(p + 1) * PAGE] = x
        loc = None
        for part in _split3(x):
            t = _dot(part, tri)
            loc = t if loc is None else loc + t
        cum = loc + car
        c_ref[0, :, p * PAGE:(p + 1) * PAGE] = cum
        car = cum[:, PAGE - 1:PAGE]
    carry_ref[...] = car


def _cumsum_pages(pool, pt_flat, b, n_pages, b_f, carry_in, apply_logsig, npg=8):
    h = pool.shape[1]
    npg = math.gcd(npg, n_pages)
    in_specs = [pl.BlockSpec((h, 1), lambda bi, s, pt: (0, 0)),
                pl.BlockSpec((1, h, 1), lambda bi, s, pt: (bi, 0, 0))]
    in_specs += [pl.BlockSpec((1, h, PAGE), (lambda bi, s, pt, p=p: (pt[bi * n_pages + s * npg + p], 0, 0)))
                 for p in range(npg)]
    o_spec = pl.BlockSpec((1, h, npg * PAGE), lambda bi, s, pt: (bi, 0, s))
    out = jax.ShapeDtypeStruct((b, h, n_pages * PAGE), F32)
    grid_spec = pltpu.PrefetchScalarGridSpec(
        num_scalar_prefetch=1, grid=(b, n_pages // npg), in_specs=in_specs, out_specs=(o_spec, o_spec),
        scratch_shapes=[pltpu.VMEM((h, 1), F32)])
    return pl.pallas_call(
        functools.partial(_cumsum_kernel, npg=npg, apply_logsig=apply_logsig), name="cumsum_pages",
        out_shape=(out, out), grid_spec=grid_spec,
        compiler_params=_params(("parallel", "arbitrary")),
    )(pt_flat, b_f.reshape(h, 1), carry_in, *([pool] * npg))


def _router_kernel(x_ref, w_ref, o_ref):
    x = x_ref[...]
    w = w_ref[...]
    xh = x.astype(BF16)
    xl = (x - xh.astype(F32)).astype(BF16)
    wh = w.astype(BF16)
    wl = (w - wh.astype(F32)).astype(BF16)
    logits = _dot(xh, wh) + _dot(xl, wh) + _dot(xh, wl)
    lane = lax.broadcasted_iota(I32, logits.shape, 1)
    big = logits.shape[1]
    lg = jnp.where(lane < N_EXPERTS, logits, -jnp.inf)
    m1 = jnp.max(lg, -1, keepdims=True)
    i1 = jnp.min(jnp.where(lg == m1, lane, big), -1, keepdims=True)
    lg2 = jnp.where(lane == i1, -jnp.inf, lg)
    m2 = jnp.max(lg2, -1, keepdims=True)
    i2 = jnp.min(jnp.where(lg2 == m2, lane, big), -1, keepdims=True)
    e2 = jnp.exp(m2 - m1)
    g1 = 1.0 / (1.0 + e2)
    g2 = e2 / (1.0 + e2)
    out = jnp.where(lane == 0, i1.astype(F32), jnp.where(lane == 1, i2.astype(F32),
                    jnp.where(lane == 2, g1, jnp.where(lane == 3, g2, 0.0))))
    o_ref[...] = out


def _router(x, w_router):
    n, dm = x.shape
    tm = _row_tile(n)
    wp = jnp.zeros((dm, 128), F32).at[:, :N_EXPERTS].set(w_router)
    return pl.pallas_call(
        _router_kernel, name="router",
        out_shape=jax.ShapeDtypeStruct((n, 128), F32),
        grid=(n // tm,),
        in_specs=[pl.BlockSpec((tm, dm), lambda i: (i, 0)), _const_spec((dm, 128))],
        out_specs=pl.BlockSpec((tm, 128), lambda i: (i, 0)),
        compiler_params=_params(("parallel",)),
    )(x, wp)


def _expert_kernel(be_ref, nb_ref, x_ref, wg_ref, wu_ref, wd_ref, o_ref):
    i = pl.program_id(0)

    @pl.when(i < nb_ref[0])
    def _():
        xb = x_ref[...]
        h = jax.nn.silu(_dot(xb, wg_ref[0])) * _dot(xb, wu_ref[0])
        o_ref[...] = _dot(h.astype(BF16), wd_ref[0])

    @pl.when(i >= nb_ref[0])
    def _():
        o_ref[...] = jnp.zeros(o_ref.shape, o_ref.dtype)


def _experts(buf, block_e, n_used, wg, wu, wd):
    nrow, dm = buf.shape
    nb = nrow // MOE_BLOCK
    dff = wg.shape[2]
    grid_spec = pltpu.PrefetchScalarGridSpec(
        num_scalar_prefetch=2, grid=(nb,),
        in_specs=[pl.BlockSpec((MOE_BLOCK, dm), lambda i, be, nu: (i, 0)),
                  pl.BlockSpec((1, dm, dff), lambda i, be, nu: (be[i], 0, 0)),
                  pl.BlockSpec((1, dm, dff), lambda i, be, nu: (be[i], 0, 0)),
                  pl.BlockSpec((1, dff, dm), lambda i, be, nu: (be[i], 0, 0))],
        out_specs=pl.BlockSpec((MOE_BLOCK, dm), lambda i, be, nu: (i, 0)))
    return pl.pallas_call(
        _expert_kernel, name="experts",
        out_shape=jax.ShapeDtypeStruct((nrow, dm), F32), grid_spec=grid_spec,
        compiler_params=_params(("arbitrary",)),
    )(block_e, n_used, buf, wg, wu, wd)


def _moe(x, w_router, wg, wu, wd):
    n, dm = x.shape
    route = _router(x, w_router)
    top_e = route[:, 0:2].astype(I32)
    gate = route[:, 2:4]
    n_assign = n * TOP_K
    flat_e = top_e.reshape(-1)
    order = jnp.argsort(flat_e)
    e_sorted = flat_e[order]
    tok_sorted = order // TOP_K
    g_sorted = gate.reshape(-1)[order]
    counts = jnp.bincount(flat_e, length=N_EXPERTS)
    padded = (counts + MOE_BLOCK - 1) // MOE_BLOCK * MOE_BLOCK
    start = jnp.cumsum(counts) - counts
    pad_end = jnp.cumsum(padded)
    pad_start = pad_end - padded
    dest = pad_start[e_sorted] + jnp.arange(n_assign) - start[e_sorted]
    n_blocks = -(-n_assign // MOE_BLOCK) + N_EXPERTS
    block_e = jnp.minimum(jnp.searchsorted(pad_end, jnp.arange(n_blocks) * MOE_BLOCK, side="right"),
                          N_EXPERTS - 1).astype(I32)
    n_used = (pad_end[-1] // MOE_BLOCK).astype(I32).reshape(1)
    buf = jnp.zeros((n_blocks * MOE_BLOCK, dm), BF16).at[dest].set(x.astype(BF16)[tok_sorted])
    out = _experts(buf, block_e, n_used, wg.astype(BF16), wu.astype(BF16), wd.astype(BF16))
    y_assign = out[dest] * g_sorted[:, None]
    return jnp.zeros((n, dm), F32).at[tok_sorted].add(y_assign)


C0_QDA, C0_DIFF, C0_QNS, C0_CMP, C0_SEL, C0_WIN, C0_GATE, C0 = 0, 512, 1536, 2048, 2304, 2560, 2816, 2944
C1_Q, C1_KV, C1_F, C1 = 0, 1024, 1536, 1664
FOX_DK = 128


def _l0_weight(w_in):
    k_off, v_off = 512, 1024
    cols = list(range(0, 512))
    for h in range(DA_HEADS):
        cols += list(range(k_off + 128 * h, k_off + 128 * (h + 1)))
        cols += list(range(v_off + 128 * h, v_off + 128 * (h + 1)))
    cols += list(range(1536, 2840))
    w = w_in[:, np.asarray(cols)]
    return jnp.pad(w, ((0, 0), (0, C0 - w.shape[1]))).astype(BF16)


def _pad_rows(x, rows):
    return jnp.pad(x, ((0, 0), (0, rows - x.shape[1])) + ((0, 0),) * (x.ndim - 2))


def _fox_augment(q, k, c):
    b, t = c.shape[:2]

    def split3(x):
        def top(v):
            return lax.bitcast_convert_type(lax.bitcast_convert_type(v, jnp.uint32) & jnp.uint32(0xFFFF0000), F32)
        hi = top(x)
        mid = top(x - hi)
        lo = top(x - hi - mid)
        return jnp.stack([hi.astype(BF16), mid.astype(BF16), lo.astype(BF16)], -1)

    c3 = split3(c)
    nc3 = split3(-c).reshape(b, t, FOX_KV_HEADS, FOX_R * 3)
    slot = jnp.asarray(np.repeat(np.eye(FOX_R), 3, axis=1), BF16)
    slot = jnp.broadcast_to(jnp.tile(slot, (FOX_KV_HEADS, 1))[None, None], (b, t, FOX_HEADS, 3 * FOX_R))
    pad_q = jnp.zeros((b, t, FOX_HEADS, FOX_DK - HEAD_DIM - 3 - 3 * FOX_R), BF16)
    qa = jnp.concatenate([q * jnp.asarray(SCALE, BF16), c3, slot, pad_q], -1)
    ones = jnp.ones((b, t, FOX_KV_HEADS, 3), BF16)
    pad_k = jnp.zeros((b, t, FOX_KV_HEADS, FOX_DK - HEAD_DIM - 3 - 3 * FOX_R), BF16)
    ka = jnp.concatenate([k, ones, nc3, pad_k], -1)
    return qa.transpose(0, 2, 1, 3), ka.transpose(0, 2, 1, 3)


def kernel(x_prompt, x_sample, cache_diff_kv, cache_nsa_cmp, cache_nsa_sel, state_nsa_win, cache_fox_kv,
           cache_fox_logf, page_table, rel_bias, l0_w_in, l0_w_out, da_lambda, da_norm, nsa_cmp_pos,
           nsa_cmp_w1, nsa_cmp_w2, ffn_w_gate, ffn_w_up, ffn_w_down, l1_w_in, fox_b_f, l1_w_out,
           moe_router, moe_w_gate, moe_w_up, moe_w_down, ln_g, ln_b):
    b, t, dm = x_prompt.shape
    bs, ts, _ = x_sample.shape
    n_pages = page_table.shape[1]
    n_phys = cache_diff_kv.shape[0]
    past = n_pages * PAGE
    np_, ns = b * t, bs * ts
    pt_flat = page_table.reshape(-1).astype(I32)
    x0 = jnp.concatenate([x_prompt.reshape(np_, dm), x_sample.reshape(ns, dm)], 0)
    g2 = NSA_GROUPS

    p0, p0b = _proj(x0, _l0_weight(l0_w_in))
    bt_c = _bias_tiles(rel_bias, 4, False, 0, DA_HEADS + NSA_HEADS)
    bt_w = _bias_tiles(rel_bias, 6, True, DA_HEADS, NSA_HEADS)
    rel_t = rel_bias.T

    vt_da = p0b[:np_, C0_DIFF:C0_DIFF + 1024].reshape(b, t, DA_HEADS, 4 * HEAD_DIM)[..., 2 * HEAD_DIM:]
    o_da_p = _diff_prompt(p0b, vt_da.transpose(0, 2, 3, 1), rel_bias, da_lambda, da_norm, bt_c, b, t,
                          C0_QDA // 128, C0_DIFF // 128)

    q_ns_p = p0b[:np_, C0_QNS:C0_QNS + 512].reshape(b, t, NSA_HEADS, HEAD_DIM).transpose(0, 2, 1, 3)

    def kv_major(col):
        kv = p0b[:np_, col:col + 256].reshape(b, t, 2, g2, HEAD_DIM)
        return kv[:, :, 0].transpose(0, 2, 1, 3), kv[:, :, 1].transpose(0, 2, 3, 1)

    wbig = _cmp_weight(nsa_cmp_w1)
    part_p = _rowmm(p0[:np_, C0_CMP:C0_CMP + 256].reshape(np_ // CMP_STRIDE, CMP_STRIDE * 256), wbig)
    kc_p, vc_p = _cmp_finish(part_p.reshape(b, t // CMP_STRIDE, 1024), nsa_cmp_pos, nsa_cmp_w1, nsa_cmp_w2)
    oc_p, memb_p = _cmp_prompt(q_ns_p, kc_p, vc_p, rel_bias)
    ks_p, vst_p = kv_major(C0_SEL)
    os_p = _gqa_prompt("sel", q_ns_p, ks_p, vst_p, rel_bias=rel_bias, bt=bt_c, memb=memb_p)
    kw_p, vwt_p = kv_major(C0_WIN)
    gate_p = p0[:np_, C0_GATE:C0_GATE + 3 * NSA_HEADS].reshape(b, t, g2, 3 * NSA_R).transpose(0, 2, 3, 1)
    o_ns_p = _gqa_prompt("win", q_ns_p, kw_p, vwt_p, bt=bt_w, oc=oc_p.transpose(0, 1, 3, 2), os=os_p, gate=gate_p)

    qpos0 = past
    qd = p0b[np_:, C0_QDA:C0_QDA + 512].reshape(bs, ts, DA_HEADS, 2, HEAD_DIM).transpose(0, 3, 2, 1, 4)
    slot_w = jnp.asarray(np.eye(4)[:2], BF16)
    qm_d = jnp.einsum("bwhqd,hH,ws->bwhqHsd", qd, jnp.eye(DA_HEADS, dtype=BF16), slot_w)
    qm_d = (qm_d * jnp.asarray(SCALE, BF16)).reshape(bs, 2 * DA_HEADS * ts, DA_HEADS * 4 * HEAD_DIM)
    rows_d = [(w, h, q) for w in range(2) for h in range(DA_HEADS) for q in range(ts)]
    rowq_d = jnp.asarray(np.array([r[2] for r in rows_d], np.int32).reshape(-1, 1))
    rowtbl_d = rel_t[np.array([r[1] for r in rows_d])]
    tail_d = _pad_rows(p0[np_:, C0_DIFF:C0_DIFF + 1024].reshape(bs, ts, 1024), PAGE)
    acc_d = _decode("diff", qm_d, rowq_d, cache_diff_kv.reshape(n_phys, PAGE, 1024), pt_flat, tail_d,
                    n_pages, 8, 0, qpos0, False, rowtbl=rowtbl_d)
    acc_d = acc_d.reshape(bs, 2, DA_HEADS, ts, DA_HEADS, 2, 2 * HEAD_DIM)
    o12 = jnp.stack([acc_d[:, :, h, :, h, 1] for h in range(DA_HEADS)], 2)
    o_da_s = _diff_finish(o12[:, 0].reshape(bs * DA_HEADS * ts, 128), o12[:, 1].reshape(bs * DA_HEADS * ts, 128),
                          da_lambda, da_norm)
    o_da_s = o_da_s.reshape(bs, DA_HEADS, ts, 128).transpose(0, 2, 1, 3).reshape(ns, 512)

    part_s = _paged_mm(cache_nsa_cmp.reshape(n_phys, PAGE // CMP_STRIDE, CMP_STRIDE * 256), pt_flat, wbig)
    l_tot = past + ts
    if l_tot // CMP_STRIDE > past // CMP_STRIDE:
        raise NotImplementedError("new rows completing a compression chunk")
    kc_s, vc_s = _cmp_finish(part_s.reshape(bs, past // CMP_STRIDE, 1024), nsa_cmp_pos, nsa_cmp_w1, nsa_cmp_w2)
    nbs_s = -(-l_tot // SEL_BLOCK)
    mb_s = -(-nbs_s // 128) * 128
    qn = p0b[np_:, C0_QNS:C0_QNS + 512].reshape(bs, ts, g2, NSA_R, HEAD_DIM) * jnp.asarray(SCALE, BF16)
    q_cmp = _pad_rows(qn.reshape(bs, ts, NSA_HEADS, HEAD_DIM), 8).transpose(0, 2, 1, 3)
    q_cmp = q_cmp.reshape(bs, NSA_HEADS * 8, HEAD_DIM)
    rowtbl_c = rel_t[DA_HEADS + np.repeat(np.arange(NSA_HEADS), 8)]
    oc_s, memb_s = _cmp_sample(q_cmp, rowtbl_c, kc_s, vc_s, qpos0, nbs_s, mb_s)
    oc_s = oc_s.reshape(bs, NSA_HEADS, 8, HEAD_DIM)[:, :, :ts].transpose(0, 2, 1, 3)
    eye_g = jnp.eye(g2, dtype=BF16)
    qm_n = jnp.einsum("bqgrd,gs->bqgrsd", qn, eye_g).reshape(bs, ts * NSA_HEADS, g2 * HEAD_DIM)
    rows_n = [(q, g, r) for q in range(ts) for g in range(g2) for r in range(NSA_R)]
    rowq_n = jnp.asarray(np.array([r[0] for r in rows_n], np.int32).reshape(-1, 1))
    rowtbl_n = rel_t[DA_HEADS + np.array([r[1] * NSA_R + r[2] for r in rows_n])]
    memb_rows = memb_s.reshape(bs, g2, 8, mb_s)[:, :, :ts].transpose(0, 2, 1, 3)
    memb_rows = jnp.broadcast_to(memb_rows[:, :, :, None, :], (bs, ts, g2, NSA_R, mb_s))
    memb_rows = memb_rows.reshape(bs, ts * NSA_HEADS, mb_s).astype(BF16)

    def feat_tail(col, width):
        return _pad_rows(p0[np_:, col:col + width].reshape(bs, ts, width), PAGE).transpose(0, 2, 1)

    pool_sel = cache_nsa_sel.transpose(0, 2, 3, 4, 1).reshape(n_phys, 4 * HEAD_DIM, PAGE)
    acc_sel = _decode("sel", qm_n, rowq_n, pool_sel, pt_flat, feat_tail(C0_SEL, 256),
                      n_pages, 8, 0, qpos0, True, rowtbl=rowtbl_n, memb=memb_rows)
    w_buf = state_nsa_win.shape[1]
    pool_win = state_nsa_win.transpose(0, 2, 3, 4, 1).reshape(bs, 4 * HEAD_DIM, w_buf)
    acc_win = _decode("win", qm_n, rowq_n, pool_win, jnp.arange(bs, dtype=I32), feat_tail(C0_WIN, 256),
                      1, 1, qpos0 - w_buf, qpos0, True, rowtbl=rowtbl_n)

    def pick_group(acc):
        a = acc.reshape(bs, ts, g2, NSA_R, g2, HEAD_DIM)
        return jnp.stack([a[:, :, g, :, g] for g in range(g2)], 2)

    gate_s = p0[np_:, C0_GATE:C0_GATE + 3 * NSA_HEADS].reshape(ns * NSA_HEADS, 3)
    o_ns_s = _nsa_combine(oc_s.reshape(ns * NSA_HEADS, HEAD_DIM), pick_group(acc_sel).reshape(ns * NSA_HEADS, HEAD_DIM),
                          pick_group(acc_win).reshape(ns * NSA_HEADS, HEAD_DIM), gate_s).reshape(ns, 512)

    o_da = jnp.concatenate([o_da_p, o_da_s], 0)
    o_ns = jnp.concatenate([o_ns_p, o_ns_s], 0)
    x1 = _mix_ln([o_da, o_ns], l0_w_out.astype(BF16), x0, ln_g[0], ln_b[0])
    x2 = _ffn_ln(x1, ffn_w_gate.astype(BF16), ffn_w_up.astype(BF16), ffn_w_down.astype(BF16), ln_g[1], ln_b[1])

    w1p = jnp.pad(l1_w_in, ((0, 0), (0, C1 - l1_w_in.shape[1]))).astype(BF16)
    p1, p1b = _proj(x2, w1p)
    fl_p = p1[:np_, C1_F:C1_F + FOX_HEADS].reshape(np_ // PAGE, PAGE, FOX_HEADS).transpose(0, 2, 1)
    lft_p, ct_p = _cumsum_pages(fl_p, jnp.arange(np_ // PAGE, dtype=I32), b, t // PAGE, fox_b_f,
                                jnp.zeros((b, FOX_HEADS, 1), F32), True)
    q_fx = p1b[:np_, C1_Q:C1_Q + 1024].reshape(b, t, FOX_HEADS, HEAD_DIM)
    kv_fx = p1b[:np_, C1_KV:C1_KV + 512].reshape(b, t, 2, FOX_KV_HEADS, HEAD_DIM)
    qa_p, ka_p = _fox_augment(q_fx, kv_fx[:, :, 0], ct_p.transpose(0, 2, 1))
    o_fx_p = _gqa_prompt("fox", qa_p, ka_p, kv_fx[:, :, 1].transpose(0, 2, 3, 1))

    _, ct_past = _cumsum_pages(cache_fox_logf.transpose(0, 2, 1), pt_flat, bs, n_pages, fox_b_f,
                               jnp.zeros((bs, FOX_HEADS, 1), F32), False)
    f_new = _pad_rows(p1[np_:, C1_F:C1_F + FOX_HEADS].reshape(bs, ts, FOX_HEADS), PAGE).transpose(0, 2, 1)
    lft_s, ct_new = _cumsum_pages(f_new, jnp.arange(bs, dtype=I32), bs, 1, fox_b_f, ct_past[:, :, past - 1:past], True)
    ck_s = jnp.concatenate([ct_past, ct_new], 2)
    cq_s = ct_new[:, :, :ts].transpose(0, 2, 1).reshape(bs, ts * FOX_HEADS, 1)
    qf = p1b[np_:, C1_Q:C1_Q + 1024].reshape(bs, ts, FOX_KV_HEADS, FOX_R, HEAD_DIM) * jnp.asarray(SCALE, BF16)
    qm_f = jnp.einsum("bqgrd,gs->bqgrsd", qf, jnp.eye(FOX_KV_HEADS, dtype=BF16))
    qm_f = qm_f.reshape(bs, ts * FOX_HEADS, FOX_KV_HEADS * HEAD_DIM)
    rowq_f = jnp.asarray(np.repeat(np.arange(ts), FOX_HEADS).astype(np.int32).reshape(-1, 1))
    tail_f = _pad_rows(p1[np_:, C1_KV:C1_KV + 512].reshape(bs, ts, 512), PAGE).transpose(0, 2, 1)
    pool_fox = cache_fox_kv.transpose(0, 2, 3, 4, 1).reshape(n_phys, 2 * FOX_KV_HEADS * HEAD_DIM, PAGE)
    acc_f = _decode("fox", qm_f, rowq_f, pool_fox, pt_flat, tail_f, n_pages, 8, 0, qpos0, True, cq=cq_s, ck=ck_s)
    acc_f = acc_f.reshape(bs, ts, FOX_KV_HEADS, FOX_R, FOX_KV_HEADS, HEAD_DIM)
    o_fx_s = jnp.stack([acc_f[:, :, g, :, g] for g in range(FOX_KV_HEADS)], 2)
    o_fx = jnp.concatenate([o_fx_p, o_fx_s.reshape(ns, 1024).astype(BF16)], 0)
    x3 = _mix_ln([o_fx], l1_w_out.astype(BF16), x2, ln_g[2], ln_b[2])

    f_moe = _moe(x3, moe_router, moe_w_gate, moe_w_up, moe_w_down)
    y = _add_ln(x3, f_moe, ln_g[3], ln_b[3])

    diff_all = p0[:, C0_DIFF:C0_DIFF + 1024]
    cmp_all = p0[:, C0_CMP:C0_CMP + 256]
    sel_all = p0[:, C0_SEL:C0_SEL + 256]
    win_all = p0[:, C0_WIN:C0_WIN + 256]
    fkv_all = p1[:, C1_KV:C1_KV + 512]
    keep_p = min(WINDOW, t)
    win_p_out = win_all[:np_].reshape(b, t, 2, g2, HEAD_DIM)[:, t - keep_p:]
    full_win_s = jnp.concatenate([state_nsa_win, win_all[np_:].reshape(bs, ts, 2, g2, HEAD_DIM)], 1)
    keep_s = min(WINDOW, full_win_s.shape[1])
    return (y[:np_].reshape(b, t, dm), y[np_:].reshape(bs, ts, dm),
            diff_all[:np_].reshape(b, t, DA_HEADS, 4 * HEAD_DIM), diff_all[np_:].reshape(bs, ts, DA_HEADS, 4 * HEAD_DIM),
            cmp_all[:np_].reshape(b, t, 2, g2, HEAD_DIM), cmp_all[np_:].reshape(bs, ts, 2, g2, HEAD_DIM),
            sel_all[:np_].reshape(b, t, 2, g2, HEAD_DIM), sel_all[np_:].reshape(bs, ts, 2, g2, HEAD_DIM),
            win_p_out, full_win_s[:, full_win_s.shape[1] - keep_s:],
            fkv_all[:np_].reshape(b, t, 2, FOX_KV_HEADS, HEAD_DIM), fkv_all[np_:].reshape(bs, ts, 2, FOX_KV_HEADS, HEAD_DIM),
            lft_p.transpose(0, 2, 1), lft_s[:, :, :ts].transpose(0, 2, 1))
```

```python
import functools
import math

import numpy as np
import jax
import jax.numpy as jnp
from jax import lax
from jax.experimental import pallas as pl
from jax.experimental.pallas import tpu as pltpu

F32, BF16, I32 = jnp.float32, jnp.bfloat16, jnp.int32

HEAD_DIM = 64
DA_HEADS = 4
NSA_HEADS = 8
NSA_GROUPS = 2
NSA_R = NSA_HEADS // NSA_GROUPS
CMP_BLOCK = 32
CMP_STRIDE = 16
CMP_HIDDEN = 2 * HEAD_DIM
SEL_BLOCK = 64
SEL_TOPK = 16
WINDOW = 512
FOX_HEADS = 16
FOX_KV_HEADS = 4
FOX_R = FOX_HEADS // FOX_KV_HEADS
N_BUCKETS = 32
MAX_DISTANCE = 128
N_EXPERTS = 8
TOP_K = 2
MOE_BLOCK = 128
LN_EPS = 1e-5
RMS_EPS = 1e-5
DEPTH = 2
ALPHA = (2.0 * DEPTH) ** 0.25
DA_LAMBDA_INIT = 0.8 - 0.6 * math.exp(-0.3 * 0)
NEG_INF = -1e30
FORCED_BONUS = 1e4
SCALE = HEAD_DIM ** -0.5

PAGE = 128
BQ = 128
BK = 256
VMEM_LIMIT = 56 * 1024 * 1024


def _t5_thresholds():
    d = np.arange(0, 4 * MAX_DISTANCE)
    df = np.maximum(d, 1).astype(np.float32)
    max_exact = N_BUCKETS // 2
    large = max_exact + (np.log(df / np.float32(max_exact)) / np.float32(math.log(MAX_DISTANCE / max_exact))
                         * np.float32(N_BUCKETS - max_exact)).astype(np.int32)
    bucket = np.where(d < max_exact, d, np.minimum(large, N_BUCKETS - 1))
    return [int(np.argmax(bucket >= b)) for b in range(1, N_BUCKETS)]


T5_THR = _t5_thresholds()
T5_FAR = T5_THR[-1]


def _t5_bias(d, val):
    out = jnp.zeros(d.shape, F32) + val(0)
    for b in range(1, N_BUCKETS):
        out = jnp.where(d >= T5_THR[b - 1], val(b), out)
    return out


def _dot(a, b):
    return jnp.dot(a, b, preferred_element_type=F32)


def _dot_nt(a, b):
    return lax.dot_general(a, b, (((1,), (1,)), ((), ())), preferred_element_type=F32)


def _split3(x):
    hi = x.astype(BF16)
    r1 = x - hi.astype(F32)
    mid = r1.astype(BF16)
    lo = (r1 - mid.astype(F32)).astype(BF16)
    return hi, mid, lo


def _layer_norm(z, g, b):
    mu = jnp.mean(z, -1, keepdims=True)
    zc = z - mu
    var = jnp.mean(zc * zc, -1, keepdims=True)
    return zc * lax.rsqrt(var + LN_EPS) * g + b


def _online_update(s, accum, m_ref, l_ref, acc_ref):
    m_prev = m_ref[...]
    m_new = jnp.maximum(m_prev, jnp.max(s, -1, keepdims=True))
    alpha = jnp.exp(m_prev - m_new)
    p = jnp.exp(s - m_new)
    l_ref[...] = alpha * l_ref[...] + jnp.sum(p, -1, keepdims=True)
    acc_ref[...] = alpha * acc_ref[...] + accum(p)
    m_ref[...] = m_new


def _online_update_t(ss, vts, m_ref, l_ref, acc_ref):
    m_prev = m_ref[...]
    m_new = m_prev
    for s in ss:
        m_new = jnp.maximum(m_new, jnp.max(s, 0, keepdims=True))
    alpha = jnp.exp(m_prev - m_new)
    l_new = alpha * l_ref[...]
    acc_new = alpha * acc_ref[...]
    for s, vt in zip(ss, vts):
        p = jnp.exp(s - m_new)
        l_new = l_new + jnp.sum(p, 0, keepdims=True)
        acc_new = acc_new + _dot(vt, p.astype(BF16))
    l_ref[...] = l_new
    acc_ref[...] = acc_new
    m_ref[...] = m_new


def _init_state(m_ref, l_ref, acc_ref):
    m_ref[...] = jnp.full(m_ref.shape, NEG_INF, F32)
    l_ref[...] = jnp.zeros(l_ref.shape, F32)
    acc_ref[...] = jnp.zeros(acc_ref.shape, F32)


def _row_tile(n):
    for t in (512, 384, 256, 128, 64, 32, 16, 8):
        if n % t == 0:
            return t
    raise ValueError(f"row count {n} is not a multiple of 8")


def _params(sem):
    return pltpu.CompilerParams(dimension_semantics=sem, vmem_limit_bytes=VMEM_LIMIT)


def _const_spec(shape):
    nd = len(shape)
    return pl.BlockSpec(shape, lambda *a: (0,) * nd, pipeline_mode=pl.Buffered(1))


def _proj_kernel(x_ref, w_ref, o_ref, ob_ref):
    y = _dot(x_ref[...].astype(BF16), w_ref[...])
    o_ref[...] = y
    ob_ref[...] = y.astype(BF16)


def _proj(x, w):
    n, k = x.shape
    c = w.shape[1]
    tm = _row_tile(n)
    return pl.pallas_call(
        _proj_kernel, name="proj",
        out_shape=(jax.ShapeDtypeStruct((n, c), F32), jax.ShapeDtypeStruct((n, c), BF16)),
        grid=(n // tm,),
        in_specs=[pl.BlockSpec((tm, k), lambda i: (i, 0)), _const_spec((k, c))],
        out_specs=(pl.BlockSpec((tm, c), lambda i: (i, 0)), pl.BlockSpec((tm, c), lambda i: (i, 0))),
        compiler_params=_params(("parallel",)),
    )(x, w)


def _rowmm_kernel(x_ref, w_ref, o_ref):
    o_ref[...] = _dot(x_ref[...].astype(BF16), w_ref[...])


def _rowmm(x, w):
    n, k = x.shape
    c = w.shape[1]
    tm = _row_tile(n)
    tm = min(tm, 256)
    return pl.pallas_call(
        _rowmm_kernel, name="rowmm",
        out_shape=jax.ShapeDtypeStruct((n, c), F32),
        grid=(n // tm,),
        in_specs=[pl.BlockSpec((tm, k), lambda i: (i, 0)), _const_spec((k, c))],
        out_specs=pl.BlockSpec((tm, c), lambda i: (i, 0)),
        compiler_params=_params(("parallel",)),
    )(x, w)


def _mix_ln_kernel(*refs, n_in):
    a_refs = refs[:n_in]
    w_ref, x_ref, g_ref, b_ref, y_ref = refs[n_in:]
    acc = None
    off = 0
    for a in a_refs:
        k = a.shape[1]
        t = _dot(a[...], w_ref[off:off + k, :])
        acc = t if acc is None else acc + t
        off += k
    y_ref[...] = _layer_norm(ALPHA * x_ref[...] + acc, g_ref[...], b_ref[...])


def _mix_ln(a_list, w, x, g, b):
    n, dm = x.shape
    tm = _row_tile(n)
    in_specs = [pl.BlockSpec((tm, a.shape[1]), lambda i: (i, 0)) for a in a_list]
    in_specs += [_const_spec(w.shape), pl.BlockSpec((tm, dm), lambda i: (i, 0)),
                 _const_spec((1, dm)), _const_spec((1, dm))]
    return pl.pallas_call(
        functools.partial(_mix_ln_kernel, n_in=len(a_list)), name="mix_ln",
        out_shape=jax.ShapeDtypeStruct((n, dm), F32),
        grid=(n // tm,), in_specs=in_specs,
        out_specs=pl.BlockSpec((tm, dm), lambda i: (i, 0)),
        compiler_params=_params(("parallel",)),
    )(*a_list, w, x, g.reshape(1, dm), b.reshape(1, dm))


def _ffn_ln_kernel(x_ref, wg_ref, wu_ref, wd_ref, g_ref, b_ref, y_ref):
    x = x_ref[...]
    xb = x.astype(BF16)
    h = jax.nn.silu(_dot(xb, wg_ref[...])) * _dot(xb, wu_ref[...])
    f = _dot(h.astype(BF16), wd_ref[...])
    y_ref[...] = _layer_norm(ALPHA * x + f, g_ref[...], b_ref[...])


def _ffn_ln(x, wg, wu, wd, g, b):
    n, dm = x.shape
    tm = min(_row_tile(n), 384)
    return pl.pallas_call(
        _ffn_ln_kernel, name="ffn_ln",
        out_shape=jax.ShapeDtypeStruct((n, dm), F32),
        grid=(n // tm,),
        in_specs=[pl.BlockSpec((tm, dm), lambda i: (i, 0)), _const_spec(wg.shape), _const_spec(wu.shape),
                  _const_spec(wd.shape), _const_spec((1, dm)), _const_spec((1, dm))],
        out_specs=pl.BlockSpec((tm, dm), lambda i: (i, 0)),
        compiler_params=_params(("parallel",)),
    )(x, wg, wu, wd, g.reshape(1, dm), b.reshape(1, dm))


def _bias_tiles_kernel(rel_ref, o_ref, *, nt, window, head0):
    h = pl.program_id(0) + head0
    kk = lax.broadcasted_iota(I32, (BK, BQ), 0)
    qq = lax.broadcasted_iota(I32, (BK, BQ), 1)
    for t in range(nt):
        d = t * BQ + qq - kk
        bias = _t5_bias(d, lambda b: rel_ref[b, h])
        bad = d < 0
        if window:
            bad = bad | (d >= WINDOW)
        o_ref[0, t] = jnp.where(bad, NEG_INF, bias)


def _bias_tiles(rel_bias, nt, window, head0, nheads):
    return pl.pallas_call(
        functools.partial(_bias_tiles_kernel, nt=nt, window=window, head0=head0), name="bias_tiles",
        out_shape=jax.ShapeDtypeStruct((nheads, nt, BK, BQ), F32),
        grid=(nheads,),
        in_specs=[pl.BlockSpec(memory_space=pltpu.SMEM)],
        out_specs=pl.BlockSpec((1, nt, BK, BQ), lambda h: (h, 0, 0, 0)),
        compiler_params=_params(("parallel",)),
    )(rel_bias)


def _da_lambda(lp):
    return (jnp.exp(jnp.sum(lp[0:1] * lp[1:2], -1, keepdims=True))
            - jnp.exp(jnp.sum(lp[2:3] * lp[3:4], -1, keepdims=True)) + DA_LAMBDA_INIT)


def _to_rows(xt):
    eye = (lax.broadcasted_iota(I32, (BQ, BQ), 0) == lax.broadcasted_iota(I32, (BQ, BQ), 1)).astype(BF16)
    return _dot_nt(eye, xt)


def _flash_kernel(*refs, mode):
    if mode == "diff":
        rel_ref, lam_ref, nw_ref, bt_ref, q_ref, k_ref, vt_ref, o_ref, m_ref, l_ref, acc_ref = refs
    elif mode == "fox":
        q_ref, k_ref, vt_ref, o_ref, m_ref, l_ref, acc_ref = refs
    elif mode == "sel":
        rel_ref, bt_ref, q_ref, k_ref, vt_ref, memb_ref, o_ref, m_ref, l_ref, acc_ref = refs
    else:
        bt_ref, q_ref, k_ref, vt_ref, oc_ref, os_ref, gate_ref, o_ref, m_ref, l_ref, acc_ref = refs
    g = pl.program_id(1)
    i = pl.program_id(2)
    if mode == "diff":
        jd = i
        part_t0 = (0, 1, 0, 1)
    else:
        jd = lax.shift_right_logical(i, 1)
        t0 = lax.bitwise_and(i, 1)
    if mode == "diff":
        nr = 4
        q = q_ref[...]
        lane = lax.broadcasted_iota(I32, q.shape, 1)
        zero = jnp.zeros_like(q)
        qs = jnp.concatenate([jnp.where(lane < HEAD_DIM, q, zero), jnp.where(lane >= HEAD_DIM, q, zero)], 0)
        qs = qs * jnp.asarray(SCALE, BF16)
    else:
        nr = q_ref.shape[1]
        qs = q_ref[0].reshape(nr * BQ, q_ref.shape[3])
        if mode != "fox":
            qs = qs * jnp.asarray(SCALE, BF16)
    _init_state(m_ref, l_ref, acc_ref)

    def scores(j, back, diag):
        off = pl.multiple_of(j * BK, BK)
        if mode == "diff":
            kt = k_ref[pl.ds(off, BK), :]
        else:
            kt = k_ref[0, 0, pl.ds(off, BK), :]
        s = _dot_nt(kt, qs)
        if mode == "fox":
            if diag:
                kk = lax.broadcasted_iota(I32, (BK, BQ), 0)
                qq = lax.broadcasted_iota(I32, (BK, BQ), 1)
                ok = (t0 * BQ + qq - kk) >= 0
                s = jnp.where(jnp.concatenate([ok] * nr, 1), s, NEG_INF)
        else:
            extra = None
            if mode == "sel":
                nbs = memb_ref.shape[2]
                kk = lax.broadcasted_iota(I32, (BK, nbs), 0)
                jb = lax.broadcasted_iota(I32, (BK, nbs), 1)
                e = (jb == j * (BK // SEL_BLOCK) + lax.shift_right_logical(kk, 6)).astype(BF16)
                extra = (_dot(e, memb_ref[0, 0]) - 1.0) * (-NEG_INF)
            parts = []
            for r in range(nr):
                if back is None:
                    br = rel_ref[N_BUCKETS - 1, g if mode == "diff" else DA_HEADS + g * nr + r]
                elif mode == "diff":
                    br = bt_ref[0, part_t0[r] + 2 * back]
                else:
                    br = bt_ref[r, t0 + 2 * back]
                if extra is not None:
                    br = br + extra
                parts.append(s[:, r * BQ:(r + 1) * BQ] + br)
            s = jnp.concatenate(parts, 1)
        return s, vt_ref[0, 0, :, pl.ds(off, BK)]

    def tiles(spec):
        ss, vts = zip(*[scores(*a) for a in spec])
        _online_update_t(ss, vts, m_ref, l_ref, acc_ref)

    if mode != "win":
        n_far = jd if mode == "fox" else jnp.maximum(jd - 1, 0)

        def far_body(j2, c):
            tiles([(2 * j2, None, False), (2 * j2 + 1, None, False)])
            return c

        lax.fori_loop(0, lax.shift_right_logical(n_far, 1), far_body, 0)

        @pl.when(lax.bitwise_and(n_far, 1) == 1)
        def _():
            tiles([(n_far - 1, None, False)])
    n_near = 1 if mode == "fox" else (3 if mode == "win" else 2)
    for cnt in range(1, n_near + 1):
        cond = (jd >= cnt - 1) if cnt == n_near else (jd == cnt - 1)

        @pl.when(cond)
        def _(cnt=cnt):
            tiles([(jd - a, a, a == 0) for a in reversed(range(cnt))])

    ot = acc_ref[...] / l_ref[...]
    if mode == "diff":
        a = ot[:, :2 * BQ] - _da_lambda(lam_ref[...]) * ot[:, 2 * BQ:]
        r = a * lax.rsqrt(jnp.mean(a * a, 0, keepdims=True) + RMS_EPS) * nw_ref[...] * (1.0 - DA_LAMBDA_INIT)
        r = r.astype(BF16)
        o_ref[...] = jnp.concatenate([_to_rows(r[:, :BQ]), _to_rows(r[:, BQ:])], 0).astype(o_ref.dtype)
    elif mode == "fox":
        o_ref[...] = jnp.concatenate([_to_rows(ot[:, r * BQ:(r + 1) * BQ].astype(BF16)) for r in range(nr)],
                                     1).astype(o_ref.dtype)
    elif mode == "sel":
        for r in range(nr):
            o_ref[0, r] = ot[:, r * BQ:(r + 1) * BQ]
    else:
        gt = jax.nn.sigmoid(gate_ref[0, 0])
        parts = []
        for r in range(nr):
            comb = (gt[3 * r:3 * r + 1] * oc_ref[0, r] + gt[3 * r + 1:3 * r + 2] * os_ref[0, r]
                    + gt[3 * r + 2:3 * r + 3] * ot[:, r * BQ:(r + 1) * BQ])
            parts.append(_to_rows(comb.astype(BF16)))
        o_ref[...] = jnp.concatenate(parts, 1).astype(o_ref.dtype)


def _flash_scratch(nr, dv):
    return [pltpu.VMEM((1, nr * BQ), F32), pltpu.VMEM((1, nr * BQ), F32), pltpu.VMEM((dv, nr * BQ), F32)]


def _diff_prompt(p0b, vt, rel_bias, da_lambda, da_norm, bt, b, t, col_q, col_k):
    nq = t // BK
    dv = 2 * HEAD_DIM
    return pl.pallas_call(
        functools.partial(_flash_kernel, mode="diff"), name="diff_prompt",
        out_shape=jax.ShapeDtypeStruct((b * t, DA_HEADS * dv), BF16),
        grid=(b, DA_HEADS, nq),
        in_specs=[pl.BlockSpec(memory_space=pltpu.SMEM),
                  _const_spec((4, HEAD_DIM)), _const_spec((dv, 1)),
                  pl.BlockSpec((1, 4, BK, BQ), lambda bi, h, i: (h, 0, 0, 0)),
                  pl.BlockSpec((BK, 128), lambda bi, h, i: (bi * nq + i, col_q + h)),
                  pl.BlockSpec((t, 128), lambda bi, h, i: (bi, col_k + 2 * h)),
                  pl.BlockSpec((1, 1, dv, t), lambda bi, h, i: (bi, h, 0, 0))],
        out_specs=pl.BlockSpec((BK, dv), lambda bi, h, i: (bi * nq + i, h)),
        scratch_shapes=_flash_scratch(4, dv),
        compiler_params=_params(("parallel", "parallel", "arbitrary")),
    )(rel_bias, da_lambda, da_norm.reshape(-1, 1), bt, p0b, p0b, vt)


def _gqa_prompt(mode, q, k, vt, **kw):
    b, hq, t, dk = q.shape
    hk = k.shape[1]
    nr = hq // hk
    nq = t // BQ
    q_spec = pl.BlockSpec((1, nr, BQ, dk), lambda bi, g, i: (bi, g, i, 0))
    k_spec = pl.BlockSpec((1, 1, t, dk), lambda bi, g, i: (bi, g, 0, 0))
    vt_spec = pl.BlockSpec((1, 1, HEAD_DIM, t), lambda bi, g, i: (bi, g, 0, 0))
    ot_spec = pl.BlockSpec((1, nr, HEAD_DIM, BQ), lambda bi, g, i: (bi, g, 0, i))
    smem = pl.BlockSpec(memory_space=pltpu.SMEM)
    row_out = pl.BlockSpec((BQ, nr * HEAD_DIM), lambda bi, g, i: (bi * nq + i, g))
    if mode == "fox":
        in_specs = [q_spec, k_spec, vt_spec]
        args = (q, k, vt)
        out_shape = jax.ShapeDtypeStruct((b * t, hq * HEAD_DIM), BF16)
        out_spec = row_out
    elif mode == "sel":
        nbs = kw["memb"].shape[2]
        in_specs = [smem, pl.BlockSpec((nr, 4, BK, BQ), lambda bi, g, i: (1 + g, 0, 0, 0)),
                    q_spec, k_spec, vt_spec,
                    pl.BlockSpec((1, 1, nbs, BQ), lambda bi, g, i: (bi, g, 0, i))]
        args = (kw["rel_bias"], kw["bt"], q, k, vt, kw["memb"])
        out_shape = jax.ShapeDtypeStruct((b, hq, HEAD_DIM, t), F32)
        out_spec = ot_spec
    else:
        in_specs = [pl.BlockSpec((nr, 6, BK, BQ), lambda bi, g, i: (g, 0, 0, 0)),
                    q_spec, k_spec, vt_spec, ot_spec, ot_spec,
                    pl.BlockSpec((1, 1, 3 * nr, BQ), lambda bi, g, i: (bi, g, 0, i))]
        args = (kw["bt"], q, k, vt, kw["oc"], kw["os"], kw["gate"])
        out_shape = jax.ShapeDtypeStruct((b * t, hq * HEAD_DIM), BF16)
        out_spec = row_out
    return pl.pallas_call(
        functools.partial(_flash_kernel, mode=mode), name="gqa_" + mode,
        out_shape=out_shape, grid=(b, hk, nq), in_specs=in_specs, out_specs=out_spec,
        scratch_shapes=_flash_scratch(nr, HEAD_DIM),
        compiler_params=_params(("parallel", "parallel", "arbitrary")),
    )(*args)


def _diff_finish_kernel(o1_ref, o2_ref, lam_ref, nw_ref, o_ref):
    a = o1_ref[...] - _da_lambda(lam_ref[...]) * o2_ref[...]
    r = a * lax.rsqrt(jnp.mean(a * a, -1, keepdims=True) + RMS_EPS) * nw_ref[...] * (1.0 - DA_LAMBDA_INIT)
    o_ref[...] = r.astype(o_ref.dtype)


def _diff_finish(o1, o2, da_lambda, da_norm):
    n, w = o1.shape
    return pl.pallas_call(
        _diff_finish_kernel, name="diff_finish",
        out_shape=jax.ShapeDtypeStruct((n, w), BF16),
    )(o1, o2, da_lambda, da_norm.reshape(1, -1))


def _cmp_weight(w1):
    w1r = w1.reshape(2, 2, CMP_STRIDE, HEAD_DIM, CMP_HIDDEN)
    w1c = w1r[jnp.array([0, 0, 1, 1])]
    wb = jnp.einsum("chrdn,ce->rcdehn", w1c, jnp.eye(4, dtype=w1.dtype))
    return wb.reshape(CMP_STRIDE * 4 * HEAD_DIM, 4 * 2 * CMP_HIDDEN).astype(BF16)


def _cmp_finish_kernel(part_ref, pos_ref, w1_ref, w2_ref, kc_ref, vc_ref):
    nch = part_ref.shape[1]
    row = lax.broadcasted_iota(I32, (nch, HEAD_DIM), 0)
    for kv in range(2):
        posw = _dot(pos_ref[kv], w1_ref[kv])[0:1]
        for g in range(NSA_GROUPS):
            c = kv * NSA_GROUPS + g
            a = part_ref[0, :, 256 * c:256 * c + 128]
            bm = part_ref[0, :, 256 * c + 128:256 * c + 256]
            hid = a + pltpu.roll(bm, nch - 1, 0) + posw
            y = _dot(jax.nn.gelu(hid).astype(BF16), w2_ref[kv])
            y = jnp.where(row < nch - 1, y, 0.0).astype(BF16)
            if kv == 0:
                kc_ref[0, g] = y
            else:
                vc_ref[0, g] = y


def _cmp_finish(part, cmp_pos, w1, w2):
    b, nch, _ = part.shape
    pos8 = jnp.broadcast_to(cmp_pos.reshape(2, 1, CMP_BLOCK * HEAD_DIM), (2, 8, CMP_BLOCK * HEAD_DIM)).astype(BF16)
    out = jax.ShapeDtypeStruct((b, NSA_GROUPS, nch, HEAD_DIM), BF16)
    o_spec = pl.BlockSpec((1, NSA_GROUPS, nch, HEAD_DIM), lambda bi: (bi, 0, 0, 0))
    return pl.pallas_call(
        _cmp_finish_kernel, name="cmp_finish",
        out_shape=(out, out), grid=(b,),
        in_specs=[pl.BlockSpec((1, nch, 1024), lambda bi: (bi, 0, 0)), _const_spec(pos8.shape),
                  _const_spec(w1.shape), _const_spec(w2.shape)],
        out_specs=(o_spec, o_spec),
        compiler_params=_params(("parallel",)),
    )(part, pos8, w1.astype(BF16), w2.astype(BF16))


def _paged_cmp_kernel(pt_ref, *refs, npg):
    pages = refs[:npg]
    w_ref, o_ref = refs[npg:]
    nchunk = PAGE // CMP_STRIDE
    feat = pages[0].shape[1]
    ri = lax.broadcasted_iota(I32, (PAGE, PAGE), 0)
    ci = lax.broadcasted_iota(I32, (PAGE, PAGE), 1)
    perm = (ci == CMP_STRIDE * lax.bitwise_and(ri, nchunk - 1) + lax.shift_right_logical(ri, 3)).astype(BF16)
    xps = [_dot_nt(perm, pg[0].astype(BF16)) for pg in pages]
    acc = None
    for r in range(CMP_STRIDE):
        lhs = jnp.concatenate([xp[nchunk * r:nchunk * (r + 1)] for xp in xps], 0).astype(BF16)
        t = _dot(lhs, w_ref[feat * r:feat * (r + 1), :])
        acc = t if acc is None else acc + t
    o_ref[...] = acc


def _paged_cmp(pool, pt_flat, w, npg=16):
    n = pt_flat.shape[0]
    _, f, _ = pool.shape
    c = w.shape[1]
    npg = math.gcd(npg, n)
    nchunk = PAGE // CMP_STRIDE
    specs = [pl.BlockSpec((1, f, PAGE), (lambda s, pt, p=p: (pt[s * npg + p], 0, 0))) for p in range(npg)]
    grid_spec = pltpu.PrefetchScalarGridSpec(
        num_scalar_prefetch=1, grid=(n // npg,),
        in_specs=specs + [pl.BlockSpec(w.shape, lambda s, pt: (0, 0), pipeline_mode=pl.Buffered(1))],
        out_specs=pl.BlockSpec((npg * nchunk, c), lambda s, pt: (s, 0)))
    return pl.pallas_call(
        functools.partial(_paged_cmp_kernel, npg=npg), name="paged_cmp",
        out_shape=jax.ShapeDtypeStruct((n * nchunk, c), F32), grid_spec=grid_spec,
        compiler_params=_params(("parallel",)),
    )(pt_flat, *([pool] * npg), w)


def _overlap_t(nbs, ncp):
    jb = lax.broadcasted_iota(I32, (nbs, ncp), 0) * SEL_BLOCK
    ci = lax.broadcasted_iota(I32, (nbs, ncp), 1) * CMP_STRIDE
    return ((ci < jb + SEL_BLOCK) & (ci + CMP_BLOCK > jb)).astype(BF16)


def _cmp_prompt_kernel(rel_ref, q_ref, kc_ref, vc_ref, oc_ref, memb_ref, *, nbs, k_eff):
    g = pl.program_id(1)
    i = pl.program_id(2)
    nr = q_ref.shape[1]
    ncp = kc_ref.shape[2]
    qs = q_ref[0].reshape(nr * BQ, HEAD_DIM) * jnp.asarray(SCALE, BF16)
    s = _dot_nt(qs, kc_ref[0, 0])
    pos = i * BQ + lax.broadcasted_iota(I32, (BQ, ncp), 0)
    d = pos - (lax.broadcasted_iota(I32, (BQ, ncp), 1) * CMP_STRIDE + CMP_BLOCK - 1)
    ok = d >= 0
    okf = ok.astype(F32)
    p_parts = []
    for r in range(nr):
        h = DA_HEADS + g * nr + r
        sr = jnp.where(ok, s[r * BQ:(r + 1) * BQ] + _t5_bias(d, lambda b: rel_ref[b, h]), NEG_INF)
        e = jnp.exp(sr - jnp.max(sr, -1, keepdims=True))
        p_parts.append(e / jnp.sum(e, -1, keepdims=True) * okf)
    p = jnp.concatenate(p_parts, 0)
    oc_ref[0] = _dot(p.astype(BF16), vc_ref[0, 0]).reshape(nr, BQ, HEAD_DIM)
    psum = p_parts[0]
    for r in range(1, nr):
        psum = psum + p_parts[r]
    ot = _overlap_t(nbs, ncp)
    imp = None
    for part in _split3(psum):
        t = _dot_nt(ot, part)
        imp = t if imp is None else imp + t
    jrow = lax.broadcasted_iota(I32, (nbs, BQ), 0)
    cur = lax.shift_right_logical(i * BQ + lax.broadcasted_iota(I32, (nbs, BQ), 1), 6)
    valid = jrow <= cur
    forced = valid & ((jrow == 0) | (jrow >= cur - 1))
    score = jnp.where(valid, imp + jnp.where(forced, FORCED_BONUS, 0.0), -1.0)
    rank = jnp.zeros((nbs, BQ), I32)
    for ii in range(nbs):
        row = score[ii:ii + 1, :]
        ahead = (row > score) | ((row == score) & (jrow > ii))
        rank = rank + ahead.astype(I32)
    memb_ref[0, 0] = ((rank < k_eff) & (score >= 0.0)).astype(memb_ref.dtype)


def _cmp_prompt(q, kc, vc, rel_bias):
    b, hq, t, _ = q.shape
    nr = hq // NSA_GROUPS
    nq = t // BQ
    ncp = kc.shape[2]
    nbs = -(-t // SEL_BLOCK)
    q_spec = pl.BlockSpec((1, nr, BQ, HEAD_DIM), lambda bi, g, i: (bi, g, i, 0))
    c_spec = pl.BlockSpec((1, 1, ncp, HEAD_DIM), lambda bi, g, i: (bi, g, 0, 0))
    return pl.pallas_call(
        functools.partial(_cmp_prompt_kernel, nbs=nbs, k_eff=min(SEL_TOPK, nbs)), name="cmp_prompt",
        out_shape=(jax.ShapeDtypeStruct((b, hq, t, HEAD_DIM), F32),
                   jax.ShapeDtypeStruct((b, NSA_GROUPS, nbs, t), BF16)),
        grid=(b, NSA_GROUPS, nq),
        in_specs=[pl.BlockSpec(memory_space=pltpu.SMEM), q_spec, c_spec, c_spec],
        out_specs=(q_spec, pl.BlockSpec((1, 1, nbs, BQ), lambda bi, g, i: (bi, g, 0, i))),
        compiler_params=_params(("parallel", "parallel", "parallel")),
    )(rel_bias, q, kc, vc)


def _cmp_sample_kernel(q_ref, rowtbl_ref, kc_ref, vc_ref, oc_ref, memb_ref, *, qpos0, nbs, k_eff):
    ncp = kc_ref.shape[2]
    mb = memb_ref.shape[2]
    rows = NSA_R * 8
    qi = lax.bitwise_and(lax.broadcasted_iota(I32, (rows, ncp), 0), 7)
    d = qpos0 + qi - (lax.broadcasted_iota(I32, (rows, ncp), 1) * CMP_STRIDE + CMP_BLOCK - 1)
    ok = d >= 0
    jb = lax.broadcasted_iota(I32, (ncp, mb), 1) * SEL_BLOCK
    ci = lax.broadcasted_iota(I32, (ncp, mb), 0) * CMP_STRIDE
    ov = ((ci < jb + SEL_BLOCK) & (ci + CMP_BLOCK > jb)).astype(BF16)
    jl = lax.broadcasted_iota(I32, (8, mb), 1)
    cur = lax.shift_right_logical(qpos0 + lax.broadcasted_iota(I32, (8, mb), 0), 6)
    valid = (jl <= cur) & (jl < nbs)
    forced = valid & ((jl == 0) | (jl >= cur - 1))
    for g in range(NSA_GROUPS):
        s = _dot_nt(q_ref[0, g * rows:(g + 1) * rows], kc_ref[0, g])
        tbl = rowtbl_ref[g * rows:(g + 1) * rows]
        sm = jnp.where(ok, s + _t5_bias(d, lambda b: tbl[:, b:b + 1]), NEG_INF)
        e = jnp.exp(sm - jnp.max(sm, -1, keepdims=True))
        p = e / jnp.sum(e, -1, keepdims=True) * ok.astype(F32)
        oc_ref[0, g * rows:(g + 1) * rows] = _dot(p.astype(BF16), vc_ref[0, g])
        psum = p[0:8]
        for r in range(1, NSA_R):
            psum = psum + p[8 * r:8 * r + 8]
        imp = None
        for part in _split3(psum):
            t = _dot(part, ov)
            imp = t if imp is None else imp + t
        score = jnp.where(valid, imp + jnp.where(forced, FORCED_BONUS, 0.0), -1.0)
        score = jnp.where(jl < nbs, score, -2.0)
        rank = jnp.zeros((8, mb), I32)
        for ii in range(nbs):
            col = score[:, ii:ii + 1]
            ahead = (col > score) | ((col == score) & (jl > ii))
            rank = rank + ahead.astype(I32)
        memb_ref[0, g * 8:(g + 1) * 8] = ((rank < k_eff) & (score >= 0.0)).astype(F32)


def _cmp_sample(qrows, rowtbl, kc, vc, qpos0, nbs, mb):
    b = qrows.shape[0]
    ncp = kc.shape[2]
    nrows = NSA_GROUPS * NSA_R * 8
    c_spec = pl.BlockSpec((1, NSA_GROUPS, ncp, HEAD_DIM), lambda bi: (bi, 0, 0, 0))
    return pl.pallas_call(
        functools.partial(_cmp_sample_kernel, qpos0=qpos0, nbs=nbs, k_eff=min(SEL_TOPK, nbs)),
        name="cmp_sample",
        out_shape=(jax.ShapeDtypeStruct((b, nrows, HEAD_DIM), F32),
                   jax.ShapeDtypeStruct((b, NSA_GROUPS * 8, mb), F32)),
        grid=(b,),
        in_specs=[pl.BlockSpec((1, nrows, HEAD_DIM), lambda bi: (bi, 0, 0)), _const_spec(rowtbl.shape),
                  c_spec, c_spec],
        out_specs=(pl.BlockSpec((1, nrows, HEAD_DIM), lambda bi: (bi, 0, 0)),
                   pl.BlockSpec((1, NSA_GROUPS * 8, mb), lambda bi: (bi, 0, 0))),
        compiler_params=_params(("parallel",)),
    )(qrows, rowtbl, kc, vc)


def _decode_kernel(pt_ref, *refs, mode, npg, n_pages, pw, kpos0, qpos0, feat_major):
    q_ref, rowq_ref = refs[0], refs[1]
    k = 2
    if mode == "fox":
        cq_ref, aux_ref, auxtail_ref = refs[k:k + 3]
        k += 3
    else:
        rowtbl_ref = refs[k]
        k += 1
        if mode == "sel":
            aux_ref, auxtail_ref = refs[k:k + 2]
            k += 2
    pages = refs[k:k + npg]
    tail_ref, o_ref, m_ref, l_ref, acc_ref = refs[k + npg:]
    step = pl.program_id(1)
    nr = q_ref.shape[1]
    q = q_ref[0]
    rowpos = qpos0 + rowq_ref[:, 0:1]
    key_shift = 3 if mode == "diffx" else 0

    @pl.when(step == 0)
    def _():
        _init_state(m_ref, l_ref, acc_ref)

    def process(page_list, kp0, aux):
        kbs = [pg.astype(BF16) for pg in page_list]
        if feat_major:
            half = kbs[0].shape[0] // 2
            ss = [_dot(q, kb[:half]) for kb in kbs]
            widths = [kb.shape[1] for kb in kbs]
        else:
            ss = [_dot_nt(q, kb) for kb in kbs]
            widths = [kb.shape[0] for kb in kbs]
        s = ss[0] if len(ss) == 1 else jnp.concatenate(ss, 1)
        n = s.shape[1]
        col = lax.broadcasted_iota(I32, (nr, n), 1)
        d = rowpos - (kp0 + lax.shift_right_logical(col, key_shift))
        bad = d < 0
        if mode == "fox":
            s = s + (cq_ref[0] - jnp.concatenate([aux] * (nr // FOX_HEADS), 0))
        else:
            tbl = rowtbl_ref[...]
            near = qpos0 - (kp0 + (n >> key_shift) - 1) < T5_FAR
            s = s + lax.cond(near, lambda: _t5_bias(d, lambda b: tbl[:, b:b + 1]),
                             lambda: jnp.broadcast_to(tbl[:, N_BUCKETS - 1:N_BUCKETS], d.shape))
            if mode == "win":
                bad = bad | (d >= WINDOW)
            if mode == "sel":
                bad = bad | (aux < 0.5)
            if mode == "diffx":
                bad = bad | (lax.bitwise_and(col, 7) != rowq_ref[:, 1:2])
        s = jnp.where(bad, NEG_INF, s)

        def accum(p):
            out = None
            off = 0
            for kb, w in zip(kbs, widths):
                pp = p[:, off:off + w]
                if mode == "diffx":
                    pp = pltpu.roll(pp, 4, 1)
                pp = pp.astype(BF16)
                t = _dot_nt(pp, kb[half:]) if feat_major else _dot(pp, kb)
                out = t if out is None else out + t
                off += w
            return out

        _online_update(s, accum, m_ref, l_ref, acc_ref)

    keys_pp = pw >> key_shift
    process([pg[0] for pg in pages], kpos0 + step * (npg * keys_pp),
            aux_ref[0] if mode in ("fox", "sel") else None)

    @pl.when(step == pl.num_programs(1) - 1)
    def _():
        process([tail_ref[0]], qpos0, auxtail_ref[0] if mode in ("fox", "sel") else None)
        o_ref[0] = acc_ref[...] / l_ref[...]


def _decode(mode, qm, rowq, pool, pt_flat, tail, n_pages, npg, kpos0, qpos0, feat_major, **kw):
    b, nr, _ = qm.shape
    n_steps = n_pages // npg
    pshape = pool.shape[1:]
    pw = pshape[1] if feat_major else pshape[0]
    accw = pshape[0] // 2 if feat_major else pshape[1]
    in_specs = [pl.BlockSpec((1,) + qm.shape[1:], lambda bi, s, pt: (bi, 0, 0)),
                pl.BlockSpec((nr, 2), lambda bi, s, pt: (0, 0))]
    args = [qm, rowq]
    if mode == "fox":
        in_specs.append(pl.BlockSpec((1, nr, 1), lambda bi, s, pt: (bi, 0, 0)))
        args.append(kw["cq"])
    else:
        in_specs.append(pl.BlockSpec((nr, N_BUCKETS), lambda bi, s, pt: (0, 0)))
        args.append(kw["rowtbl"])
    if mode in ("fox", "sel"):
        aux = kw["aux"]
        in_specs += [pl.BlockSpec((1, aux.shape[1], npg * pw), lambda bi, s, pt: (bi, 0, s)),
                     pl.BlockSpec((1, aux.shape[1], PAGE), lambda bi, s, pt: (bi, 0, (n_pages * pw) // PAGE))]
        args += [aux, aux]
    in_specs += [pl.BlockSpec((1,) + pshape, (lambda bi, s, pt, p=p: (pt[bi * n_pages + s * npg + p], 0, 0)))
                 for p in range(npg)]
    args += [pool] * npg
    in_specs.append(pl.BlockSpec((1,) + tail.shape[1:], lambda bi, s, pt: (bi, 0, 0)))
    args.append(tail)
    grid_spec = pltpu.PrefetchScalarGridSpec(
        num_scalar_prefetch=1, grid=(b, n_steps), in_specs=in_specs,
        out_specs=pl.BlockSpec((1, nr, accw), lambda bi, s, pt: (bi, 0, 0)),
        scratch_shapes=[pltpu.VMEM((nr, 1), F32), pltpu.VMEM((nr, 1), F32), pltpu.VMEM((nr, accw), F32)])
    return pl.pallas_call(
        functools.partial(_decode_kernel, mode=mode, npg=npg, n_pages=n_pages, pw=pw, kpos0=kpos0, qpos0=qpos0,
                          feat_major=feat_major),
        name="decode_" + mode,
        out_shape=jax.ShapeDtypeStruct((b, nr, accw), F32), grid_spec=grid_spec,
        compiler_params=_params(("parallel", "arbitrary")),
    )(pt_flat, *args)


def _nsa_combine_kernel(oc_ref, os_ref, ow_ref, gate_ref, o_ref):
    gt = jax.nn.sigmoid(gate_ref[...])
    o = gt[:, 0:1] * oc_ref[...] + gt[:, 1:2] * os_ref[...] + gt[:, 2:3] * ow_ref[...]
    o_ref[...] = o.astype(o_ref.dtype)


def _nsa_combine(oc, os_, ow, gate):
    return pl.pallas_call(
        _nsa_combine_kernel, name="nsa_combine",
        out_shape=jax.ShapeDtypeStruct(oc.shape, BF16),
    )(oc, os_, ow, gate)


def _cumsum_kernel(pt_ref, bf_ref, cin_ref, *refs, npg, apply_logsig):
    pages = refs[:npg]
    lf_ref, c_ref, carry_ref = refs[npg:]

    @pl.when(pl.program_id(1) == 0)
    def _():
        carry_ref[...] = cin_ref[0]

    tri = (lax.broadcasted_iota(I32, (PAGE, PAGE), 0) <= lax.broadcasted_iota(I32, (PAGE, PAGE), 1)).astype(BF16)
    car = carry_ref[...]
    for p in range(npg):
        x = pages[p][0]
        if apply_logsig:
            z = x + bf_ref[...]
            x = jnp.minimum(z, 0.0) - jnp.log1p(jnp.exp(-jnp.abs(z)))
        lf_ref[0, :, p * PAGE:(p + 1) * PAGE] = x
        loc = None
        for part in _split3(x):
            t = _dot(part, tri)
            loc = t if loc is None else loc + t
        cum = loc + car
        c_ref[0, :, p * PAGE:(p + 1) * PAGE] = cum
        car = cum[:, PAGE - 1:PAGE]
    carry_ref[...] = car


def _cumsum_pages(pool, pt_flat, b, n_pages, b_f, carry_in, apply_logsig, npg=32):
    h = pool.shape[1]
    npg = math.gcd(npg, n_pages)
    in_specs = [pl.BlockSpec((h, 1), lambda bi, s, pt: (0, 0)),
                pl.BlockSpec((1, h, 1), lambda bi, s, pt: (bi, 0, 0))]
    in_specs += [pl.BlockSpec((1, h, PAGE), (lambda bi, s, pt, p=p: (pt[bi * n_pages + s * npg + p], 0, 0)))
                 for p in range(npg)]
    o_spec = pl.BlockSpec((1, h, npg * PAGE), lambda bi, s, pt: (bi, 0, s))
    out = jax.ShapeDtypeStruct((b, h, n_pages * PAGE), F32)
    grid_spec = pltpu.PrefetchScalarGridSpec(
        num_scalar_prefetch=1, grid=(b, n_pages // npg), in_specs=in_specs, out_specs=(o_spec, o_spec),
        scratch_shapes=[pltpu.VMEM((h, 1), F32)])
    return pl.pallas_call(
        functools.partial(_cumsum_kernel, npg=npg, apply_logsig=apply_logsig), name="cumsum_pages",
        out_shape=(out, out), grid_spec=grid_spec,
        compiler_params=_params(("parallel", "arbitrary")),
    )(pt_flat, b_f.reshape(h, 1), carry_in, *([pool] * npg))


def _router_kernel(x_ref, w_ref, o_ref):
    x = x_ref[...]
    w = w_ref[...]
    xh = x.astype(BF16)
    xl = (x - xh.astype(F32)).astype(BF16)
    wh = w.astype(BF16)
    wl = (w - wh.astype(F32)).astype(BF16)
    logits = _dot(xh, wh) + _dot(xl, wh) + _dot(xh, wl)
    lane = lax.broadcasted_iota(I32, logits.shape, 1)
    big = logits.shape[1]
    lg = jnp.where(lane < N_EXPERTS, logits, -jnp.inf)
    m1 = jnp.max(lg, -1, keepdims=True)
    i1 = jnp.min(jnp.where(lg == m1, lane, big), -1, keepdims=True)
    lg2 = jnp.where(lane == i1, -jnp.inf, lg)
    m2 = jnp.max(lg2, -1, keepdims=True)
    i2 = jnp.min(jnp.where(lg2 == m2, lane, big), -1, keepdims=True)
    e2 = jnp.exp(m2 - m1)
    g1 = 1.0 / (1.0 + e2)
    g2 = e2 / (1.0 + e2)
    out = jnp.where(lane == 0, i1.astype(F32), jnp.where(lane == 1, i2.astype(F32),
                    jnp.where(lane == 2, g1, jnp.where(lane == 3, g2, 0.0))))
    o_ref[...] = out


def _router(x, w_router):
    n, dm = x.shape
    tm = _row_tile(n)
    wp = jnp.zeros((dm, 128), F32).at[:, :N_EXPERTS].set(w_router)
    return pl.pallas_call(
        _router_kernel, name="router",
        out_shape=jax.ShapeDtypeStruct((n, 128), F32),
        grid=(n // tm,),
        in_specs=[pl.BlockSpec((tm, dm), lambda i: (i, 0)), _const_spec((dm, 128))],
        out_specs=pl.BlockSpec((tm, 128), lambda i: (i, 0)),
        compiler_params=_params(("parallel",)),
    )(x, wp)


def _gather_rows(idx_ref, n_rows, src_hbm, dst_ref, sem):
    def row_copy(r, src_row):
        return pltpu.make_async_copy(src_hbm.at[pl.ds(src_row, 1)], dst_ref.at[pl.ds(r, 1)], sem)

    def start(r, c):
        row_copy(r, idx_ref[0, 0, r]).start()
        return c

    def wait(r, c):
        row_copy(r, 0).wait()
        return c

    lax.fori_loop(0, n_rows, start, 0)
    lax.fori_loop(0, n_rows, wait, 0)


def _expert_kernel(be_ref, nb_ref, src_ref, x_hbm, wg_ref, wu_ref, wd_ref, o_ref, xbuf, sem):
    i = pl.program_id(0)

    @pl.when(i < nb_ref[0])
    def _():
        _gather_rows(src_ref, MOE_BLOCK, x_hbm, xbuf, sem)
        xb = xbuf[...].astype(BF16)
        h = jax.nn.silu(_dot(xb, wg_ref[0])) * _dot(xb, wu_ref[0])
        o_ref[...] = _dot(h.astype(BF16), wd_ref[0])

    @pl.when(i >= nb_ref[0])
    def _():
        o_ref[...] = jnp.zeros(o_ref.shape, o_ref.dtype)


def _experts(x, src_rows, block_e, n_used, wg, wu, wd):
    nb = src_rows.shape[0]
    src_rows = src_rows.reshape(nb, 1, MOE_BLOCK)
    dm = x.shape[1]
    dff = wg.shape[2]
    grid_spec = pltpu.PrefetchScalarGridSpec(
        num_scalar_prefetch=2, grid=(nb,),
        in_specs=[pl.BlockSpec((1, 1, MOE_BLOCK), lambda i, be, nu: (i, 0, 0), memory_space=pltpu.SMEM),
                  pl.BlockSpec(memory_space=pl.ANY),
                  pl.BlockSpec((1, dm, dff), lambda i, be, nu: (be[i], 0, 0)),
                  pl.BlockSpec((1, dm, dff), lambda i, be, nu: (be[i], 0, 0)),
                  pl.BlockSpec((1, dff, dm), lambda i, be, nu: (be[i], 0, 0))],
        out_specs=pl.BlockSpec((MOE_BLOCK, dm), lambda i, be, nu: (i, 0)),
        scratch_shapes=[pltpu.VMEM((MOE_BLOCK, dm), F32), pltpu.SemaphoreType.DMA(())])
    return pl.pallas_call(
        _expert_kernel, name="experts",
        out_shape=jax.ShapeDtypeStruct((nb * MOE_BLOCK, dm), F32), grid_spec=grid_spec,
        compiler_params=_params(("arbitrary",)),
    )(block_e, n_used, src_rows, x, wg, wu, wd)


def _combine_ln_kernel(pos_ref, y_hbm, x_ref, route_ref, g_ref, b_ref, o_ref, ybuf, sem):
    tm = x_ref.shape[0]
    _gather_rows(pos_ref, TOP_K * tm, y_hbm, ybuf, sem)
    gate = route_ref[...]
    f = gate[:, 2:3] * ybuf[0:tm] + gate[:, 3:4] * ybuf[tm:2 * tm]
    o_ref[...] = _layer_norm(ALPHA * x_ref[...] + f, g_ref[...], b_ref[...])


def _combine_ln(x, route, y_sorted, pos, g, b):
    n, dm = x.shape
    tm = pos.shape[1] // TOP_K
    pos = pos.reshape(n // tm, 1, TOP_K * tm)
    return pl.pallas_call(
        _combine_ln_kernel, name="combine_ln",
        out_shape=jax.ShapeDtypeStruct((n, dm), F32),
        grid=(n // tm,),
        in_specs=[pl.BlockSpec((1, 1, TOP_K * tm), lambda i: (i, 0, 0), memory_space=pltpu.SMEM),
                  pl.BlockSpec(memory_space=pl.ANY),
                  pl.BlockSpec((tm, dm), lambda i: (i, 0)), pl.BlockSpec((tm, 128), lambda i: (i, 0)),
                  _const_spec((1, dm)), _const_spec((1, dm))],
        out_specs=pl.BlockSpec((tm, dm), lambda i: (i, 0)),
        scratch_shapes=[pltpu.VMEM((TOP_K * tm, dm), F32), pltpu.SemaphoreType.DMA(())],
        compiler_params=_params(("arbitrary",)),
    )(pos, y_sorted, x, route, g.reshape(1, dm), b.reshape(1, dm))


def _moe_ln(x, w_router, wg, wu, wd, g, b):
    n, dm = x.shape
    route = _router(x, w_router)
    top_e = route[:, 0:2].astype(I32)
    n_assign = n * TOP_K
    flat_e = top_e.reshape(-1)
    order = jnp.argsort(flat_e)
    e_sorted = flat_e[order]
    counts = jnp.bincount(flat_e, length=N_EXPERTS)
    padded = (counts + MOE_BLOCK - 1) // MOE_BLOCK * MOE_BLOCK
    start = jnp.cumsum(counts) - counts
    pad_end = jnp.cumsum(padded)
    pad_start = pad_end - padded
    dest = (pad_start[e_sorted] + jnp.arange(n_assign) - start[e_sorted]).astype(I32)
    n_blocks = -(-n_assign // MOE_BLOCK) + N_EXPERTS
    block_e = jnp.minimum(jnp.searchsorted(pad_end, jnp.arange(n_blocks) * MOE_BLOCK, side="right"),
                          N_EXPERTS - 1).astype(I32)
    n_used = (pad_end[-1] // MOE_BLOCK).astype(I32).reshape(1)
    src_rows = jnp.zeros((n_blocks * MOE_BLOCK,), I32).at[dest].set((order // TOP_K).astype(I32))
    y_sorted = _experts(x, src_rows.reshape(n_blocks, MOE_BLOCK), block_e, n_used,
                        wg.astype(BF16), wu.astype(BF16), wd.astype(BF16))
    tm = min(_row_tile(n), MOE_BLOCK)
    pos_of = jnp.zeros((n_assign,), I32).at[order].set(dest).reshape(n // tm, tm, TOP_K)
    pos = pos_of.transpose(0, 2, 1).reshape(n // tm, TOP_K * tm)
    return _combine_ln(x, route, y_sorted, pos, g, b)


C0_QDA, C0_DIFF, C0_QNS, C0_CMP, C0_SEL, C0_WIN, C0_GATE, C0 = 0, 512, 1536, 2048, 2304, 2560, 2816, 2944
C1_Q, C1_KV, C1_F, C1 = 0, 1024, 1536, 1664
FOX_DK = 128


def _l0_weight(w_in):
    k_off, v_off = 512, 1024
    cols = list(range(0, 512))
    for h in range(DA_HEADS):
        cols += list(range(k_off + 128 * h, k_off + 128 * (h + 1)))
        cols += list(range(v_off + 128 * h, v_off + 128 * (h + 1)))
    cols += list(range(1536, 2840))
    w = w_in[:, np.asarray(cols)]
    return jnp.pad(w, ((0, 0), (0, C0 - w.shape[1]))).astype(BF16)


def _pad_rows(x, rows):
    return jnp.pad(x, ((0, 0), (0, rows - x.shape[1])) + ((0, 0),) * (x.ndim - 2))


def _fox_augment(q, k, c):
    b, t = c.shape[:2]

    def split3(x):
        def top(v):
            return lax.bitcast_convert_type(lax.bitcast_convert_type(v, jnp.uint32) & jnp.uint32(0xFFFF0000), F32)
        hi = top(x)
        mid = top(x - hi)
        lo = top(x - hi - mid)
        return jnp.stack([hi.astype(BF16), mid.astype(BF16), lo.astype(BF16)], -1)

    c3 = split3(c)
    nc3 = split3(-c).reshape(b, t, FOX_KV_HEADS, FOX_R * 3)
    slot = jnp.asarray(np.repeat(np.eye(FOX_R), 3, axis=1), BF16)
    slot = jnp.broadcast_to(jnp.tile(slot, (FOX_KV_HEADS, 1))[None, None], (b, t, FOX_HEADS, 3 * FOX_R))
    pad_q = jnp.zeros((b, t, FOX_HEADS, FOX_DK - HEAD_DIM - 3 - 3 * FOX_R), BF16)
    qa = jnp.concatenate([q * jnp.asarray(SCALE, BF16), c3, slot, pad_q], -1)
    ones = jnp.ones((b, t, FOX_KV_HEADS, 3), BF16)
    pad_k = jnp.zeros((b, t, FOX_KV_HEADS, FOX_DK - HEAD_DIM - 3 - 3 * FOX_R), BF16)
    ka = jnp.concatenate([k, ones, nc3, pad_k], -1)
    return qa.transpose(0, 2, 1, 3), ka.transpose(0, 2, 1, 3)


def kernel(x_prompt, x_sample, cache_diff_kv, cache_nsa_cmp, cache_nsa_sel, state_nsa_win, cache_fox_kv,
           cache_fox_logf, page_table, rel_bias, l0_w_in, l0_w_out, da_lambda, da_norm, nsa_cmp_pos,
           nsa_cmp_w1, nsa_cmp_w2, ffn_w_gate, ffn_w_up, ffn_w_down, l1_w_in, fox_b_f, l1_w_out,
           moe_router, moe_w_gate, moe_w_up, moe_w_down, ln_g, ln_b):
    b, t, dm = x_prompt.shape
    bs, ts, _ = x_sample.shape
    n_pages = page_table.shape[1]
    n_phys = cache_diff_kv.shape[0]
    past = n_pages * PAGE
    np_, ns = b * t, bs * ts
    pt_flat = page_table.reshape(-1).astype(I32)
    x0 = jnp.concatenate([x_prompt.reshape(np_, dm), x_sample.reshape(ns, dm)], 0)
    g2 = NSA_GROUPS

    p0, p0b = _proj(x0, _l0_weight(l0_w_in))
    bt_c = _bias_tiles(rel_bias, 4, False, 0, DA_HEADS + NSA_HEADS)
    bt_w = _bias_tiles(rel_bias, 6, True, DA_HEADS, NSA_HEADS)
    rel_t = rel_bias.T

    vt_da = p0b[:np_, C0_DIFF:C0_DIFF + 1024].reshape(b, t, DA_HEADS, 4 * HEAD_DIM)[..., 2 * HEAD_DIM:]
    o_da_p = _diff_prompt(p0b, vt_da.transpose(0, 2, 3, 1), rel_bias, da_lambda, da_norm, bt_c, b, t,
                          C0_QDA // 128, C0_DIFF // 128)

    q_ns_p = p0b[:np_, C0_QNS:C0_QNS + 512].reshape(b, t, NSA_HEADS, HEAD_DIM).transpose(0, 2, 1, 3)

    def kv_major(col):
        kv = p0b[:np_, col:col + 256].reshape(b, t, 2, g2, HEAD_DIM)
        return kv[:, :, 0].transpose(0, 2, 1, 3), kv[:, :, 1].transpose(0, 2, 3, 1)

    wbig = _cmp_weight(nsa_cmp_w1)
    part_p = _rowmm(p0[:np_, C0_CMP:C0_CMP + 256].reshape(np_ // CMP_STRIDE, CMP_STRIDE * 256), wbig)
    kc_p, vc_p = _cmp_finish(part_p.reshape(b, t // CMP_STRIDE, 1024), nsa_cmp_pos, nsa_cmp_w1, nsa_cmp_w2)
    oc_p, memb_p = _cmp_prompt(q_ns_p, kc_p, vc_p, rel_bias)
    ks_p, vst_p = kv_major(C0_SEL)
    os_p = _gqa_prompt("sel", q_ns_p, ks_p, vst_p, rel_bias=rel_bias, bt=bt_c, memb=memb_p)
    kw_p, vwt_p = kv_major(C0_WIN)
    gate_p = p0[:np_, C0_GATE:C0_GATE + 3 * NSA_HEADS].reshape(b, t, g2, 3 * NSA_R).transpose(0, 2, 3, 1)
    o_ns_p = _gqa_prompt("win", q_ns_p, kw_p, vwt_p, bt=bt_w, oc=oc_p.transpose(0, 1, 3, 2), os=os_p, gate=gate_p)

    qpos0 = past
    qd = p0b[np_:, C0_QDA:C0_QDA + 512].reshape(bs, ts, DA_HEADS, 2, HEAD_DIM).transpose(0, 3, 2, 1, 4)
    qm_d = jnp.einsum("bwhqd,ws->bwhqsd", qd, jnp.eye(2, dtype=BF16)) * jnp.asarray(SCALE, BF16)
    qm_d = qm_d.reshape(bs, 2 * DA_HEADS * ts, 2 * HEAD_DIM)
    rows_d = [(w, h, q) for w in range(2) for h in range(DA_HEADS) for q in range(ts)]
    rowq_d = jnp.asarray(np.array([[r[2], r[1]] for r in rows_d], np.int32))
    rowtbl_d = rel_t[np.array([r[1] for r in rows_d])]

    def key_rows(x):
        lead = x.shape[:-2]
        x = x.reshape(lead + (PAGE, DA_HEADS, 2, 2 * HEAD_DIM))
        return jnp.swapaxes(x, -3, -2).reshape(lead + (PAGE * 2 * DA_HEADS, 2 * HEAD_DIM))

    tail_d = key_rows(_pad_rows(p0[np_:, C0_DIFF:C0_DIFF + 1024].reshape(bs, ts, 1024), PAGE))
    acc_d = _decode("diffx", qm_d, rowq_d, key_rows(cache_diff_kv.reshape(n_phys, PAGE, 1024)), pt_flat, tail_d,
                    n_pages, 8, 0, qpos0, False, rowtbl=rowtbl_d)
    o12 = acc_d.reshape(bs, 2, DA_HEADS * ts, 2 * HEAD_DIM)
    o_da_s = _diff_finish(o12[:, 0].reshape(bs * DA_HEADS * ts, 128), o12[:, 1].reshape(bs * DA_HEADS * ts, 128),
                          da_lambda, da_norm)
    o_da_s = o_da_s.reshape(bs, DA_HEADS, ts, 128).transpose(0, 2, 1, 3).reshape(ns, 512)

    part_s = _paged_cmp(cache_nsa_cmp.transpose(0, 2, 3, 4, 1).reshape(n_phys, 4 * HEAD_DIM, PAGE), pt_flat, wbig)
    l_tot = past + ts
    if l_tot // CMP_STRIDE > past // CMP_STRIDE:
        raise NotImplementedError("new rows completing a compression chunk")
    kc_s, vc_s = _cmp_finish(part_s.reshape(bs, past // CMP_STRIDE, 1024), nsa_cmp_pos, nsa_cmp_w1, nsa_cmp_w2)
    nbs_s = -(-l_tot // SEL_BLOCK)
    mb_s = -(-nbs_s // 128) * 128
    qn = p0b[np_:, C0_QNS:C0_QNS + 512].reshape(bs, ts, g2, NSA_R, HEAD_DIM) * jnp.asarray(SCALE, BF16)
    q_cmp = _pad_rows(qn.reshape(bs, ts, NSA_HEADS, HEAD_DIM), 8).transpose(0, 2, 1, 3)
    q_cmp = q_cmp.reshape(bs, NSA_HEADS * 8, HEAD_DIM)
    rowtbl_c = rel_t[DA_HEADS + np.repeat(np.arange(NSA_HEADS), 8)]
    oc_s, memb_s = _cmp_sample(q_cmp, rowtbl_c, kc_s, vc_s, qpos0, nbs_s, mb_s)
    oc_s = oc_s.reshape(bs, NSA_HEADS, 8, HEAD_DIM)[:, :, :ts].transpose(0, 2, 1, 3)
    eye_g = jnp.eye(g2, dtype=BF16)
    qm_n = jnp.einsum("bqgrd,gs->bqgrsd", qn, eye_g).reshape(bs, ts * NSA_HEADS, g2 * HEAD_DIM)
    rows_n = [(q, g, r) for q in range(ts) for g in range(g2) for r in range(NSA_R)]
    rowq_n = jnp.asarray(np.array([[r[0], r[1] * NSA_R + r[2]] for r in rows_n], np.int32))
    rowtbl_n = rel_t[DA_HEADS + np.array([r[1] * NSA_R + r[2] for r in rows_n])]
    memb_rows = memb_s.reshape(bs, g2, 8, mb_s)[:, :, :ts, :nbs_s].transpose(0, 2, 1, 3)
    memb_rows = jnp.broadcast_to(memb_rows[:, :, :, None, :, None], (bs, ts, g2, NSA_R, nbs_s, SEL_BLOCK))
    memb_rows = memb_rows.reshape(bs, ts * NSA_HEADS, nbs_s * SEL_BLOCK).astype(BF16)
    memb_rows = jnp.pad(memb_rows, ((0, 0), (0, 0), (0, past + PAGE - nbs_s * SEL_BLOCK)))

    def feat_tail(col, width):
        return _pad_rows(p0[np_:, col:col + width].reshape(bs, ts, width), PAGE).transpose(0, 2, 1)

    pool_sel = cache_nsa_sel.transpose(0, 2, 3, 4, 1).reshape(n_phys, 4 * HEAD_DIM, PAGE)
    acc_sel = _decode("sel", qm_n, rowq_n, pool_sel, pt_flat, feat_tail(C0_SEL, 256),
                      n_pages, 8, 0, qpos0, True, rowtbl=rowtbl_n, aux=memb_rows)
    w_buf = state_nsa_win.shape[1]
    pool_win = state_nsa_win.transpose(0, 2, 3, 4, 1).reshape(bs, 4 * HEAD_DIM, w_buf)
    acc_win = _decode("win", qm_n, rowq_n, pool_win, jnp.arange(bs, dtype=I32), feat_tail(C0_WIN, 256),
                      1, 1, qpos0 - w_buf, qpos0, True, rowtbl=rowtbl_n)

    def pick_group(acc):
        a = acc.reshape(bs, ts, g2, NSA_R, g2, HEAD_DIM)
        return jnp.stack([a[:, :, g, :, g] for g in range(g2)], 2)

    gate_s = p0[np_:, C0_GATE:C0_GATE + 3 * NSA_HEADS].reshape(ns * NSA_HEADS, 3)
    o_ns_s = _nsa_combine(oc_s.reshape(ns * NSA_HEADS, HEAD_DIM), pick_group(acc_sel).reshape(ns * NSA_HEADS, HEAD_DIM),
                          pick_group(acc_win).reshape(ns * NSA_HEADS, HEAD_DIM), gate_s).reshape(ns, 512)

    o_da = jnp.concatenate([o_da_p, o_da_s], 0)
    o_ns = jnp.concatenate([o_ns_p, o_ns_s], 0)
    x1 = _mix_ln([o_da, o_ns], l0_w_out.astype(BF16), x0, ln_g[0], ln_b[0])
    x2 = _ffn_ln(x1, ffn_w_gate.astype(BF16), ffn_w_up.astype(BF16), ffn_w_down.astype(BF16), ln_g[1], ln_b[1])

    w1p = jnp.pad(l1_w_in, ((0, 0), (0, C1 - l1_w_in.shape[1]))).astype(BF16)
    p1, p1b = _proj(x2, w1p)
    fl_p = p1[:np_, C1_F:C1_F + FOX_HEADS].reshape(np_ // PAGE, PAGE, FOX_HEADS).transpose(0, 2, 1)
    lft_p, ct_p = _cumsum_pages(fl_p, jnp.arange(np_ // PAGE, dtype=I32), b, t // PAGE, fox_b_f,
                                jnp.zeros((b, FOX_HEADS, 1), F32), True)
    q_fx = p1b[:np_, C1_Q:C1_Q + 1024].reshape(b, t, FOX_HEADS, HEAD_DIM)
    kv_fx = p1b[:np_, C1_KV:C1_KV + 512].reshape(b, t, 2, FOX_KV_HEADS, HEAD_DIM)
    qa_p, ka_p = _fox_augment(q_fx, kv_fx[:, :, 0], ct_p.transpose(0, 2, 1))
    o_fx_p = _gqa_prompt("fox", qa_p, ka_p, kv_fx[:, :, 1].transpose(0, 2, 3, 1))

    _, ct_past = _cumsum_pages(cache_fox_logf.transpose(0, 2, 1), pt_flat, bs, n_pages, fox_b_f,
                               jnp.zeros((bs, FOX_HEADS, 1), F32), False)
    f_new = _pad_rows(p1[np_:, C1_F:C1_F + FOX_HEADS].reshape(bs, ts, FOX_HEADS), PAGE).transpose(0, 2, 1)
    lft_s, ct_new = _cumsum_pages(f_new, jnp.arange(bs, dtype=I32), bs, 1, fox_b_f, ct_past[:, :, past - 1:past], True)
    ck_s = jnp.concatenate([ct_past, ct_new], 2)
    cq_s = ct_new[:, :, :ts].transpose(0, 2, 1).reshape(bs, ts * FOX_HEADS, 1)
    qf = p1b[np_:, C1_Q:C1_Q + 1024].reshape(bs, ts, FOX_KV_HEADS, FOX_R, HEAD_DIM) * jnp.asarray(SCALE, BF16)
    qm_f = jnp.einsum("bqgrd,gs->bqgrsd", qf, jnp.eye(FOX_KV_HEADS, dtype=BF16))
    qm_f = qm_f.reshape(bs, ts * FOX_HEADS, FOX_KV_HEADS * HEAD_DIM)
    rowq_f = jnp.asarray(np.stack([np.repeat(np.arange(ts), FOX_HEADS), np.tile(np.arange(FOX_HEADS), ts)],
                                  1).astype(np.int32))
    tail_f = _pad_rows(p1[np_:, C1_KV:C1_KV + 512].reshape(bs, ts, 512), PAGE).transpose(0, 2, 1)
    pool_fox = cache_fox_kv.transpose(0, 2, 3, 4, 1).reshape(n_phys, 2 * FOX_KV_HEADS * HEAD_DIM, PAGE)
    acc_f = _decode("fox", qm_f, rowq_f, pool_fox, pt_flat, tail_f, n_pages, 8, 0, qpos0, True, cq=cq_s, aux=ck_s)
    acc_f = acc_f.reshape(bs, ts, FOX_KV_HEADS, FOX_R, FOX_KV_HEADS, HEAD_DIM)
    o_fx_s = jnp.stack([acc_f[:, :, g, :, g] for g in range(FOX_KV_HEADS)], 2)
    o_fx = jnp.concatenate([o_fx_p, o_fx_s.reshape(ns, 1024).astype(BF16)], 0)
    x3 = _mix_ln([o_fx], l1_w_out.astype(BF16), x2, ln_g[2], ln_b[2])

    y = _moe_ln(x3, moe_router, moe_w_gate, moe_w_up, moe_w_down, ln_g[3], ln_b[3])

    diff_all = p0[:, C0_DIFF:C0_DIFF + 1024]
    cmp_all = p0[:, C0_CMP:C0_CMP + 256]
    sel_all = p0[:, C0_SEL:C0_SEL + 256]
    win_all = p0[:, C0_WIN:C0_WIN + 256]
    fkv_all = p1[:, C1_KV:C1_KV + 512]
    keep_p = min(WINDOW, t)
    win_p_out = win_all[:np_].reshape(b, t, 2, g2, HEAD_DIM)[:, t - keep_p:]
    full_win_s = jnp.concatenate([state_nsa_win, win_all[np_:].reshape(bs, ts, 2, g2, HEAD_DIM)], 1)
    keep_s = min(WINDOW, full_win_s.shape[1])
    return (y[:np_].reshape(b, t, dm), y[np_:].reshape(bs, ts, dm),
            diff_all[:np_].reshape(b, t, DA_HEADS, 4 * HEAD_DIM), diff_all[np_:].reshape(bs, ts, DA_HEADS, 4 * HEAD_DIM),
            cmp_all[:np_].reshape(b, t, 2, g2, HEAD_DIM), cmp_all[np_:].reshape(bs, ts, 2, g2, HEAD_DIM),
            sel_all[:np_].reshape(b, t, 2, g2, HEAD_DIM), sel_all[np_:].reshape(bs, ts, 2, g2, HEAD_DIM),
            win_p_out, full_win_s[:, full_win_s.shape[1] - keep_s:],
            fkv_all[:np_].reshape(b, t, 2, FOX_KV_HEADS, HEAD_DIM), fkv_all[np_:].reshape(bs, ts, 2, FOX_KV_HEADS, HEAD_DIM),
            lft_p.transpose(0, 2, 1), lft_s[:, :, :ts].transpose(0, 2, 1))
```

```python
import functools
import math

import numpy as np
import jax
import jax.numpy as jnp
from jax import lax
from jax.experimental import pallas as pl
from jax.experimental.pallas import tpu as pltpu

F32, BF16, I32 = jnp.float32, jnp.bfloat16, jnp.int32

HEAD_DIM = 64
DA_HEADS = 4
NSA_HEADS = 8
NSA_GROUPS = 2
NSA_R = NSA_HEADS // NSA_GROUPS
CMP_BLOCK = 32
CMP_STRIDE = 16
CMP_HIDDEN = 2 * HEAD_DIM
SEL_BLOCK = 64
SEL_TOPK = 16
WINDOW = 512
FOX_HEADS = 16
FOX_KV_HEADS = 4
FOX_R = FOX_HEADS // FOX_KV_HEADS
N_BUCKETS = 32
MAX_DISTANCE = 128
N_EXPERTS = 8
TOP_K = 2
MOE_BLOCK = 128
LN_EPS = 1e-5
RMS_EPS = 1e-5
DEPTH = 2
ALPHA = (2.0 * DEPTH) ** 0.25
DA_LAMBDA_INIT = 0.8 - 0.6 * math.exp(-0.3 * 0)
NEG_INF = -1e30
FORCED_BONUS = 1e4
SCALE = HEAD_DIM ** -0.5

PAGE = 128
BQ = 128
BK = 256
VMEM_LIMIT = 56 * 1024 * 1024


def _t5_thresholds():
    d = np.arange(0, 4 * MAX_DISTANCE)
    df = np.maximum(d, 1).astype(np.float32)
    max_exact = N_BUCKETS // 2
    large = max_exact + (np.log(df / np.float32(max_exact)) / np.float32(math.log(MAX_DISTANCE / max_exact))
                         * np.float32(N_BUCKETS - max_exact)).astype(np.int32)
    bucket = np.where(d < max_exact, d, np.minimum(large, N_BUCKETS - 1))
    return [int(np.argmax(bucket >= b)) for b in range(1, N_BUCKETS)]


T5_THR = _t5_thresholds()
T5_FAR = T5_THR[-1]


def _t5_bias(d, val):
    out = jnp.zeros(d.shape, F32) + val(0)
    for b in range(1, N_BUCKETS):
        out = jnp.where(d >= T5_THR[b - 1], val(b), out)
    return out


def _dot(a, b):
    return jnp.dot(a, b, preferred_element_type=F32)


def _dot_nt(a, b):
    return lax.dot_general(a, b, (((1,), (1,)), ((), ())), preferred_element_type=F32)


def _split3(x):
    hi = x.astype(BF16)
    r1 = x - hi.astype(F32)
    mid = r1.astype(BF16)
    lo = (r1 - mid.astype(F32)).astype(BF16)
    return hi, mid, lo


def _layer_norm(z, g, b):
    mu = jnp.mean(z, -1, keepdims=True)
    zc = z - mu
    var = jnp.mean(zc * zc, -1, keepdims=True)
    return zc * lax.rsqrt(var + LN_EPS) * g + b


def _online_update(s, accum, m_ref, l_ref, acc_ref):
    m_prev = m_ref[...]
    m_new = jnp.maximum(m_prev, jnp.max(s, -1, keepdims=True))
    alpha = jnp.exp(m_prev - m_new)
    p = jnp.exp(s - m_new)
    l_ref[...] = alpha * l_ref[...] + jnp.sum(p, -1, keepdims=True)
    acc_ref[...] = alpha * acc_ref[...] + accum(p)
    m_ref[...] = m_new


def _online_update_t(ss, vts, m_ref, l_ref, acc_ref):
    m_prev = m_ref[...]
    m_new = m_prev
    for s in ss:
        m_new = jnp.maximum(m_new, jnp.max(s, 0, keepdims=True))
    alpha = jnp.exp(m_prev - m_new)
    l_new = alpha * l_ref[...]
    acc_new = alpha * acc_ref[...]
    for s, vt in zip(ss, vts):
        p = jnp.exp(s - m_new)
        l_new = l_new + jnp.sum(p, 0, keepdims=True)
        acc_new = acc_new + _dot(vt, p.astype(BF16))
    l_ref[...] = l_new
    acc_ref[...] = acc_new
    m_ref[...] = m_new


def _init_state(m_ref, l_ref, acc_ref):
    m_ref[...] = jnp.full(m_ref.shape, NEG_INF, F32)
    l_ref[...] = jnp.zeros(l_ref.shape, F32)
    acc_ref[...] = jnp.zeros(acc_ref.shape, F32)


def _row_tile(n):
    for t in (512, 384, 256, 128, 64, 32, 16, 8):
        if n % t == 0:
            return t
    raise ValueError(f"row count {n} is not a multiple of 8")


def _params(sem):
    return pltpu.CompilerParams(dimension_semantics=sem, vmem_limit_bytes=VMEM_LIMIT)


def _const_spec(shape):
    nd = len(shape)
    return pl.BlockSpec(shape, lambda *a: (0,) * nd, pipeline_mode=pl.Buffered(1))


def _proj_kernel(x_ref, w_ref, o_ref, ob_ref):
    y = _dot(x_ref[...].astype(BF16), w_ref[...])
    o_ref[...] = y
    ob_ref[...] = y.astype(BF16)


def _proj(x, w):
    n, k = x.shape
    c = w.shape[1]
    tm = _row_tile(n)
    return pl.pallas_call(
        _proj_kernel, name="proj",
        out_shape=(jax.ShapeDtypeStruct((n, c), F32), jax.ShapeDtypeStruct((n, c), BF16)),
        grid=(n // tm,),
        in_specs=[pl.BlockSpec((tm, k), lambda i: (i, 0)), _const_spec((k, c))],
        out_specs=(pl.BlockSpec((tm, c), lambda i: (i, 0)), pl.BlockSpec((tm, c), lambda i: (i, 0))),
        compiler_params=_params(("parallel",)),
    )(x, w)


def _rowmm_kernel(x_ref, w_ref, o_ref):
    o_ref[...] = _dot(x_ref[...].astype(BF16), w_ref[...])


def _rowmm(x, w):
    n, k = x.shape
    c = w.shape[1]
    tm = _row_tile(n)
    tm = min(tm, 256)
    return pl.pallas_call(
        _rowmm_kernel, name="rowmm",
        out_shape=jax.ShapeDtypeStruct((n, c), F32),
        grid=(n // tm,),
        in_specs=[pl.BlockSpec((tm, k), lambda i: (i, 0)), _const_spec((k, c))],
        out_specs=pl.BlockSpec((tm, c), lambda i: (i, 0)),
        compiler_params=_params(("parallel",)),
    )(x, w)


def _mix_ln_kernel(*refs, n_in):
    a_refs = refs[:n_in]
    w_ref, x_ref, g_ref, b_ref, y_ref = refs[n_in:]
    acc = None
    off = 0
    for a in a_refs:
        k = a.shape[1]
        t = _dot(a[...], w_ref[off:off + k, :])
        acc = t if acc is None else acc + t
        off += k
    y_ref[...] = _layer_norm(ALPHA * x_ref[...] + acc, g_ref[...], b_ref[...])


def _mix_ln(a_list, w, x, g, b):
    n, dm = x.shape
    tm = _row_tile(n)
    in_specs = [pl.BlockSpec((tm, a.shape[1]), lambda i: (i, 0)) for a in a_list]
    in_specs += [_const_spec(w.shape), pl.BlockSpec((tm, dm), lambda i: (i, 0)),
                 _const_spec((1, dm)), _const_spec((1, dm))]
    return pl.pallas_call(
        functools.partial(_mix_ln_kernel, n_in=len(a_list)), name="mix_ln",
        out_shape=jax.ShapeDtypeStruct((n, dm), F32),
        grid=(n // tm,), in_specs=in_specs,
        out_specs=pl.BlockSpec((tm, dm), lambda i: (i, 0)),
        compiler_params=_params(("parallel",)),
    )(*a_list, w, x, g.reshape(1, dm), b.reshape(1, dm))


def _ffn_ln_kernel(x_ref, wg_ref, wu_ref, wd_ref, g_ref, b_ref, y_ref):
    x = x_ref[...]
    xb = x.astype(BF16)
    h = jax.nn.silu(_dot(xb, wg_ref[...])) * _dot(xb, wu_ref[...])
    f = _dot(h.astype(BF16), wd_ref[...])
    y_ref[...] = _layer_norm(ALPHA * x + f, g_ref[...], b_ref[...])


def _ffn_ln(x, wg, wu, wd, g, b):
    n, dm = x.shape
    tm = min(_row_tile(n), 384)
    return pl.pallas_call(
        _ffn_ln_kernel, name="ffn_ln",
        out_shape=jax.ShapeDtypeStruct((n, dm), F32),
        grid=(n // tm,),
        in_specs=[pl.BlockSpec((tm, dm), lambda i: (i, 0)), _const_spec(wg.shape), _const_spec(wu.shape),
                  _const_spec(wd.shape), _const_spec((1, dm)), _const_spec((1, dm))],
        out_specs=pl.BlockSpec((tm, dm), lambda i: (i, 0)),
        compiler_params=_params(("parallel",)),
    )(x, wg, wu, wd, g.reshape(1, dm), b.reshape(1, dm))


def _bias_tiles_kernel(rel_ref, o_ref, *, nt, window, head0):
    h = pl.program_id(0) + head0
    kk = lax.broadcasted_iota(I32, (BK, BQ), 0)
    qq = lax.broadcasted_iota(I32, (BK, BQ), 1)
    for t in range(nt):
        d = t * BQ + qq - kk
        bias = _t5_bias(d, lambda b: rel_ref[b, h])
        bad = d < 0
        if window:
            bad = bad | (d >= WINDOW)
        o_ref[0, t] = jnp.where(bad, NEG_INF, bias)


def _bias_tiles(rel_bias, nt, window, head0, nheads):
    return pl.pallas_call(
        functools.partial(_bias_tiles_kernel, nt=nt, window=window, head0=head0), name="bias_tiles",
        out_shape=jax.ShapeDtypeStruct((nheads, nt, BK, BQ), F32),
        grid=(nheads,),
        in_specs=[pl.BlockSpec(memory_space=pltpu.SMEM)],
        out_specs=pl.BlockSpec((1, nt, BK, BQ), lambda h: (h, 0, 0, 0)),
        compiler_params=_params(("parallel",)),
    )(rel_bias)


def _da_lambda(lp):
    return (jnp.exp(jnp.sum(lp[0:1] * lp[1:2], -1, keepdims=True))
            - jnp.exp(jnp.sum(lp[2:3] * lp[3:4], -1, keepdims=True)) + DA_LAMBDA_INIT)


def _to_rows(xt):
    eye = (lax.broadcasted_iota(I32, (BQ, BQ), 0) == lax.broadcasted_iota(I32, (BQ, BQ), 1)).astype(BF16)
    return _dot_nt(eye, xt)


def _flash_kernel(*refs, mode):
    if mode == "diff":
        rel_ref, lam_ref, nw_ref, bt_ref, q_ref, k_ref, vt_ref, o_ref, m_ref, l_ref, acc_ref = refs
    elif mode == "fox":
        q_ref, k_ref, vt_ref, o_ref, m_ref, l_ref, acc_ref = refs
    elif mode == "sel":
        rel_ref, bt_ref, q_ref, k_ref, vt_ref, memb_ref, o_ref, m_ref, l_ref, acc_ref = refs
    else:
        bt_ref, q_ref, k_ref, vt_ref, oc_ref, os_ref, gate_ref, o_ref, m_ref, l_ref, acc_ref = refs
    g = pl.program_id(1)
    i = pl.program_id(2)
    if mode == "diff":
        jd = i
        part_t0 = (0, 1, 0, 1)
    else:
        jd = lax.shift_right_logical(i, 1)
        t0 = lax.bitwise_and(i, 1)
    if mode == "diff":
        nr = 4
        q = q_ref[...]
        lane = lax.broadcasted_iota(I32, q.shape, 1)
        zero = jnp.zeros_like(q)
        qs = jnp.concatenate([jnp.where(lane < HEAD_DIM, q, zero), jnp.where(lane >= HEAD_DIM, q, zero)], 0)
        qs = qs * jnp.asarray(SCALE, BF16)
    else:
        nr = q_ref.shape[1]
        qs = q_ref[0].reshape(nr * BQ, q_ref.shape[3])
        if mode != "fox":
            qs = qs * jnp.asarray(SCALE, BF16)
    _init_state(m_ref, l_ref, acc_ref)

    def scores(j, back, diag):
        off = pl.multiple_of(j * BK, BK)
        if mode == "diff":
            kt = k_ref[pl.ds(off, BK), :]
        else:
            kt = k_ref[0, 0, pl.ds(off, BK), :]
        s = _dot_nt(kt, qs)
        if mode == "fox":
            if diag:
                kk = lax.broadcasted_iota(I32, (BK, BQ), 0)
                qq = lax.broadcasted_iota(I32, (BK, BQ), 1)
                ok = (t0 * BQ + qq - kk) >= 0
                s = jnp.where(jnp.concatenate([ok] * nr, 1), s, NEG_INF)
        else:
            extra = None
            if mode == "sel":
                nbs = memb_ref.shape[2]
                kk = lax.broadcasted_iota(I32, (BK, nbs), 0)
                jb = lax.broadcasted_iota(I32, (BK, nbs), 1)
                e = (jb == j * (BK // SEL_BLOCK) + lax.shift_right_logical(kk, 6)).astype(BF16)
                extra = (_dot(e, memb_ref[0, 0]) - 1.0) * (-NEG_INF)
            parts = []
            for r in range(nr):
                if back is None:
                    br = rel_ref[N_BUCKETS - 1, g if mode == "diff" else DA_HEADS + g * nr + r]
                elif mode == "diff":
                    br = bt_ref[0, part_t0[r] + 2 * back]
                else:
                    br = bt_ref[r, t0 + 2 * back]
                if extra is not None:
                    br = br + extra
                parts.append(s[:, r * BQ:(r + 1) * BQ] + br)
            s = jnp.concatenate(parts, 1)
        return s, vt_ref[0, 0, :, pl.ds(off, BK)]

    def tiles(spec):
        ss, vts = zip(*[scores(*a) for a in spec])
        _online_update_t(ss, vts, m_ref, l_ref, acc_ref)

    if mode != "win":
        n_far = jd if mode == "fox" else jnp.maximum(jd - 1, 0)

        def far_body(j4, c):
            tiles([(4 * j4 + u, None, False) for u in range(4)])
            return c

        lax.fori_loop(0, lax.shift_right_logical(n_far, 2), far_body, 0)
        done = lax.bitwise_and(n_far, -4)

        @pl.when(lax.bitwise_and(n_far, 2) != 0)
        def _():
            tiles([(done, None, False), (done + 1, None, False)])

        @pl.when(lax.bitwise_and(n_far, 1) != 0)
        def _():
            tiles([(n_far - 1, None, False)])
    n_near = 1 if mode == "fox" else (3 if mode == "win" else 2)
    for cnt in range(1, n_near + 1):
        cond = (jd >= cnt - 1) if cnt == n_near else (jd == cnt - 1)

        @pl.when(cond)
        def _(cnt=cnt):
            tiles([(jd - a, a, a == 0) for a in reversed(range(cnt))])

    ot = acc_ref[...] / l_ref[...]
    if mode == "diff":
        a = ot[:, :2 * BQ] - _da_lambda(lam_ref[...]) * ot[:, 2 * BQ:]
        r = a * lax.rsqrt(jnp.mean(a * a, 0, keepdims=True) + RMS_EPS) * nw_ref[...] * (1.0 - DA_LAMBDA_INIT)
        r = r.astype(BF16)
        o_ref[...] = jnp.concatenate([_to_rows(r[:, :BQ]), _to_rows(r[:, BQ:])], 0).astype(o_ref.dtype)
    elif mode == "fox":
        o_ref[...] = jnp.concatenate([_to_rows(ot[:, r * BQ:(r + 1) * BQ].astype(BF16)) for r in range(nr)],
                                     1).astype(o_ref.dtype)
    elif mode == "sel":
        for r in range(nr):
            o_ref[0, r] = ot[:, r * BQ:(r + 1) * BQ]
    else:
        gt = jax.nn.sigmoid(gate_ref[0, 0])
        parts = []
        for r in range(nr):
            comb = (gt[3 * r:3 * r + 1] * oc_ref[0, r] + gt[3 * r + 1:3 * r + 2] * os_ref[0, r]
                    + gt[3 * r + 2:3 * r + 3] * ot[:, r * BQ:(r + 1) * BQ])
            parts.append(_to_rows(comb.astype(BF16)))
        o_ref[...] = jnp.concatenate(parts, 1).astype(o_ref.dtype)


def _flash_scratch(nr, dv):
    return [pltpu.VMEM((1, nr * BQ), F32), pltpu.VMEM((1, nr * BQ), F32), pltpu.VMEM((dv, nr * BQ), F32)]


def _diff_prompt(p0b, vt, rel_bias, da_lambda, da_norm, bt, b, t, col_q, col_k):
    nq = t // BK
    dv = 2 * HEAD_DIM
    return pl.pallas_call(
        functools.partial(_flash_kernel, mode="diff"), name="diff_prompt",
        out_shape=jax.ShapeDtypeStruct((b * t, DA_HEADS * dv), BF16),
        grid=(b, DA_HEADS, nq),
        in_specs=[pl.BlockSpec(memory_space=pltpu.SMEM),
                  _const_spec((4, HEAD_DIM)), _const_spec((dv, 1)),
                  pl.BlockSpec((1, 4, BK, BQ), lambda bi, h, i: (h, 0, 0, 0)),
                  pl.BlockSpec((BK, 128), lambda bi, h, i: (bi * nq + i, col_q + h)),
                  pl.BlockSpec((t, 128), lambda bi, h, i: (bi, col_k + 2 * h)),
                  pl.BlockSpec((1, 1, dv, t), lambda bi, h, i: (bi, h, 0, 0))],
        out_specs=pl.BlockSpec((BK, dv), lambda bi, h, i: (bi * nq + i, h)),
        scratch_shapes=_flash_scratch(4, dv),
        compiler_params=_params(("parallel", "parallel", "arbitrary")),
    )(rel_bias, da_lambda, da_norm.reshape(-1, 1), bt, p0b, p0b, vt)


def _gqa_prompt(mode, q, k, vt, **kw):
    b, hq, t, dk = q.shape
    hk = k.shape[1]
    nr = hq // hk
    nq = t // BQ
    q_spec = pl.BlockSpec((1, nr, BQ, dk), lambda bi, g, i: (bi, g, i, 0))
    k_spec = pl.BlockSpec((1, 1, t, dk), lambda bi, g, i: (bi, g, 0, 0))
    vt_spec = pl.BlockSpec((1, 1, HEAD_DIM, t), lambda bi, g, i: (bi, g, 0, 0))
    ot_spec = pl.BlockSpec((1, nr, HEAD_DIM, BQ), lambda bi, g, i: (bi, g, 0, i))
    smem = pl.BlockSpec(memory_space=pltpu.SMEM)
    row_out = pl.BlockSpec((BQ, nr * HEAD_DIM), lambda bi, g, i: (bi * nq + i, g))
    if mode == "fox":
        in_specs = [q_spec, k_spec, vt_spec]
        args = (q, k, vt)
        out_shape = jax.ShapeDtypeStruct((b * t, hq * HEAD_DIM), BF16)
        out_spec = row_out
    elif mode == "sel":
        nbs = kw["memb"].shape[2]
        in_specs = [smem, pl.BlockSpec((nr, 4, BK, BQ), lambda bi, g, i: (1 + g, 0, 0, 0)),
                    q_spec, k_spec, vt_spec,
                    pl.BlockSpec((1, 1, nbs, BQ), lambda bi, g, i: (bi, g, 0, i))]
        args = (kw["rel_bias"], kw["bt"], q, k, vt, kw["memb"])
        out_shape = jax.ShapeDtypeStruct((b, hq, HEAD_DIM, t), F32)
        out_spec = ot_spec
    else:
        in_specs = [pl.BlockSpec((nr, 6, BK, BQ), lambda bi, g, i: (g, 0, 0, 0)),
                    q_spec, k_spec, vt_spec, ot_spec, ot_spec,
                    pl.BlockSpec((1, 1, 3 * nr, BQ), lambda bi, g, i: (bi, g, 0, i))]
        args = (kw["bt"], q, k, vt, kw["oc"], kw["os"], kw["gate"])
        out_shape = jax.ShapeDtypeStruct((b * t, hq * HEAD_DIM), BF16)
        out_spec = row_out
    return pl.pallas_call(
        functools.partial(_flash_kernel, mode=mode), name="gqa_" + mode,
        out_shape=out_shape, grid=(b, hk, nq), in_specs=in_specs, out_specs=out_spec,
        scratch_shapes=_flash_scratch(nr, HEAD_DIM),
        compiler_params=_params(("parallel", "parallel", "arbitrary")),
    )(*args)


def _diff_finish_kernel(o1_ref, o2_ref, lam_ref, nw_ref, o_ref):
    a = o1_ref[...] - _da_lambda(lam_ref[...]) * o2_ref[...]
    r = a * lax.rsqrt(jnp.mean(a * a, -1, keepdims=True) + RMS_EPS) * nw_ref[...] * (1.0 - DA_LAMBDA_INIT)
    o_ref[...] = r.astype(o_ref.dtype)


def _diff_finish(o1, o2, da_lambda, da_norm):
    n, w = o1.shape
    return pl.pallas_call(
        _diff_finish_kernel, name="diff_finish",
        out_shape=jax.ShapeDtypeStruct((n, w), BF16),
    )(o1, o2, da_lambda, da_norm.reshape(1, -1))


def _cmp_weight(w1):
    w1r = w1.reshape(2, 2, CMP_STRIDE, HEAD_DIM, CMP_HIDDEN)
    w1c = w1r[jnp.array([0, 0, 1, 1])]
    wb = jnp.einsum("chrdn,ce->rcdehn", w1c, jnp.eye(4, dtype=w1.dtype))
    return wb.reshape(CMP_STRIDE * 4 * HEAD_DIM, 4 * 2 * CMP_HIDDEN).astype(BF16)


def _cmp_finish_kernel(part_ref, pos_ref, w1_ref, w2_ref, kc_ref, vc_ref):
    nch = part_ref.shape[1]
    row = lax.broadcasted_iota(I32, (nch, HEAD_DIM), 0)
    for kv in range(2):
        posw = _dot(pos_ref[kv], w1_ref[kv])[0:1]
        for g in range(NSA_GROUPS):
            c = kv * NSA_GROUPS + g
            a = part_ref[0, :, 256 * c:256 * c + 128]
            bm = part_ref[0, :, 256 * c + 128:256 * c + 256]
            hid = a + pltpu.roll(bm, nch - 1, 0) + posw
            y = _dot(jax.nn.gelu(hid).astype(BF16), w2_ref[kv])
            y = jnp.where(row < nch - 1, y, 0.0).astype(BF16)
            if kv == 0:
                kc_ref[0, g] = y
            else:
                vc_ref[0, g] = y


def _cmp_finish(part, cmp_pos, w1, w2):
    b, nch, _ = part.shape
    pos8 = jnp.broadcast_to(cmp_pos.reshape(2, 1, CMP_BLOCK * HEAD_DIM), (2, 8, CMP_BLOCK * HEAD_DIM)).astype(BF16)
    out = jax.ShapeDtypeStruct((b, NSA_GROUPS, nch, HEAD_DIM), BF16)
    o_spec = pl.BlockSpec((1, NSA_GROUPS, nch, HEAD_DIM), lambda bi: (bi, 0, 0, 0))
    return pl.pallas_call(
        _cmp_finish_kernel, name="cmp_finish",
        out_shape=(out, out), grid=(b,),
        in_specs=[pl.BlockSpec((1, nch, 1024), lambda bi: (bi, 0, 0)), _const_spec(pos8.shape),
                  _const_spec(w1.shape), _const_spec(w2.shape)],
        out_specs=(o_spec, o_spec),
        compiler_params=_params(("parallel",)),
    )(part, pos8, w1.astype(BF16), w2.astype(BF16))


def _paged_cmp_kernel(pt_ref, *refs, npg):
    pages = refs[:npg]
    w_ref, o_ref = refs[npg:]
    nchunk = PAGE // CMP_STRIDE
    feat = pages[0].shape[1]
    ri = lax.broadcasted_iota(I32, (PAGE, PAGE), 0)
    ci = lax.broadcasted_iota(I32, (PAGE, PAGE), 1)
    perm = (ci == CMP_STRIDE * lax.bitwise_and(ri, nchunk - 1) + lax.shift_right_logical(ri, 3)).astype(BF16)
    xps = [_dot_nt(perm, pg[0].astype(BF16)) for pg in pages]
    acc = None
    for r in range(CMP_STRIDE):
        lhs = jnp.concatenate([xp[nchunk * r:nchunk * (r + 1)] for xp in xps], 0).astype(BF16)
        t = _dot(lhs, w_ref[feat * r:feat * (r + 1), :])
        acc = t if acc is None else acc + t
    o_ref[...] = acc


def _paged_cmp(pool, pt_flat, w, npg=16):
    n = pt_flat.shape[0]
    _, f, _ = pool.shape
    c = w.shape[1]
    npg = math.gcd(npg, n)
    nchunk = PAGE // CMP_STRIDE
    specs = [pl.BlockSpec((1, f, PAGE), (lambda s, pt, p=p: (pt[s * npg + p], 0, 0))) for p in range(npg)]
    grid_spec = pltpu.PrefetchScalarGridSpec(
        num_scalar_prefetch=1, grid=(n // npg,),
        in_specs=specs + [pl.BlockSpec(w.shape, lambda s, pt: (0, 0), pipeline_mode=pl.Buffered(1))],
        out_specs=pl.BlockSpec((npg * nchunk, c), lambda s, pt: (s, 0)))
    return pl.pallas_call(
        functools.partial(_paged_cmp_kernel, npg=npg), name="paged_cmp",
        out_shape=jax.ShapeDtypeStruct((n * nchunk, c), F32), grid_spec=grid_spec,
        compiler_params=_params(("parallel",)),
    )(pt_flat, *([pool] * npg), w)


def _overlap_t(nbs, ncp):
    jb = lax.broadcasted_iota(I32, (nbs, ncp), 0) * SEL_BLOCK
    ci = lax.broadcasted_iota(I32, (nbs, ncp), 1) * CMP_STRIDE
    return ((ci < jb + SEL_BLOCK) & (ci + CMP_BLOCK > jb)).astype(BF16)


def _cmp_prompt_kernel(rel_ref, q_ref, kc_ref, vc_ref, oc_ref, memb_ref, *, nbs, k_eff):
    g = pl.program_id(1)
    i = pl.program_id(2)
    nr = q_ref.shape[1]
    ncp = kc_ref.shape[2]
    qs = q_ref[0].reshape(nr * BQ, HEAD_DIM) * jnp.asarray(SCALE, BF16)
    s = _dot_nt(qs, kc_ref[0, 0])
    pos = i * BQ + lax.broadcasted_iota(I32, (BQ, ncp), 0)
    d = pos - (lax.broadcasted_iota(I32, (BQ, ncp), 1) * CMP_STRIDE + CMP_BLOCK - 1)
    ok = d >= 0
    okf = ok.astype(F32)
    p_parts = []
    seg = 128 if ncp % 128 == 0 else ncp
    for r in range(nr):
        h = DA_HEADS + g * nr + r
        pieces = []
        for c0 in range(0, ncp, seg):
            ds = d[:, c0:c0 + seg]
            end_lo = CMP_STRIDE * c0 + CMP_BLOCK - 1
            end_hi = CMP_STRIDE * (c0 + seg - 1) + CMP_BLOCK - 1
            hit = (i * BQ - end_hi < T5_FAR) & (i * BQ + BQ - 1 - end_lo >= 0)
            bias = lax.cond(hit, lambda ds=ds, h=h: _t5_bias(ds, lambda b: rel_ref[b, h]),
                            lambda ds=ds, h=h: jnp.zeros(ds.shape, F32) + rel_ref[N_BUCKETS - 1, h])
            pieces.append(jnp.where(ok[:, c0:c0 + seg], s[r * BQ:(r + 1) * BQ, c0:c0 + seg] + bias, NEG_INF))
        sr = pieces[0] if len(pieces) == 1 else jnp.concatenate(pieces, 1)
        e = jnp.exp(sr - jnp.max(sr, -1, keepdims=True))
        p_parts.append(e / jnp.sum(e, -1, keepdims=True) * okf)
    p = jnp.concatenate(p_parts, 0)
    oc_ref[0] = _dot(p.astype(BF16), vc_ref[0, 0]).reshape(nr, BQ, HEAD_DIM)
    psum = p_parts[0]
    for r in range(1, nr):
        psum = psum + p_parts[r]
    ot = _overlap_t(nbs, ncp)
    imp = None
    for part in _split3(psum):
        t = _dot_nt(ot, part)
        imp = t if imp is None else imp + t
    jrow = lax.broadcasted_iota(I32, (nbs, BQ), 0)
    cur = lax.shift_right_logical(i * BQ + lax.broadcasted_iota(I32, (nbs, BQ), 1), 6)
    valid = jrow <= cur
    forced = valid & ((jrow == 0) | (jrow >= cur - 1))
    score = jnp.where(valid, imp + jnp.where(forced, FORCED_BONUS, 0.0), -1.0)
    rank = jnp.zeros((nbs, BQ), I32)
    for ii in range(nbs):
        row = score[ii:ii + 1, :]
        ahead = (row > score) | ((row == score) & (jrow > ii))
        rank = rank + ahead.astype(I32)
    memb_ref[0, 0] = ((rank < k_eff) & (score >= 0.0)).astype(memb_ref.dtype)


def _cmp_prompt(q, kc, vc, rel_bias):
    b, hq, t, _ = q.shape
    nr = hq // NSA_GROUPS
    nq = t // BQ
    ncp = kc.shape[2]
    nbs = -(-t // SEL_BLOCK)
    q_spec = pl.BlockSpec((1, nr, BQ, HEAD_DIM), lambda bi, g, i: (bi, g, i, 0))
    c_spec = pl.BlockSpec((1, 1, ncp, HEAD_DIM), lambda bi, g, i: (bi, g, 0, 0))
    return pl.pallas_call(
        functools.partial(_cmp_prompt_kernel, nbs=nbs, k_eff=min(SEL_TOPK, nbs)), name="cmp_prompt",
        out_shape=(jax.ShapeDtypeStruct((b, hq, t, HEAD_DIM), F32),
                   jax.ShapeDtypeStruct((b, NSA_GROUPS, nbs, t), BF16)),
        grid=(b, NSA_GROUPS, nq),
        in_specs=[pl.BlockSpec(memory_space=pltpu.SMEM), q_spec, c_spec, c_spec],
        out_specs=(q_spec, pl.BlockSpec((1, 1, nbs, BQ), lambda bi, g, i: (bi, g, 0, i))),
        compiler_params=_params(("parallel", "parallel", "parallel")),
    )(rel_bias, q, kc, vc)


def _cmp_sample_kernel(q_ref, rowtbl_ref, kc_ref, vc_ref, oc_ref, memb_ref, *, qpos0, nbs, k_eff):
    ncp = kc_ref.shape[2]
    mb = memb_ref.shape[2]
    rows = NSA_R * 8
    qi = lax.bitwise_and(lax.broadcasted_iota(I32, (rows, ncp), 0), 7)
    d = qpos0 + qi - (lax.broadcasted_iota(I32, (rows, ncp), 1) * CMP_STRIDE + CMP_BLOCK - 1)
    ok = d >= 0
    jb = lax.broadcasted_iota(I32, (ncp, mb), 1) * SEL_BLOCK
    ci = lax.broadcasted_iota(I32, (ncp, mb), 0) * CMP_STRIDE
    ov = ((ci < jb + SEL_BLOCK) & (ci + CMP_BLOCK > jb)).astype(BF16)
    jl = lax.broadcasted_iota(I32, (8, mb), 1)
    cur = lax.shift_right_logical(qpos0 + lax.broadcasted_iota(I32, (8, mb), 0), 6)
    valid = (jl <= cur) & (jl < nbs)
    forced = valid & ((jl == 0) | (jl >= cur - 1))
    for g in range(NSA_GROUPS):
        s = _dot_nt(q_ref[0, g * rows:(g + 1) * rows], kc_ref[0, g])
        tbl = rowtbl_ref[g * rows:(g + 1) * rows]
        sm = jnp.where(ok, s + _t5_bias(d, lambda b: tbl[:, b:b + 1]), NEG_INF)
        e = jnp.exp(sm - jnp.max(sm, -1, keepdims=True))
        p = e / jnp.sum(e, -1, keepdims=True) * ok.astype(F32)
        oc_ref[0, g * rows:(g + 1) * rows] = _dot(p.astype(BF16), vc_ref[0, g])
        psum = p[0:8]
        for r in range(1, NSA_R):
            psum = psum + p[8 * r:8 * r + 8]
        imp = None
        for part in _split3(psum):
            t = _dot(part, ov)
            imp = t if imp is None else imp + t
        score = jnp.where(valid, imp + jnp.where(forced, FORCED_BONUS, 0.0), -1.0)
        score = jnp.where(jl < nbs, score, -2.0)
        rank = jnp.zeros((8, mb), I32)
        for ii in range(nbs):
            col = score[:, ii:ii + 1]
            ahead = (col > score) | ((col == score) & (jl > ii))
            rank = rank + ahead.astype(I32)
        memb_ref[0, g * 8:(g + 1) * 8] = ((rank < k_eff) & (score >= 0.0)).astype(F32)


def _cmp_sample(qrows, rowtbl, kc, vc, qpos0, nbs, mb):
    b = qrows.shape[0]
    ncp = kc.shape[2]
    nrows = NSA_GROUPS * NSA_R * 8
    c_spec = pl.BlockSpec((1, NSA_GROUPS, ncp, HEAD_DIM), lambda bi: (bi, 0, 0, 0))
    return pl.pallas_call(
        functools.partial(_cmp_sample_kernel, qpos0=qpos0, nbs=nbs, k_eff=min(SEL_TOPK, nbs)),
        name="cmp_sample",
        out_shape=(jax.ShapeDtypeStruct((b, nrows, HEAD_DIM), F32),
                   jax.ShapeDtypeStruct((b, NSA_GROUPS * 8, mb), F32)),
        grid=(b,),
        in_specs=[pl.BlockSpec((1, nrows, HEAD_DIM), lambda bi: (bi, 0, 0)), _const_spec(rowtbl.shape),
                  c_spec, c_spec],
        out_specs=(pl.BlockSpec((1, nrows, HEAD_DIM), lambda bi: (bi, 0, 0)),
                   pl.BlockSpec((1, NSA_GROUPS * 8, mb), lambda bi: (bi, 0, 0))),
        compiler_params=_params(("parallel",)),
    )(qrows, rowtbl, kc, vc)


def _decode_kernel(pt_ref, *refs, mode, npg, n_pages, pw, kpos0, qpos0, feat_major):
    q_ref, rowq_ref = refs[0], refs[1]
    k = 2
    if mode == "fox":
        cq_ref, aux_ref, auxtail_ref = refs[k:k + 3]
        k += 3
    else:
        rowtbl_ref = refs[k]
        k += 1
        if mode == "sel":
            aux_ref, auxtail_ref = refs[k:k + 2]
            k += 2
    pages = refs[k:k + npg]
    tail_ref, o_ref, m_ref, l_ref, acc_ref = refs[k + npg:k + npg + 5]
    step = pl.program_id(1)
    nr = q_ref.shape[1]
    q = q_ref[0]
    rowpos = qpos0 + rowq_ref[:, 0:1]
    key_shift = 3 if mode == "diffx" else 0

    @pl.when(step == 0)
    def _():
        _init_state(m_ref, l_ref, acc_ref)
        if mode == "diffx":
            far_ref = refs[k + npg + 5]
            col = lax.broadcasted_iota(I32, far_ref.shape, 1)
            far_ref[...] = jnp.where(lax.bitwise_and(col, 7) == rowq_ref[:, 1:2],
                                     rowtbl_ref[:, N_BUCKETS - 1:N_BUCKETS], NEG_INF)

    def process(page_list, kp0, aux, is_tail):
        kbs = [pg.astype(BF16) for pg in page_list]
        if feat_major:
            half = kbs[0].shape[0] // 2
            ss = [_dot(q, kb[:half]) for kb in kbs]
            widths = [kb.shape[1] for kb in kbs]
        else:
            ss = [_dot_nt(q, kb) for kb in kbs]
            widths = [kb.shape[0] for kb in kbs]
        s = ss[0] if len(ss) == 1 else jnp.concatenate(ss, 1)
        n = s.shape[1]

        def distance():
            col = lax.broadcasted_iota(I32, (nr, n), 1)
            return col, rowpos - (kp0 + lax.shift_right_logical(col, key_shift))

        def biased(sv):
            col, d = distance()
            tbl = rowtbl_ref[...]
            bad = d < 0
            if mode == "win":
                bad = bad | (d >= WINDOW)
            if mode == "sel":
                bad = bad | (aux < 0.5 * NEG_INF)
            if mode == "diffx":
                bad = bad | (lax.bitwise_and(col, 7) != rowq_ref[:, 1:2])
            return jnp.where(bad, NEG_INF, sv + _t5_bias(d, lambda b: tbl[:, b:b + 1]))

        if mode == "fox":
            s = s + (cq_ref[0] - jnp.concatenate([aux] * (nr // FOX_HEADS), 0))
            if is_tail:
                s = jnp.where(distance()[1] < 0, NEG_INF, s)
        elif is_tail or mode == "win":
            s = biased(s)
        else:
            near = qpos0 - (kp0 + (n >> key_shift) - 1) < T5_FAR
            far_bias = refs[k + npg + 5][...] if mode == "diffx" else aux
            s = lax.cond(near, lambda: biased(s), lambda: s + far_bias)

        def accum(p):
            out = None
            off = 0
            for kb, w in zip(kbs, widths):
                pp = p[:, off:off + w]
                if mode == "diffx":
                    pp = pltpu.roll(pp, 4, 1)
                pp = pp.astype(BF16)
                t = _dot_nt(pp, kb[half:]) if feat_major else _dot(pp, kb)
                out = t if out is None else out + t
                off += w
            return out

        _online_update(s, accum, m_ref, l_ref, acc_ref)

    keys_pp = pw >> key_shift
    process([pg[0] for pg in pages], kpos0 + step * (npg * keys_pp),
            aux_ref[0] if mode in ("fox", "sel") else None, False)

    @pl.when(step == pl.num_programs(1) - 1)
    def _():
        process([tail_ref[0]], qpos0, auxtail_ref[0] if mode in ("fox", "sel") else None, True)
        o_ref[0] = acc_ref[...] / l_ref[...]


def _decode(mode, qm, rowq, pool, pt_flat, tail, n_pages, npg, kpos0, qpos0, feat_major, **kw):
    b, nr, _ = qm.shape
    n_steps = n_pages // npg
    pshape = pool.shape[1:]
    pw = pshape[1] if feat_major else pshape[0]
    accw = pshape[0] // 2 if feat_major else pshape[1]
    in_specs = [pl.BlockSpec((1,) + qm.shape[1:], lambda bi, s, pt: (bi, 0, 0)),
                pl.BlockSpec((nr, 2), lambda bi, s, pt: (0, 0))]
    args = [qm, rowq]
    if mode == "fox":
        in_specs.append(pl.BlockSpec((1, nr, 1), lambda bi, s, pt: (bi, 0, 0)))
        args.append(kw["cq"])
    else:
        in_specs.append(pl.BlockSpec((nr, N_BUCKETS), lambda bi, s, pt: (0, 0)))
        args.append(kw["rowtbl"])
    if mode in ("fox", "sel"):
        aux = kw["aux"]
        in_specs += [pl.BlockSpec((1, aux.shape[1], npg * pw), lambda bi, s, pt: (bi, 0, s)),
                     pl.BlockSpec((1, aux.shape[1], PAGE), lambda bi, s, pt: (bi, 0, (n_pages * pw) // PAGE))]
        args += [aux, aux]
    in_specs += [pl.BlockSpec((1,) + pshape, (lambda bi, s, pt, p=p: (pt[bi * n_pages + s * npg + p], 0, 0)))
                 for p in range(npg)]
    args += [pool] * npg
    in_specs.append(pl.BlockSpec((1,) + tail.shape[1:], lambda bi, s, pt: (bi, 0, 0)))
    args.append(tail)
    scratch = [pltpu.VMEM((nr, 1), F32), pltpu.VMEM((nr, 1), F32), pltpu.VMEM((nr, accw), F32)]
    if mode == "diffx":
        scratch.append(pltpu.VMEM((nr, npg * pw), F32))
    grid_spec = pltpu.PrefetchScalarGridSpec(
        num_scalar_prefetch=1, grid=(b, n_steps), in_specs=in_specs,
        out_specs=pl.BlockSpec((1, nr, accw), lambda bi, s, pt: (bi, 0, 0)),
        scratch_shapes=scratch)
    return pl.pallas_call(
        functools.partial(_decode_kernel, mode=mode, npg=npg, n_pages=n_pages, pw=pw, kpos0=kpos0, qpos0=qpos0,
                          feat_major=feat_major),
        name="decode_" + mode,
        out_shape=jax.ShapeDtypeStruct((b, nr, accw), F32), grid_spec=grid_spec,
        compiler_params=_params(("parallel", "arbitrary")),
    )(pt_flat, *args)


def _nsa_combine_kernel(oc_ref, os_ref, ow_ref, gate_ref, o_ref):
    gt = jax.nn.sigmoid(gate_ref[...])
    o = gt[:, 0:1] * oc_ref[...] + gt[:, 1:2] * os_ref[...] + gt[:, 2:3] * ow_ref[...]
    o_ref[...] = o.astype(o_ref.dtype)


def _nsa_combine(oc, os_, ow, gate):
    return pl.pallas_call(
        _nsa_combine_kernel, name="nsa_combine",
        out_shape=jax.ShapeDtypeStruct(oc.shape, BF16),
    )(oc, os_, ow, gate)


def _cumsum_kernel(pt_ref, bf_ref, cin_ref, *refs, npg, apply_logsig):
    pages = refs[:npg]
    lf_ref, c_ref, carry_ref = refs[npg:]

    @pl.when(pl.program_id(1) == 0)
    def _():
        carry_ref[...] = cin_ref[0]

    tri = (lax.broadcasted_iota(I32, (PAGE, PAGE), 0) <= lax.broadcasted_iota(I32, (PAGE, PAGE), 1)).astype(BF16)
    car = carry_ref[...]
    h = pages[0].shape[1]
    group = max(1, PAGE // h)
    for p0 in range(0, npg, group):
        xs = []
        for p in range(p0, min(p0 + group, npg)):
            x = pages[p][0]
            if apply_logsig:
                z = x + bf_ref[...]
                x = jnp.minimum(z, 0.0) - jnp.log1p(jnp.exp(-jnp.abs(z)))
            lf_ref[0, :, p * PAGE:(p + 1) * PAGE] = x
            xs.append(x)
        xcat = xs[0] if len(xs) == 1 else jnp.concatenate(xs, 0)
        loc = None
        for part in _split3(xcat):
            t = _dot(part, tri)
            loc = t if loc is None else loc + t
        for u, p in enumerate(range(p0, min(p0 + group, npg))):
            loc_p = loc[u * h:(u + 1) * h]
            c_ref[0, :, p * PAGE:(p + 1) * PAGE] = loc_p + car
            car = car + loc_p[:, PAGE - 1:PAGE]
    carry_ref[...] = car


def _cumsum_pages(pool, pt_flat, b, n_pages, b_f, carry_in, apply_logsig, npg=32):
    h = pool.shape[1]
    npg = math.gcd(npg, n_pages)
    in_specs = [pl.BlockSpec((h, 1), lambda bi, s, pt: (0, 0)),
                pl.BlockSpec((1, h, 1), lambda bi, s, pt: (bi, 0, 0))]
    in_specs += [pl.BlockSpec((1, h, PAGE), (lambda bi, s, pt, p=p: (pt[bi * n_pages + s * npg + p], 0, 0)))
                 for p in range(npg)]
    o_spec = pl.BlockSpec((1, h, npg * PAGE), lambda bi, s, pt: (bi, 0, s))
    out = jax.ShapeDtypeStruct((b, h, n_pages * PAGE), F32)
    grid_spec = pltpu.PrefetchScalarGridSpec(
        num_scalar_prefetch=1, grid=(b, n_pages // npg), in_specs=in_specs, out_specs=(o_spec, o_spec),
        scratch_shapes=[pltpu.VMEM((h, 1), F32)])
    return pl.pallas_call(
        functools.partial(_cumsum_kernel, npg=npg, apply_logsig=apply_logsig), name="cumsum_pages",
        out_shape=(out, out), grid_spec=grid_spec,
        compiler_params=_params(("parallel", "arbitrary")),
    )(pt_flat, b_f.reshape(h, 1), carry_in, *([pool] * npg))


def _router_kernel(x_ref, w_ref, o_ref):
    x = x_ref[...]
    w = w_ref[...]
    xh = x.astype(BF16)
    xl = (x - xh.astype(F32)).astype(BF16)
    wh = w.astype(BF16)
    wl = (w - wh.astype(F32)).astype(BF16)
    logits = _dot(xh, wh) + _dot(xl, wh) + _dot(xh, wl)
    lane = lax.broadcasted_iota(I32, logits.shape, 1)
    big = logits.shape[1]
    lg = jnp.where(lane < N_EXPERTS, logits, -jnp.inf)
    m1 = jnp.max(lg, -1, keepdims=True)
    i1 = jnp.min(jnp.where(lg == m1, lane, big), -1, keepdims=True)
    lg2 = jnp.where(lane == i1, -jnp.inf, lg)
    m2 = jnp.max(lg2, -1, keepdims=True)
    i2 = jnp.min(jnp.where(lg2 == m2, lane, big), -1, keepdims=True)
    e2 = jnp.exp(m2 - m1)
    g1 = 1.0 / (1.0 + e2)
    g2 = e2 / (1.0 + e2)
    out = jnp.where(lane == 0, i1.astype(F32), jnp.where(lane == 1, i2.astype(F32),
                    jnp.where(lane == 2, g1, jnp.where(lane == 3, g2, 0.0))))
    o_ref[...] = out


def _router(x, w_router):
    n, dm = x.shape
    tm = _row_tile(n)
    wp = jnp.zeros((dm, 128), F32).at[:, :N_EXPERTS].set(w_router)
    return pl.pallas_call(
        _router_kernel, name="router",
        out_shape=jax.ShapeDtypeStruct((n, 128), F32),
        grid=(n // tm,),
        in_specs=[pl.BlockSpec((tm, dm), lambda i: (i, 0)), _const_spec((dm, 128))],
        out_specs=pl.BlockSpec((tm, 128), lambda i: (i, 0)),
        compiler_params=_params(("parallel",)),
    )(x, wp)


def _row_copy(src_hbm, src_row, dst_ref, r, sem):
    return pltpu.make_async_copy(src_hbm.at[pl.ds(src_row, 1)], dst_ref.at[pl.ds(r, 1)], sem)


def _gather_start(idx_ref, n_rows, src_hbm, dst_ref, sem):
    for r in range(n_rows):
        _row_copy(src_hbm, idx_ref[0, 0, r], dst_ref, r, sem).start()


def _gather_wait(n_rows, src_hbm, dst_ref, sem):
    for r in range(n_rows):
        _row_copy(src_hbm, 0, dst_ref, r, sem).wait()


def _expert_kernel(be_ref, nb_ref, cur_ref, nxt_ref, x_hbm, wg_ref, wu_ref, wd_ref, o_ref, xbuf, sem):
    i = pl.program_id(0)
    n_used = nb_ref[0]
    slot = lax.bitwise_and(i, 1)

    @pl.when((i == 0) & (n_used > 0))
    def _():
        _gather_start(cur_ref, MOE_BLOCK, x_hbm, xbuf.at[0], sem.at[0])

    @pl.when(i + 1 < n_used)
    def _():
        _gather_start(nxt_ref, MOE_BLOCK, x_hbm, xbuf.at[1 - slot], sem.at[1 - slot])

    @pl.when(i < n_used)
    def _():
        _gather_wait(MOE_BLOCK, x_hbm, xbuf.at[slot], sem.at[slot])
        xb = xbuf[slot].astype(BF16)
        h = jax.nn.silu(_dot(xb, wg_ref[0])) * _dot(xb, wu_ref[0])
        o_ref[...] = _dot(h.astype(BF16), wd_ref[0])

    @pl.when(i >= n_used)
    def _():
        o_ref[...] = jnp.zeros(o_ref.shape, o_ref.dtype)


def _experts(x, src_rows, block_e, n_used, wg, wu, wd):
    nb = src_rows.shape[0]
    src_rows = src_rows.reshape(nb, 1, MOE_BLOCK)
    dm = x.shape[1]
    dff = wg.shape[2]
    grid_spec = pltpu.PrefetchScalarGridSpec(
        num_scalar_prefetch=2, grid=(nb,),
        in_specs=[pl.BlockSpec((1, 1, MOE_BLOCK), lambda i, be, nu: (i, 0, 0), memory_space=pltpu.SMEM),
                  pl.BlockSpec((1, 1, MOE_BLOCK), lambda i, be, nu: (jnp.minimum(i + 1, nb - 1), 0, 0),
                               memory_space=pltpu.SMEM),
                  pl.BlockSpec(memory_space=pl.ANY),
                  pl.BlockSpec((1, dm, dff), lambda i, be, nu: (be[i], 0, 0)),
                  pl.BlockSpec((1, dm, dff), lambda i, be, nu: (be[i], 0, 0)),
                  pl.BlockSpec((1, dff, dm), lambda i, be, nu: (be[i], 0, 0))],
        out_specs=pl.BlockSpec((MOE_BLOCK, dm), lambda i, be, nu: (i, 0)),
        scratch_shapes=[pltpu.VMEM((2, MOE_BLOCK, dm), F32), pltpu.SemaphoreType.DMA((2,))])
    return pl.pallas_call(
        _expert_kernel, name="experts",
        out_shape=jax.ShapeDtypeStruct((nb * MOE_BLOCK, dm), F32), grid_spec=grid_spec,
        compiler_params=_params(("arbitrary",)),
    )(block_e, n_used, src_rows, src_rows, x, wg, wu, wd)


def _combine_ln_kernel(cur_ref, nxt_ref, y_hbm, x_ref, route_ref, g_ref, b_ref, o_ref, ybuf, sem):
    i = pl.program_id(0)
    tm = x_ref.shape[0]
    slot = lax.bitwise_and(i, 1)

    @pl.when(i == 0)
    def _():
        _gather_start(cur_ref, TOP_K * tm, y_hbm, ybuf.at[0], sem.at[0])

    @pl.when(i + 1 < pl.num_programs(0))
    def _():
        _gather_start(nxt_ref, TOP_K * tm, y_hbm, ybuf.at[1 - slot], sem.at[1 - slot])

    _gather_wait(TOP_K * tm, y_hbm, ybuf.at[slot], sem.at[slot])
    gate = route_ref[...]
    f = gate[:, 2:3] * ybuf[slot, 0:tm] + gate[:, 3:4] * ybuf[slot, tm:2 * tm]
    o_ref[...] = _layer_norm(ALPHA * x_ref[...] + f, g_ref[...], b_ref[...])


def _combine_ln(x, route, y_sorted, pos, g, b):
    n, dm = x.shape
    tm = pos.shape[1] // TOP_K
    nt = n // tm
    pos = pos.reshape(nt, 1, TOP_K * tm)
    return pl.pallas_call(
        _combine_ln_kernel, name="combine_ln",
        out_shape=jax.ShapeDtypeStruct((n, dm), F32),
        grid=(nt,),
        in_specs=[pl.BlockSpec((1, 1, TOP_K * tm), lambda i: (i, 0, 0), memory_space=pltpu.SMEM),
                  pl.BlockSpec((1, 1, TOP_K * tm), lambda i: (jnp.minimum(i + 1, nt - 1), 0, 0),
                               memory_space=pltpu.SMEM),
                  pl.BlockSpec(memory_space=pl.ANY),
                  pl.BlockSpec((tm, dm), lambda i: (i, 0)), pl.BlockSpec((tm, 128), lambda i: (i, 0)),
                  _const_spec((1, dm)), _const_spec((1, dm))],
        out_specs=pl.BlockSpec((tm, dm), lambda i: (i, 0)),
        scratch_shapes=[pltpu.VMEM((2, TOP_K * tm, dm), F32), pltpu.SemaphoreType.DMA((2,))],
        compiler_params=_params(("arbitrary",)),
    )(pos, pos, y_sorted, x, route, g.reshape(1, dm), b.reshape(1, dm))


def _moe_ln(x, w_router, wg, wu, wd, g, b):
    n, dm = x.shape
    route = _router(x, w_router)
    top_e = route[:, 0:2].astype(I32)
    n_assign = n * TOP_K
    flat_e = top_e.reshape(-1)
    order = jnp.argsort(flat_e)
    e_sorted = flat_e[order]
    counts = jnp.bincount(flat_e, length=N_EXPERTS)
    padded = (counts + MOE_BLOCK - 1) // MOE_BLOCK * MOE_BLOCK
    start = jnp.cumsum(counts) - counts
    pad_end = jnp.cumsum(padded)
    pad_start = pad_end - padded
    dest = (pad_start[e_sorted] + jnp.arange(n_assign) - start[e_sorted]).astype(I32)
    n_blocks = -(-n_assign // MOE_BLOCK) + N_EXPERTS
    block_e = jnp.minimum(jnp.searchsorted(pad_end, jnp.arange(n_blocks) * MOE_BLOCK, side="right"),
                          N_EXPERTS - 1).astype(I32)
    n_used = (pad_end[-1] // MOE_BLOCK).astype(I32).reshape(1)
    src_rows = jnp.zeros((n_blocks * MOE_BLOCK,), I32).at[dest].set((order // TOP_K).astype(I32))
    y_sorted = _experts(x, src_rows.reshape(n_blocks, MOE_BLOCK), block_e, n_used,
                        wg.astype(BF16), wu.astype(BF16), wd.astype(BF16))
    tm = min(_row_tile(n), MOE_BLOCK)
    pos_of = jnp.zeros((n_assign,), I32).at[order].set(dest).reshape(n // tm, tm, TOP_K)
    pos = pos_of.transpose(0, 2, 1).reshape(n // tm, TOP_K * tm)
    return _combine_ln(x, route, y_sorted, pos, g, b)


C0_QDA, C0_DIFF, C0_QNS, C0_CMP, C0_SEL, C0_WIN, C0_GATE, C0 = 0, 512, 1536, 2048, 2304, 2560, 2816, 2944
C1_Q, C1_KV, C1_F, C1 = 0, 1024, 1536, 1664
FOX_DK = 128


def _l0_weight(w_in):
    k_off, v_off = 512, 1024
    cols = list(range(0, 512))
    for h in range(DA_HEADS):
        cols += list(range(k_off + 128 * h, k_off + 128 * (h + 1)))
        cols += list(range(v_off + 128 * h, v_off + 128 * (h + 1)))
    cols += list(range(1536, 2840))
    w = w_in[:, np.asarray(cols)]
    return jnp.pad(w, ((0, 0), (0, C0 - w.shape[1]))).astype(BF16)


def _pad_rows(x, rows):
    return jnp.pad(x, ((0, 0), (0, rows - x.shape[1])) + ((0, 0),) * (x.ndim - 2))


def _fox_augment(q, k, c):
    b, t = c.shape[:2]

    def split3(x):
        def top(v):
            return lax.bitcast_convert_type(lax.bitcast_convert_type(v, jnp.uint32) & jnp.uint32(0xFFFF0000), F32)
        hi = top(x)
        mid = top(x - hi)
        lo = top(x - hi - mid)
        return jnp.stack([hi.astype(BF16), mid.astype(BF16), lo.astype(BF16)], -1)

    c3 = split3(c)
    nc3 = split3(-c).reshape(b, t, FOX_KV_HEADS, FOX_R * 3)
    slot = jnp.asarray(np.repeat(np.eye(FOX_R), 3, axis=1), BF16)
    slot = jnp.broadcast_to(jnp.tile(slot, (FOX_KV_HEADS, 1))[None, None], (b, t, FOX_HEADS, 3 * FOX_R))
    pad_q = jnp.zeros((b, t, FOX_HEADS, FOX_DK - HEAD_DIM - 3 - 3 * FOX_R), BF16)
    qa = jnp.concatenate([q * jnp.asarray(SCALE, BF16), c3, slot, pad_q], -1)
    ones = jnp.ones((b, t, FOX_KV_HEADS, 3), BF16)
    pad_k = jnp.zeros((b, t, FOX_KV_HEADS, FOX_DK - HEAD_DIM - 3 - 3 * FOX_R), BF16)
    ka = jnp.concatenate([k, ones, nc3, pad_k], -1)
    return qa.transpose(0, 2, 1, 3), ka.transpose(0, 2, 1, 3)


def kernel(x_prompt, x_sample, cache_diff_kv, cache_nsa_cmp, cache_nsa_sel, state_nsa_win, cache_fox_kv,
           cache_fox_logf, page_table, rel_bias, l0_w_in, l0_w_out, da_lambda, da_norm, nsa_cmp_pos,
           nsa_cmp_w1, nsa_cmp_w2, ffn_w_gate, ffn_w_up, ffn_w_down, l1_w_in, fox_b_f, l1_w_out,
           moe_router, moe_w_gate, moe_w_up, moe_w_down, ln_g, ln_b):
    b, t, dm = x_prompt.shape
    bs, ts, _ = x_sample.shape
    n_pages = page_table.shape[1]
    n_phys = cache_diff_kv.shape[0]
    past = n_pages * PAGE
    np_, ns = b * t, bs * ts
    pt_flat = page_table.reshape(-1).astype(I32)
    x0 = jnp.concatenate([x_prompt.reshape(np_, dm), x_sample.reshape(ns, dm)], 0)
    g2 = NSA_GROUPS

    p0, p0b = _proj(x0, _l0_weight(l0_w_in))
    bt_c = _bias_tiles(rel_bias, 4, False, 0, DA_HEADS + NSA_HEADS)
    bt_w = _bias_tiles(rel_bias, 6, True, DA_HEADS, NSA_HEADS)
    rel_t = rel_bias.T

    vt_da = p0b[:np_, C0_DIFF:C0_DIFF + 1024].reshape(b, t, DA_HEADS, 4 * HEAD_DIM)[..., 2 * HEAD_DIM:]
    o_da_p = _diff_prompt(p0b, vt_da.transpose(0, 2, 3, 1), rel_bias, da_lambda, da_norm, bt_c, b, t,
                          C0_QDA // 128, C0_DIFF // 128)

    q_ns_p = p0b[:np_, C0_QNS:C0_QNS + 512].reshape(b, t, NSA_HEADS, HEAD_DIM).transpose(0, 2, 1, 3)

    def kv_major(col):
        kv = p0b[:np_, col:col + 256].reshape(b, t, 2, g2, HEAD_DIM)
        return kv[:, :, 0].transpose(0, 2, 1, 3), kv[:, :, 1].transpose(0, 2, 3, 1)

    wbig = _cmp_weight(nsa_cmp_w1)
    part_p = _rowmm(p0[:np_, C0_CMP:C0_CMP + 256].reshape(np_ // CMP_STRIDE, CMP_STRIDE * 256), wbig)
    kc_p, vc_p = _cmp_finish(part_p.reshape(b, t // CMP_STRIDE, 1024), nsa_cmp_pos, nsa_cmp_w1, nsa_cmp_w2)
    oc_p, memb_p = _cmp_prompt(q_ns_p, kc_p, vc_p, rel_bias)
    ks_p, vst_p = kv_major(C0_SEL)
    os_p = _gqa_prompt("sel", q_ns_p, ks_p, vst_p, rel_bias=rel_bias, bt=bt_c, memb=memb_p)
    kw_p, vwt_p = kv_major(C0_WIN)
    gate_p = p0[:np_, C0_GATE:C0_GATE + 3 * NSA_HEADS].reshape(b, t, g2, 3 * NSA_R).transpose(0, 2, 3, 1)
    o_ns_p = _gqa_prompt("win", q_ns_p, kw_p, vwt_p, bt=bt_w, oc=oc_p.transpose(0, 1, 3, 2), os=os_p, gate=gate_p)

    qpos0 = past
    qd = p0b[np_:, C0_QDA:C0_QDA + 512].reshape(bs, ts, DA_HEADS, 2, HEAD_DIM).transpose(0, 3, 2, 1, 4)
    qm_d = jnp.einsum("bwhqd,ws->bwhqsd", qd, jnp.eye(2, dtype=BF16)) * jnp.asarray(SCALE, BF16)
    qm_d = qm_d.reshape(bs, 2 * DA_HEADS * ts, 2 * HEAD_DIM)
    rows_d = [(w, h, q) for w in range(2) for h in range(DA_HEADS) for q in range(ts)]
    rowq_d = jnp.asarray(np.array([[r[2], r[1]] for r in rows_d], np.int32))
    rowtbl_d = rel_t[np.array([r[1] for r in rows_d])]

    def key_rows(x):
        lead = x.shape[:-2]
        x = x.reshape(lead + (PAGE, DA_HEADS, 2, 2 * HEAD_DIM))
        return jnp.swapaxes(x, -3, -2).reshape(lead + (PAGE * 2 * DA_HEADS, 2 * HEAD_DIM))

    tail_d = key_rows(_pad_rows(p0[np_:, C0_DIFF:C0_DIFF + 1024].reshape(bs, ts, 1024), PAGE))
    acc_d = _decode("diffx", qm_d, rowq_d, key_rows(cache_diff_kv.reshape(n_phys, PAGE, 1024)), pt_flat, tail_d,
                    n_pages, 8, 0, qpos0, False, rowtbl=rowtbl_d)
    o12 = acc_d.reshape(bs, 2, DA_HEADS * ts, 2 * HEAD_DIM)
    o_da_s = _diff_finish(o12[:, 0].reshape(bs * DA_HEADS * ts, 128), o12[:, 1].reshape(bs * DA_HEADS * ts, 128),
                          da_lambda, da_norm)
    o_da_s = o_da_s.reshape(bs, DA_HEADS, ts, 128).transpose(0, 2, 1, 3).reshape(ns, 512)

    part_s = _paged_cmp(cache_nsa_cmp.transpose(0, 2, 3, 4, 1).reshape(n_phys, 4 * HEAD_DIM, PAGE), pt_flat, wbig)
    l_tot = past + ts
    if l_tot // CMP_STRIDE > past // CMP_STRIDE:
        raise NotImplementedError("new rows completing a compression chunk")
    kc_s, vc_s = _cmp_finish(part_s.reshape(bs, past // CMP_STRIDE, 1024), nsa_cmp_pos, nsa_cmp_w1, nsa_cmp_w2)
    nbs_s = -(-l_tot // SEL_BLOCK)
    mb_s = -(-nbs_s // 128) * 128
    qn = p0b[np_:, C0_QNS:C0_QNS + 512].reshape(bs, ts, g2, NSA_R, HEAD_DIM) * jnp.asarray(SCALE, BF16)
    q_cmp = _pad_rows(qn.reshape(bs, ts, NSA_HEADS, HEAD_DIM), 8).transpose(0, 2, 1, 3)
    q_cmp = q_cmp.reshape(bs, NSA_HEADS * 8, HEAD_DIM)
    rowtbl_c = rel_t[DA_HEADS + np.repeat(np.arange(NSA_HEADS), 8)]
    oc_s, memb_s = _cmp_sample(q_cmp, rowtbl_c, kc_s, vc_s, qpos0, nbs_s, mb_s)
    oc_s = oc_s.reshape(bs, NSA_HEADS, 8, HEAD_DIM)[:, :, :ts].transpose(0, 2, 1, 3)
    eye_g = jnp.eye(g2, dtype=BF16)
    qm_n = jnp.einsum("bqgrd,gs->bqgrsd", qn, eye_g).reshape(bs, ts * NSA_HEADS, g2 * HEAD_DIM)
    rows_n = [(q, g, r) for q in range(ts) for g in range(g2) for r in range(NSA_R)]
    rowq_n = jnp.asarray(np.array([[r[0], r[1] * NSA_R + r[2]] for r in rows_n], np.int32))
    rowtbl_n = rel_t[DA_HEADS + np.array([r[1] * NSA_R + r[2] for r in rows_n])]
    memb_rows = memb_s.reshape(bs, g2, 8, mb_s)[:, :, :ts, :nbs_s].transpose(0, 2, 1, 3)
    memb_rows = jnp.broadcast_to(memb_rows[:, :, :, None, :, None], (bs, ts, g2, NSA_R, nbs_s, SEL_BLOCK))
    memb_rows = memb_rows.reshape(bs, ts * NSA_HEADS, nbs_s * SEL_BLOCK)
    memb_rows = jnp.where(memb_rows > 0.5, rowtbl_n[None, :, N_BUCKETS - 1:N_BUCKETS], NEG_INF)
    memb_rows = jnp.pad(memb_rows, ((0, 0), (0, 0), (0, past + PAGE - nbs_s * SEL_BLOCK)), constant_values=NEG_INF)

    def feat_tail(col, width):
        return _pad_rows(p0[np_:, col:col + width].reshape(bs, ts, width), PAGE).transpose(0, 2, 1)

    pool_sel = cache_nsa_sel.transpose(0, 2, 3, 4, 1).reshape(n_phys, 4 * HEAD_DIM, PAGE)
    acc_sel = _decode("sel", qm_n, rowq_n, pool_sel, pt_flat, feat_tail(C0_SEL, 256),
                      n_pages, 8, 0, qpos0, True, rowtbl=rowtbl_n, aux=memb_rows)
    w_buf = state_nsa_win.shape[1]
    pool_win = state_nsa_win.transpose(0, 2, 3, 4, 1).reshape(bs, 4 * HEAD_DIM, w_buf)
    acc_win = _decode("win", qm_n, rowq_n, pool_win, jnp.arange(bs, dtype=I32), feat_tail(C0_WIN, 256),
                      1, 1, qpos0 - w_buf, qpos0, True, rowtbl=rowtbl_n)

    def pick_group(acc):
        a = acc.reshape(bs, ts, g2, NSA_R, g2, HEAD_DIM)
        return jnp.stack([a[:, :, g, :, g] for g in range(g2)], 2)

    gate_s = p0[np_:, C0_GATE:C0_GATE + 3 * NSA_HEADS].reshape(ns * NSA_HEADS, 3)
    o_ns_s = _nsa_combine(oc_s.reshape(ns * NSA_HEADS, HEAD_DIM), pick_group(acc_sel).reshape(ns * NSA_HEADS, HEAD_DIM),
                          pick_group(acc_win).reshape(ns * NSA_HEADS, HEAD_DIM), gate_s).reshape(ns, 512)

    o_da = jnp.concatenate([o_da_p, o_da_s], 0)
    o_ns = jnp.concatenate([o_ns_p, o_ns_s], 0)
    x1 = _mix_ln([o_da, o_ns], l0_w_out.astype(BF16), x0, ln_g[0], ln_b[0])
    x2 = _ffn_ln(x1, ffn_w_gate.astype(BF16), ffn_w_up.astype(BF16), ffn_w_down.astype(BF16), ln_g[1], ln_b[1])

    w1p = jnp.pad(l1_w_in, ((0, 0), (0, C1 - l1_w_in.shape[1]))).astype(BF16)
    p1, p1b = _proj(x2, w1p)
    fl_p = p1[:np_, C1_F:C1_F + FOX_HEADS].reshape(np_ // PAGE, PAGE, FOX_HEADS).transpose(0, 2, 1)
    lft_p, ct_p = _cumsum_pages(fl_p, jnp.arange(np_ // PAGE, dtype=I32), b, t // PAGE, fox_b_f,
                                jnp.zeros((b, FOX_HEADS, 1), F32), True)
    q_fx = p1b[:np_, C1_Q:C1_Q + 1024].reshape(b, t, FOX_HEADS, HEAD_DIM)
    kv_fx = p1b[:np_, C1_KV:C1_KV + 512].reshape(b, t, 2, FOX_KV_HEADS, HEAD_DIM)
    qa_p, ka_p = _fox_augment(q_fx, kv_fx[:, :, 0], ct_p.transpose(0, 2, 1))
    o_fx_p = _gqa_prompt("fox", qa_p, ka_p, kv_fx[:, :, 1].transpose(0, 2, 3, 1))

    _, ct_past = _cumsum_pages(cache_fox_logf.transpose(0, 2, 1), pt_flat, bs, n_pages, fox_b_f,
                               jnp.zeros((bs, FOX_HEADS, 1), F32), False)
    f_new = _pad_rows(p1[np_:, C1_F:C1_F + FOX_HEADS].reshape(bs, ts, FOX_HEADS), PAGE).transpose(0, 2, 1)
    lft_s, ct_new = _cumsum_pages(f_new, jnp.arange(bs, dtype=I32), bs, 1, fox_b_f, ct_past[:, :, past - 1:past], True)
    ck_s = jnp.concatenate([ct_past, ct_new], 2)
    cq_s = ct_new[:, :, :ts].transpose(0, 2, 1).reshape(bs, ts * FOX_HEADS, 1)
    qf = p1b[np_:, C1_Q:C1_Q + 1024].reshape(bs, ts, FOX_KV_HEADS, FOX_R, HEAD_DIM) * jnp.asarray(SCALE, BF16)
    qm_f = jnp.einsum("bqgrd,gs->bqgrsd", qf, jnp.eye(FOX_KV_HEADS, dtype=BF16))
    qm_f = qm_f.reshape(bs, ts * FOX_HEADS, FOX_KV_HEADS * HEAD_DIM)
    rowq_f = jnp.asarray(np.stack([np.repeat(np.arange(ts), FOX_HEADS), np.tile(np.arange(FOX_HEADS), ts)],
                                  1).astype(np.int32))
    tail_f = _pad_rows(p1[np_:, C1_KV:C1_KV + 512].reshape(bs, ts, 512), PAGE).transpose(0, 2, 1)
    pool_fox = cache_fox_kv.transpose(0, 2, 3, 4, 1).reshape(n_phys, 2 * FOX_KV_HEADS * HEAD_DIM, PAGE)
    acc_f = _decode("fox", qm_f, rowq_f, pool_fox, pt_flat, tail_f, n_pages, 8, 0, qpos0, True, cq=cq_s, aux=ck_s)
    acc_f = acc_f.reshape(bs, ts, FOX_KV_HEADS, FOX_R, FOX_KV_HEADS, HEAD_DIM)
    o_fx_s = jnp.stack([acc_f[:, :, g, :, g] for g in range(FOX_KV_HEADS)], 2)
    o_fx = jnp.concatenate([o_fx_p, o_fx_s.reshape(ns, 1024).astype(BF16)], 0)
    x3 = _mix_ln([o_fx], l1_w_out.astype(BF16), x2, ln_g[2], ln_b[2])

    y = _moe_ln(x3, moe_router, moe_w_gate, moe_w_up, moe_w_down, ln_g[3], ln_b[3])

    diff_all = p0[:, C0_DIFF:C0_DIFF + 1024]
    cmp_all = p0[:, C0_CMP:C0_CMP + 256]
    sel_all = p0[:, C0_SEL:C0_SEL + 256]
    win_all = p0[:, C0_WIN:C0_WIN + 256]
    fkv_all = p1[:, C1_KV:C1_KV + 512]
    keep_p = min(WINDOW, t)
    win_p_out = win_all[:np_].reshape(b, t, 2, g2, HEAD_DIM)[:, t - keep_p:]
    full_win_s = jnp.concatenate([state_nsa_win, win_all[np_:].reshape(bs, ts, 2, g2, HEAD_DIM)], 1)
    keep_s = min(WINDOW, full_win_s.shape[1])
    return (y[:np_].reshape(b, t, dm), y[np_:].reshape(bs, ts, dm),
            diff_all[:np_].reshape(b, t, DA_HEADS, 4 * HEAD_DIM), diff_all[np_:].reshape(bs, ts, DA_HEADS, 4 * HEAD_DIM),
            cmp_all[:np_].reshape(b, t, 2, g2, HEAD_DIM), cmp_all[np_:].reshape(bs, ts, 2, g2, HEAD_DIM),
            sel_all[:np_].reshape(b, t, 2, g2, HEAD_DIM), sel_all[np_:].reshape(bs, ts, 2, g2, HEAD_DIM),
            win_p_out, full_win_s[:, full_win_s.shape[1] - keep_s:],
            fkv_all[:np_].reshape(b, t, 2, FOX_KV_HEADS, HEAD_DIM), fkv_all[np_:].reshape(bs, ts, 2, FOX_KV_HEADS, HEAD_DIM),
            lft_p.transpose(0, 2, 1), lft_s[:, :, :ts].transpose(0, 2, 1))
```

```python
import functools
import math

import numpy as np
import jax
import jax.numpy as jnp
from jax import lax
from jax.experimental import pallas as pl
from jax.experimental.pallas import tpu as pltpu

F32, BF16, I32 = jnp.float32, jnp.bfloat16, jnp.int32

HEAD_DIM = 64
DA_HEADS = 4
NSA_HEADS = 8
NSA_GROUPS = 2
NSA_R = NSA_HEADS // NSA_GROUPS
CMP_BLOCK = 32
CMP_STRIDE = 16
CMP_HIDDEN = 2 * HEAD_DIM
SEL_BLOCK = 64
SEL_TOPK = 16
WINDOW = 512
FOX_HEADS = 16
FOX_KV_HEADS = 4
FOX_R = FOX_HEADS // FOX_KV_HEADS
N_BUCKETS = 32
MAX_DISTANCE = 128
N_EXPERTS = 8
TOP_K = 2
MOE_BLOCK = 128
LN_EPS = 1e-5
RMS_EPS = 1e-5
DEPTH = 2
ALPHA = (2.0 * DEPTH) ** 0.25
DA_LAMBDA_INIT = 0.8 - 0.6 * math.exp(-0.3 * 0)
NEG_INF = -1e30
FORCED_BONUS = 1e4
SCALE = HEAD_DIM ** -0.5

PAGE = 128
BQ = 128
BK = 256
DECODE_PAGES = 16
FAR_GROUP = 8
VMEM_LIMIT = 56 * 1024 * 1024


def _t5_thresholds():
    d = np.arange(0, 4 * MAX_DISTANCE)
    df = np.maximum(d, 1).astype(np.float32)
    max_exact = N_BUCKETS // 2
    large = max_exact + (np.log(df / np.float32(max_exact)) / np.float32(math.log(MAX_DISTANCE / max_exact))
                         * np.float32(N_BUCKETS - max_exact)).astype(np.int32)
    bucket = np.where(d < max_exact, d, np.minimum(large, N_BUCKETS - 1))
    return [int(np.argmax(bucket >= b)) for b in range(1, N_BUCKETS)]


T5_THR = _t5_thresholds()
T5_FAR = T5_THR[-1]


def _t5_bias(d, val):
    out = jnp.zeros(d.shape, F32) + val(0)
    for b in range(1, N_BUCKETS):
        out = jnp.where(d >= T5_THR[b - 1], val(b), out)
    return out


def _dot(a, b):
    return jnp.dot(a, b, preferred_element_type=F32)


def _dot_nt(a, b):
    return lax.dot_general(a, b, (((1,), (1,)), ((), ())), preferred_element_type=F32)


def _split3(x):
    hi = x.astype(BF16)
    r1 = x - hi.astype(F32)
    mid = r1.astype(BF16)
    lo = (r1 - mid.astype(F32)).astype(BF16)
    return hi, mid, lo


def _layer_norm(z, g, b):
    mu = jnp.mean(z, -1, keepdims=True)
    zc = z - mu
    var = jnp.mean(zc * zc, -1, keepdims=True)
    return zc * lax.rsqrt(var + LN_EPS) * g + b


def _online_update(s, accum, m_ref, l_ref, acc_ref):
    m_prev = m_ref[...]
    m_new = jnp.maximum(m_prev, jnp.max(s, -1, keepdims=True))
    alpha = jnp.exp(m_prev - m_new)
    p = jnp.exp(s - m_new)
    l_ref[...] = alpha * l_ref[...] + jnp.sum(p, -1, keepdims=True)
    acc_ref[...] = alpha * acc_ref[...] + accum(p)
    m_ref[...] = m_new


def _online_update_t(ss, vts, m_ref, l_ref, acc_ref):
    m_prev = m_ref[...]
    m_new = m_prev
    for s in ss:
        m_new = jnp.maximum(m_new, jnp.max(s, 0, keepdims=True))
    alpha = jnp.exp(m_prev - m_new)
    l_new = alpha * l_ref[...]
    acc_new = alpha * acc_ref[...]
    for s, vt in zip(ss, vts):
        p = jnp.exp(s - m_new)
        l_new = l_new + jnp.sum(p, 0, keepdims=True)
        acc_new = acc_new + _dot(vt, p.astype(BF16))
    l_ref[...] = l_new
    acc_ref[...] = acc_new
    m_ref[...] = m_new


def _init_state(m_ref, l_ref, acc_ref):
    m_ref[...] = jnp.full(m_ref.shape, NEG_INF, F32)
    l_ref[...] = jnp.zeros(l_ref.shape, F32)
    acc_ref[...] = jnp.zeros(acc_ref.shape, F32)


def _row_tile(n):
    for t in (512, 384, 256, 128, 64, 32, 16, 8):
        if n % t == 0:
            return t
    raise ValueError(f"row count {n} is not a multiple of 8")


def _params(sem):
    return pltpu.CompilerParams(dimension_semantics=sem, vmem_limit_bytes=VMEM_LIMIT)


def _const_spec(shape):
    nd = len(shape)
    return pl.BlockSpec(shape, lambda *a: (0,) * nd, pipeline_mode=pl.Buffered(1))


def _proj_kernel(x_ref, w_ref, o_ref, ob_ref):
    y = _dot(x_ref[...].astype(BF16), w_ref[...])
    o_ref[...] = y
    ob_ref[...] = y.astype(BF16)


def _proj(x, w):
    n, k = x.shape
    c = w.shape[1]
    tm = _row_tile(n)
    return pl.pallas_call(
        _proj_kernel, name="proj",
        out_shape=(jax.ShapeDtypeStruct((n, c), F32), jax.ShapeDtypeStruct((n, c), BF16)),
        grid=(n // tm,),
        in_specs=[pl.BlockSpec((tm, k), lambda i: (i, 0)), _const_spec((k, c))],
        out_specs=(pl.BlockSpec((tm, c), lambda i: (i, 0)), pl.BlockSpec((tm, c), lambda i: (i, 0))),
        compiler_params=_params(("parallel",)),
    )(x, w)


def _rowmm_kernel(x_ref, w_ref, o_ref):
    o_ref[...] = _dot(x_ref[...].astype(BF16), w_ref[...])


def _rowmm(x, w):
    n, k = x.shape
    c = w.shape[1]
    tm = _row_tile(n)
    tm = min(tm, 256)
    return pl.pallas_call(
        _rowmm_kernel, name="rowmm",
        out_shape=jax.ShapeDtypeStruct((n, c), F32),
        grid=(n // tm,),
        in_specs=[pl.BlockSpec((tm, k), lambda i: (i, 0)), _const_spec((k, c))],
        out_specs=pl.BlockSpec((tm, c), lambda i: (i, 0)),
        compiler_params=_params(("parallel",)),
    )(x, w)


def _mix_ln_kernel(*refs, n_in):
    a_refs = refs[:n_in]
    w_ref, x_ref, g_ref, b_ref, y_ref = refs[n_in:]
    acc = None
    off = 0
    for a in a_refs:
        k = a.shape[1]
        t = _dot(a[...], w_ref[off:off + k, :])
        acc = t if acc is None else acc + t
        off += k
    y_ref[...] = _layer_norm(ALPHA * x_ref[...] + acc, g_ref[...], b_ref[...])


def _mix_ln(a_list, w, x, g, b):
    n, dm = x.shape
    tm = _row_tile(n)
    in_specs = [pl.BlockSpec((tm, a.shape[1]), lambda i: (i, 0)) for a in a_list]
    in_specs += [_const_spec(w.shape), pl.BlockSpec((tm, dm), lambda i: (i, 0)),
                 _const_spec((1, dm)), _const_spec((1, dm))]
    return pl.pallas_call(
        functools.partial(_mix_ln_kernel, n_in=len(a_list)), name="mix_ln",
        out_shape=jax.ShapeDtypeStruct((n, dm), F32),
        grid=(n // tm,), in_specs=in_specs,
        out_specs=pl.BlockSpec((tm, dm), lambda i: (i, 0)),
        compiler_params=_params(("parallel",)),
    )(*a_list, w, x, g.reshape(1, dm), b.reshape(1, dm))


def _ffn_ln_kernel(x_ref, wg_ref, wu_ref, wd_ref, g_ref, b_ref, y_ref):
    x = x_ref[...]
    xb = x.astype(BF16)
    h = jax.nn.silu(_dot(xb, wg_ref[...])) * _dot(xb, wu_ref[...])
    f = _dot(h.astype(BF16), wd_ref[...])
    y_ref[...] = _layer_norm(ALPHA * x + f, g_ref[...], b_ref[...])


def _ffn_ln(x, wg, wu, wd, g, b):
    n, dm = x.shape
    tm = min(_row_tile(n), 384)
    return pl.pallas_call(
        _ffn_ln_kernel, name="ffn_ln",
        out_shape=jax.ShapeDtypeStruct((n, dm), F32),
        grid=(n // tm,),
        in_specs=[pl.BlockSpec((tm, dm), lambda i: (i, 0)), _const_spec(wg.shape), _const_spec(wu.shape),
                  _const_spec(wd.shape), _const_spec((1, dm)), _const_spec((1, dm))],
        out_specs=pl.BlockSpec((tm, dm), lambda i: (i, 0)),
        compiler_params=_params(("parallel",)),
    )(x, wg, wu, wd, g.reshape(1, dm), b.reshape(1, dm))


def _bias_tiles_kernel(rel_ref, o_ref, *, nt, window, head0):
    h = pl.program_id(0) + head0
    kk = lax.broadcasted_iota(I32, (BK, BQ), 0)
    qq = lax.broadcasted_iota(I32, (BK, BQ), 1)
    for t in range(nt):
        d = t * BQ + qq - kk
        bias = _t5_bias(d, lambda b: rel_ref[b, h])
        bad = d < 0
        if window:
            bad = bad | (d >= WINDOW)
        o_ref[0, t] = jnp.where(bad, NEG_INF, bias)


def _bias_tiles(rel_bias, nt, window, head0, nheads):
    return pl.pallas_call(
        functools.partial(_bias_tiles_kernel, nt=nt, window=window, head0=head0), name="bias_tiles",
        out_shape=jax.ShapeDtypeStruct((nheads, nt, BK, BQ), F32),
        grid=(nheads,),
        in_specs=[pl.BlockSpec(memory_space=pltpu.SMEM)],
        out_specs=pl.BlockSpec((1, nt, BK, BQ), lambda h: (h, 0, 0, 0)),
        compiler_params=_params(("parallel",)),
    )(rel_bias)


def _da_lambda(lp):
    return (jnp.exp(jnp.sum(lp[0:1] * lp[1:2], -1, keepdims=True))
            - jnp.exp(jnp.sum(lp[2:3] * lp[3:4], -1, keepdims=True)) + DA_LAMBDA_INIT)


def _to_rows(xt):
    eye = (lax.broadcasted_iota(I32, (BQ, BQ), 0) == lax.broadcasted_iota(I32, (BQ, BQ), 1)).astype(BF16)
    return _dot_nt(eye, xt)


def _flash_kernel(*refs, mode):
    if mode == "diff":
        rel_ref, lam_ref, nw_ref, bt_ref, q_ref, k_ref, vt_ref, o_ref, m_ref, l_ref, acc_ref = refs
    elif mode == "fox":
        q_ref, k_ref, vt_ref, o_ref, m_ref, l_ref, acc_ref = refs
    elif mode == "sel":
        rel_ref, bt_ref, q_ref, k_ref, vt_ref, memb_ref, o_ref, m_ref, l_ref, acc_ref = refs
    else:
        bt_ref, q_ref, k_ref, vt_ref, oc_ref, os_ref, gate_ref, o_ref, m_ref, l_ref, acc_ref = refs
    g = pl.program_id(1)
    i = pl.program_id(2)
    if mode == "diff":
        jd = i
        part_t0 = (0, 1, 0, 1)
    else:
        jd = lax.shift_right_logical(i, 1)
        t0 = lax.bitwise_and(i, 1)
    if mode == "diff":
        nr = 4
        q = q_ref[...]
        lane = lax.broadcasted_iota(I32, q.shape, 1)
        zero = jnp.zeros_like(q)
        qs = jnp.concatenate([jnp.where(lane < HEAD_DIM, q, zero), jnp.where(lane >= HEAD_DIM, q, zero)], 0)
        qs = qs * jnp.asarray(SCALE, BF16)
    else:
        nr = q_ref.shape[1]
        qs = q_ref[0].reshape(nr * BQ, q_ref.shape[3])
        if mode != "fox":
            qs = qs * jnp.asarray(SCALE, BF16)
    _init_state(m_ref, l_ref, acc_ref)

    def scores(j, back, diag):
        off = pl.multiple_of(j * BK, BK)
        if mode == "diff":
            kt = k_ref[pl.ds(off, BK), :]
        else:
            kt = k_ref[0, 0, pl.ds(off, BK), :]
        s = _dot_nt(kt, qs)
        if mode == "fox":
            if diag:
                kk = lax.broadcasted_iota(I32, (BK, BQ), 0)
                qq = lax.broadcasted_iota(I32, (BK, BQ), 1)
                ok = (t0 * BQ + qq - kk) >= 0
                s = jnp.where(jnp.concatenate([ok] * nr, 1), s, NEG_INF)
        else:
            extra = None
            if mode == "sel":
                nbs = memb_ref.shape[2]
                kk = lax.broadcasted_iota(I32, (BK, nbs), 0)
                jb = lax.broadcasted_iota(I32, (BK, nbs), 1)
                e = (jb == j * (BK // SEL_BLOCK) + lax.shift_right_logical(kk, 6)).astype(BF16)
                extra = (_dot(e, memb_ref[0, 0]) - 1.0) * (-NEG_INF)
            parts = []
            for r in range(nr):
                if back is None:
                    br = rel_ref[N_BUCKETS - 1, g if mode == "diff" else DA_HEADS + g * nr + r]
                elif mode == "diff":
                    br = bt_ref[0, part_t0[r] + 2 * back]
                else:
                    br = bt_ref[r, t0 + 2 * back]
                if extra is not None:
                    br = br + extra
                parts.append(s[:, r * BQ:(r + 1) * BQ] + br)
            s = jnp.concatenate(parts, 1)
        return s, vt_ref[0, 0, :, pl.ds(off, BK)]

    def tiles(spec):
        ss, vts = zip(*[scores(*a) for a in spec])
        _online_update_t(ss, vts, m_ref, l_ref, acc_ref)

    if mode != "win":
        n_far = jd if mode == "fox" else jnp.maximum(jd - 1, 0)

        def far_body(jg, c):
            tiles([(FAR_GROUP * jg + u, None, False) for u in range(FAR_GROUP)])
            return c

        lax.fori_loop(0, lax.shift_right_logical(n_far, FAR_GROUP.bit_length() - 1), far_body, 0)
        base = lax.bitwise_and(n_far, -FAR_GROUP)
        width = FAR_GROUP // 2
        while width >= 1:
            @pl.when(lax.bitwise_and(n_far, width) != 0)
            def _(base=base, width=width):
                tiles([(base + u, None, False) for u in range(width)])

            base = base + lax.bitwise_and(n_far, width)
            width //= 2
    n_near = 1 if mode == "fox" else (3 if mode == "win" else 2)
    for cnt in range(1, n_near + 1):
        cond = (jd >= cnt - 1) if cnt == n_near else (jd == cnt - 1)

        @pl.when(cond)
        def _(cnt=cnt):
            tiles([(jd - a, a, a == 0) for a in reversed(range(cnt))])

    ot = acc_ref[...] / l_ref[...]
    if mode == "diff":
        a = ot[:, :2 * BQ] - _da_lambda(lam_ref[...]) * ot[:, 2 * BQ:]
        r = a * lax.rsqrt(jnp.mean(a * a, 0, keepdims=True) + RMS_EPS) * nw_ref[...] * (1.0 - DA_LAMBDA_INIT)
        r = r.astype(BF16)
        o_ref[...] = jnp.concatenate([_to_rows(r[:, :BQ]), _to_rows(r[:, BQ:])], 0).astype(o_ref.dtype)
    elif mode == "fox":
        o_ref[...] = jnp.concatenate([_to_rows(ot[:, r * BQ:(r + 1) * BQ].astype(BF16)) for r in range(nr)],
                                     1).astype(o_ref.dtype)
    elif mode == "sel":
        for r in range(nr):
            o_ref[0, r] = ot[:, r * BQ:(r + 1) * BQ]
    else:
        gt = jax.nn.sigmoid(gate_ref[0, 0])
        parts = []
        for r in range(nr):
            comb = (gt[3 * r:3 * r + 1] * oc_ref[0, r] + gt[3 * r + 1:3 * r + 2] * os_ref[0, r]
                    + gt[3 * r + 2:3 * r + 3] * ot[:, r * BQ:(r + 1) * BQ])
            parts.append(_to_rows(comb.astype(BF16)))
        o_ref[...] = jnp.concatenate(parts, 1).astype(o_ref.dtype)


def _flash_scratch(nr, dv):
    return [pltpu.VMEM((1, nr * BQ), F32), pltpu.VMEM((1, nr * BQ), F32), pltpu.VMEM((dv, nr * BQ), F32)]


def _diff_prompt(p0b, vt, rel_bias, da_lambda, da_norm, bt, b, t, col_q, col_k):
    nq = t // BK
    dv = 2 * HEAD_DIM
    return pl.pallas_call(
        functools.partial(_flash_kernel, mode="diff"), name="diff_prompt",
        out_shape=jax.ShapeDtypeStruct((b * t, DA_HEADS * dv), BF16),
        grid=(b, DA_HEADS, nq),
        in_specs=[pl.BlockSpec(memory_space=pltpu.SMEM),
                  _const_spec((4, HEAD_DIM)), _const_spec((dv, 1)),
                  pl.BlockSpec((1, 4, BK, BQ), lambda bi, h, i: (h, 0, 0, 0)),
                  pl.BlockSpec((BK, 128), lambda bi, h, i: (bi * nq + i, col_q + h)),
                  pl.BlockSpec((t, 128), lambda bi, h, i: (bi, col_k + 2 * h)),
                  pl.BlockSpec((1, 1, dv, t), lambda bi, h, i: (bi, h, 0, 0))],
        out_specs=pl.BlockSpec((BK, dv), lambda bi, h, i: (bi * nq + i, h)),
        scratch_shapes=_flash_scratch(4, dv),
        compiler_params=_params(("parallel", "parallel", "arbitrary")),
    )(rel_bias, da_lambda, da_norm.reshape(-1, 1), bt, p0b, p0b, vt)


def _gqa_prompt(mode, q, k, vt, **kw):
    b, hq, t, dk = q.shape
    hk = k.shape[1]
    nr = hq // hk
    nq = t // BQ
    q_spec = pl.BlockSpec((1, nr, BQ, dk), lambda bi, g, i: (bi, g, i, 0))
    k_spec = pl.BlockSpec((1, 1, t, dk), lambda bi, g, i: (bi, g, 0, 0))
    vt_spec = pl.BlockSpec((1, 1, HEAD_DIM, t), lambda bi, g, i: (bi, g, 0, 0))
    ot_spec = pl.BlockSpec((1, nr, HEAD_DIM, BQ), lambda bi, g, i: (bi, g, 0, i))
    smem = pl.BlockSpec(memory_space=pltpu.SMEM)
    row_out = pl.BlockSpec((BQ, nr * HEAD_DIM), lambda bi, g, i: (bi * nq + i, g))
    if mode == "fox":
        in_specs = [q_spec, k_spec, vt_spec]
        args = (q, k, vt)
        out_shape = jax.ShapeDtypeStruct((b * t, hq * HEAD_DIM), BF16)
        out_spec = row_out
    elif mode == "sel":
        nbs = kw["memb"].shape[2]
        in_specs = [smem, pl.BlockSpec((nr, 4, BK, BQ), lambda bi, g, i: (1 + g, 0, 0, 0)),
                    q_spec, k_spec, vt_spec,
                    pl.BlockSpec((1, 1, nbs, BQ), lambda bi, g, i: (bi, g, 0, i))]
        args = (kw["rel_bias"], kw["bt"], q, k, vt, kw["memb"])
        out_shape = jax.ShapeDtypeStruct((b, hq, HEAD_DIM, t), F32)
        out_spec = ot_spec
    else:
        in_specs = [pl.BlockSpec((nr, 6, BK, BQ), lambda bi, g, i: (g, 0, 0, 0)),
                    q_spec, k_spec, vt_spec, ot_spec, ot_spec,
                    pl.BlockSpec((1, 1, 3 * nr, BQ), lambda bi, g, i: (bi, g, 0, i))]
        args = (kw["bt"], q, k, vt, kw["oc"], kw["os"], kw["gate"])
        out_shape = jax.ShapeDtypeStruct((b * t, hq * HEAD_DIM), BF16)
        out_spec = row_out
    return pl.pallas_call(
        functools.partial(_flash_kernel, mode=mode), name="gqa_" + mode,
        out_shape=out_shape, grid=(b, hk, nq), in_specs=in_specs, out_specs=out_spec,
        scratch_shapes=_flash_scratch(nr, HEAD_DIM),
        compiler_params=_params(("parallel", "parallel", "arbitrary")),
    )(*args)


def _diff_finish_kernel(o1_ref, o2_ref, lam_ref, nw_ref, o_ref):
    a = o1_ref[...] - _da_lambda(lam_ref[...]) * o2_ref[...]
    r = a * lax.rsqrt(jnp.mean(a * a, -1, keepdims=True) + RMS_EPS) * nw_ref[...] * (1.0 - DA_LAMBDA_INIT)
    o_ref[...] = r.astype(o_ref.dtype)


def _diff_finish(o1, o2, da_lambda, da_norm):
    n, w = o1.shape
    return pl.pallas_call(
        _diff_finish_kernel, name="diff_finish",
        out_shape=jax.ShapeDtypeStruct((n, w), BF16),
    )(o1, o2, da_lambda, da_norm.reshape(1, -1))


def _cmp_weight(w1):
    w1r = w1.reshape(2, 2, CMP_STRIDE, HEAD_DIM, CMP_HIDDEN)
    w1c = w1r[jnp.array([0, 0, 1, 1])]
    wb = jnp.einsum("chrdn,ce->rcdehn", w1c, jnp.eye(4, dtype=w1.dtype))
    return wb.reshape(CMP_STRIDE * 4 * HEAD_DIM, 4 * 2 * CMP_HIDDEN).astype(BF16)


def _cmp_finish_kernel(part_ref, pos_ref, w1_ref, w2_ref, kc_ref, vc_ref):
    nch = part_ref.shape[1]
    row = lax.broadcasted_iota(I32, (nch, HEAD_DIM), 0)
    for kv in range(2):
        posw = _dot(pos_ref[kv], w1_ref[kv])[0:1]
        for g in range(NSA_GROUPS):
            c = kv * NSA_GROUPS + g
            a = part_ref[0, :, 256 * c:256 * c + 128]
            bm = part_ref[0, :, 256 * c + 128:256 * c + 256]
            hid = a + pltpu.roll(bm, nch - 1, 0) + posw
            y = _dot(jax.nn.gelu(hid).astype(BF16), w2_ref[kv])
            y = jnp.where(row < nch - 1, y, 0.0).astype(BF16)
            if kv == 0:
                kc_ref[0, g] = y
            else:
                vc_ref[0, g] = y


def _cmp_finish(part, cmp_pos, w1, w2):
    b, nch, _ = part.shape
    pos8 = jnp.broadcast_to(cmp_pos.reshape(2, 1, CMP_BLOCK * HEAD_DIM), (2, 8, CMP_BLOCK * HEAD_DIM)).astype(BF16)
    out = jax.ShapeDtypeStruct((b, NSA_GROUPS, nch, HEAD_DIM), BF16)
    o_spec = pl.BlockSpec((1, NSA_GROUPS, nch, HEAD_DIM), lambda bi: (bi, 0, 0, 0))
    return pl.pallas_call(
        _cmp_finish_kernel, name="cmp_finish",
        out_shape=(out, out), grid=(b,),
        in_specs=[pl.BlockSpec((1, nch, 1024), lambda bi: (bi, 0, 0)), _const_spec(pos8.shape),
                  _const_spec(w1.shape), _const_spec(w2.shape)],
        out_specs=(o_spec, o_spec),
        compiler_params=_params(("parallel",)),
    )(part, pos8, w1.astype(BF16), w2.astype(BF16))


def _paged_cmp_kernel(pt_ref, *refs, npg):
    pages = refs[:npg]
    w_ref, o_ref = refs[npg:]
    nchunk = PAGE // CMP_STRIDE
    feat = pages[0].shape[1]
    ri = lax.broadcasted_iota(I32, (PAGE, PAGE), 0)
    ci = lax.broadcasted_iota(I32, (PAGE, PAGE), 1)
    perm = (ci == CMP_STRIDE * lax.bitwise_and(ri, nchunk - 1) + lax.shift_right_logical(ri, 3)).astype(BF16)
    xps = [_dot_nt(perm, pg[0].astype(BF16)) for pg in pages]
    acc = None
    for r in range(CMP_STRIDE):
        lhs = jnp.concatenate([xp[nchunk * r:nchunk * (r + 1)] for xp in xps], 0).astype(BF16)
        t = _dot(lhs, w_ref[feat * r:feat * (r + 1), :])
        acc = t if acc is None else acc + t
    o_ref[...] = acc


def _paged_cmp(pool, pt_flat, w, npg=16):
    n = pt_flat.shape[0]
    _, f, _ = pool.shape
    c = w.shape[1]
    npg = math.gcd(npg, n)
    nchunk = PAGE // CMP_STRIDE
    specs = [pl.BlockSpec((1, f, PAGE), (lambda s, pt, p=p: (pt[s * npg + p], 0, 0))) for p in range(npg)]
    grid_spec = pltpu.PrefetchScalarGridSpec(
        num_scalar_prefetch=1, grid=(n // npg,),
        in_specs=specs + [pl.BlockSpec(w.shape, lambda s, pt: (0, 0), pipeline_mode=pl.Buffered(1))],
        out_specs=pl.BlockSpec((npg * nchunk, c), lambda s, pt: (s, 0)))
    return pl.pallas_call(
        functools.partial(_paged_cmp_kernel, npg=npg), name="paged_cmp",
        out_shape=jax.ShapeDtypeStruct((n * nchunk, c), F32), grid_spec=grid_spec,
        compiler_params=_params(("parallel",)),
    )(pt_flat, *([pool] * npg), w)


def _overlap_t(nbs, ncp):
    jb = lax.broadcasted_iota(I32, (nbs, ncp), 0) * SEL_BLOCK
    ci = lax.broadcasted_iota(I32, (nbs, ncp), 1) * CMP_STRIDE
    return ((ci < jb + SEL_BLOCK) & (ci + CMP_BLOCK > jb)).astype(BF16)


def _cmp_prompt_kernel(rel_ref, q_ref, kc_ref, vc_ref, oc_ref, memb_ref, *, nbs, k_eff):
    g = pl.program_id(1)
    i = pl.program_id(2)
    nr = q_ref.shape[1]
    ncp = kc_ref.shape[2]
    qs = q_ref[0].reshape(nr * BQ, HEAD_DIM) * jnp.asarray(SCALE, BF16)
    s = _dot_nt(qs, kc_ref[0, 0])
    pos = i * BQ + lax.broadcasted_iota(I32, (BQ, ncp), 0)
    d = pos - (lax.broadcasted_iota(I32, (BQ, ncp), 1) * CMP_STRIDE + CMP_BLOCK - 1)
    ok = d >= 0
    okf = ok.astype(F32)
    p_parts = []
    for r in range(nr):
        h = DA_HEADS + g * nr + r
        sr = jnp.where(ok, s[r * BQ:(r + 1) * BQ] + _t5_bias(d, lambda b: rel_ref[b, h]), NEG_INF)
        e = jnp.exp(sr - jnp.max(sr, -1, keepdims=True))
        p_parts.append(e / jnp.sum(e, -1, keepdims=True) * okf)
    p = jnp.concatenate(p_parts, 0)
    oc_ref[0] = _dot(p.astype(BF16), vc_ref[0, 0]).reshape(nr, BQ, HEAD_DIM)
    psum = p_parts[0]
    for r in range(1, nr):
        psum = psum + p_parts[r]
    ot = _overlap_t(nbs, ncp)
    imp = None
    for part in _split3(psum):
        t = _dot_nt(ot, part)
        imp = t if imp is None else imp + t
    jrow = lax.broadcasted_iota(I32, (nbs, BQ), 0)
    cur = lax.shift_right_logical(i * BQ + lax.broadcasted_iota(I32, (nbs, BQ), 1), 6)
    valid = jrow <= cur
    forced = valid & ((jrow == 0) | (jrow >= cur - 1))
    score = jnp.where(valid, imp + jnp.where(forced, FORCED_BONUS, 0.0), -1.0)
    rank = jnp.zeros((nbs, BQ), I32)
    for ii in range(nbs):
        row = score[ii:ii + 1, :]
        ahead = (row > score) | ((row == score) & (jrow > ii))
        rank = rank + ahead.astype(I32)
    memb_ref[0, 0] = ((rank < k_eff) & (score >= 0.0)).astype(memb_ref.dtype)


def _cmp_prompt(q, kc, vc, rel_bias):
    b, hq, t, _ = q.shape
    nr = hq // NSA_GROUPS
    nq = t // BQ
    ncp = kc.shape[2]
    nbs = -(-t // SEL_BLOCK)
    q_spec = pl.BlockSpec((1, nr, BQ, HEAD_DIM), lambda bi, g, i: (bi, g, i, 0))
    c_spec = pl.BlockSpec((1, 1, ncp, HEAD_DIM), lambda bi, g, i: (bi, g, 0, 0))
    return pl.pallas_call(
        functools.partial(_cmp_prompt_kernel, nbs=nbs, k_eff=min(SEL_TOPK, nbs)), name="cmp_prompt",
        out_shape=(jax.ShapeDtypeStruct((b, hq, t, HEAD_DIM), F32),
                   jax.ShapeDtypeStruct((b, NSA_GROUPS, nbs, t), BF16)),
        grid=(b, NSA_GROUPS, nq),
        in_specs=[pl.BlockSpec(memory_space=pltpu.SMEM), q_spec, c_spec, c_spec],
        out_specs=(q_spec, pl.BlockSpec((1, 1, nbs, BQ), lambda bi, g, i: (bi, g, 0, i))),
        compiler_params=_params(("parallel", "parallel", "parallel")),
    )(rel_bias, q, kc, vc)


def _cmp_sample_kernel(q_ref, rowtbl_ref, kc_ref, vc_ref, oc_ref, memb_ref, *, qpos0, nbs, k_eff):
    ncp = kc_ref.shape[2]
    mb = memb_ref.shape[2]
    rows = NSA_R * 8
    qi = lax.bitwise_and(lax.broadcasted_iota(I32, (rows, ncp), 0), 7)
    d = qpos0 + qi - (lax.broadcasted_iota(I32, (rows, ncp), 1) * CMP_STRIDE + CMP_BLOCK - 1)
    ok = d >= 0
    jb = lax.broadcasted_iota(I32, (ncp, mb), 1) * SEL_BLOCK
    ci = lax.broadcasted_iota(I32, (ncp, mb), 0) * CMP_STRIDE
    ov = ((ci < jb + SEL_BLOCK) & (ci + CMP_BLOCK > jb)).astype(BF16)
    jl = lax.broadcasted_iota(I32, (8, mb), 1)
    cur = lax.shift_right_logical(qpos0 + lax.broadcasted_iota(I32, (8, mb), 0), 6)
    valid = (jl <= cur) & (jl < nbs)
    forced = valid & ((jl == 0) | (jl >= cur - 1))
    for g in range(NSA_GROUPS):
        s = _dot_nt(q_ref[0, g * rows:(g + 1) * rows], kc_ref[0, g])
        tbl = rowtbl_ref[g * rows:(g + 1) * rows]
        sm = jnp.where(ok, s + _t5_bias(d, lambda b: tbl[:, b:b + 1]), NEG_INF)
        e = jnp.exp(sm - jnp.max(sm, -1, keepdims=True))
        p = e / jnp.sum(e, -1, keepdims=True) * ok.astype(F32)
        oc_ref[0, g * rows:(g + 1) * rows] = _dot(p.astype(BF16), vc_ref[0, g])
        psum = p[0:8]
        for r in range(1, NSA_R):
            psum = psum + p[8 * r:8 * r + 8]
        imp = None
        for part in _split3(psum):
            t = _dot(part, ov)
            imp = t if imp is None else imp + t
        score = jnp.where(valid, imp + jnp.where(forced, FORCED_BONUS, 0.0), -1.0)
        score = jnp.where(jl < nbs, score, -2.0)
        rank = jnp.zeros((8, mb), I32)
        for ii in range(nbs):
            col = score[:, ii:ii + 1]
            ahead = (col > score) | ((col == score) & (jl > ii))
            rank = rank + ahead.astype(I32)
        memb_ref[0, g * 8:(g + 1) * 8] = ((rank < k_eff) & (score >= 0.0)).astype(F32)


def _cmp_sample(qrows, rowtbl, kc, vc, qpos0, nbs, mb):
    b = qrows.shape[0]
    ncp = kc.shape[2]
    nrows = NSA_GROUPS * NSA_R * 8
    c_spec = pl.BlockSpec((1, NSA_GROUPS, ncp, HEAD_DIM), lambda bi: (bi, 0, 0, 0))
    return pl.pallas_call(
        functools.partial(_cmp_sample_kernel, qpos0=qpos0, nbs=nbs, k_eff=min(SEL_TOPK, nbs)),
        name="cmp_sample",
        out_shape=(jax.ShapeDtypeStruct((b, nrows, HEAD_DIM), F32),
                   jax.ShapeDtypeStruct((b, NSA_GROUPS * 8, mb), F32)),
        grid=(b,),
        in_specs=[pl.BlockSpec((1, nrows, HEAD_DIM), lambda bi: (bi, 0, 0)), _const_spec(rowtbl.shape),
                  c_spec, c_spec],
        out_specs=(pl.BlockSpec((1, nrows, HEAD_DIM), lambda bi: (bi, 0, 0)),
                   pl.BlockSpec((1, NSA_GROUPS * 8, mb), lambda bi: (bi, 0, 0))),
        compiler_params=_params(("parallel",)),
    )(qrows, rowtbl, kc, vc)


def _decode_kernel(pt_ref, *refs, mode, npg, n_pages, pw, kpos0, qpos0, feat_major):
    q_ref, rowq_ref = refs[0], refs[1]
    k = 2
    if mode == "fox":
        cq_ref, aux_ref, auxtail_ref = refs[k:k + 3]
        k += 3
    else:
        rowtbl_ref = refs[k]
        k += 1
        if mode == "sel":
            aux_ref, auxtail_ref = refs[k:k + 2]
            k += 2
    pages = refs[k:k + npg]
    tail_ref, o_ref, m_ref, l_ref, acc_ref = refs[k + npg:k + npg + 5]
    step = pl.program_id(1)
    nr = q_ref.shape[1]
    q = q_ref[0]
    rowpos = qpos0 + rowq_ref[:, 0:1]
    key_shift = 3 if mode == "diffx" else 0

    @pl.when(step == 0)
    def _():
        _init_state(m_ref, l_ref, acc_ref)
        if mode == "diffx":
            far_ref = refs[k + npg + 5]
            col = lax.broadcasted_iota(I32, far_ref.shape, 1)
            far_ref[...] = jnp.where(lax.bitwise_and(col, 7) == rowq_ref[:, 1:2],
                                     rowtbl_ref[:, N_BUCKETS - 1:N_BUCKETS], NEG_INF)

    def process(page_list, kp0, aux, is_tail):
        kbs = [pg.astype(BF16) for pg in page_list]
        if feat_major:
            half = kbs[0].shape[0] // 2
            ss = [_dot(q, kb[:half]) for kb in kbs]
            widths = [kb.shape[1] for kb in kbs]
        else:
            ss = [_dot_nt(q, kb) for kb in kbs]
            widths = [kb.shape[0] for kb in kbs]
        s = ss[0] if len(ss) == 1 else jnp.concatenate(ss, 1)
        n = s.shape[1]

        def distance():
            col = lax.broadcasted_iota(I32, (nr, n), 1)
            return col, rowpos - (kp0 + lax.shift_right_logical(col, key_shift))

        def biased(sv):
            col, d = distance()
            tbl = rowtbl_ref[...]
            bad = d < 0
            if mode == "win":
                bad = bad | (d >= WINDOW)
            if mode == "sel":
                bad = bad | (aux < 0.5 * NEG_INF)
            if mode == "diffx":
                bad = bad | (lax.bitwise_and(col, 7) != rowq_ref[:, 1:2])
            return jnp.where(bad, NEG_INF, sv + _t5_bias(d, lambda b: tbl[:, b:b + 1]))

        if mode == "fox":
            s = s + (cq_ref[0] - jnp.concatenate([aux] * (nr // FOX_HEADS), 0))
            if is_tail:
                s = jnp.where(distance()[1] < 0, NEG_INF, s)
        elif is_tail or mode == "win":
            s = biased(s)
        else:
            near = qpos0 - (kp0 + (n >> key_shift) - 1) < T5_FAR
            far_bias = refs[k + npg + 5][...] if mode == "diffx" else aux
            s = lax.cond(near, lambda: biased(s), lambda: s + far_bias)

        def accum(p):
            out = None
            off = 0
            for kb, w in zip(kbs, widths):
                pp = p[:, off:off + w]
                if mode == "diffx":
                    pp = pltpu.roll(pp, 4, 1)
                pp = pp.astype(BF16)
                t = _dot_nt(pp, kb[half:]) if feat_major else _dot(pp, kb)
                out = t if out is None else out + t
                off += w
            return out

        _online_update(s, accum, m_ref, l_ref, acc_ref)

    keys_pp = pw >> key_shift
    process([pg[0] for pg in pages], kpos0 + step * (npg * keys_pp),
            aux_ref[0] if mode in ("fox", "sel") else None, False)

    @pl.when(step == pl.num_programs(1) - 1)
    def _():
        process([tail_ref[0]], qpos0, auxtail_ref[0] if mode in ("fox", "sel") else None, True)
        o_ref[0] = acc_ref[...] / l_ref[...]


def _decode(mode, qm, rowq, pool, pt_flat, tail, n_pages, npg, kpos0, qpos0, feat_major, **kw):
    b, nr, _ = qm.shape
    npg = math.gcd(npg, n_pages)
    n_steps = n_pages // npg
    pshape = pool.shape[1:]
    pw = pshape[1] if feat_major else pshape[0]
    accw = pshape[0] // 2 if feat_major else pshape[1]
    in_specs = [pl.BlockSpec((1,) + qm.shape[1:], lambda bi, s, pt: (bi, 0, 0)),
                pl.BlockSpec((nr, 2), lambda bi, s, pt: (0, 0))]
    args = [qm, rowq]
    if mode == "fox":
        in_specs.append(pl.BlockSpec((1, nr, 1), lambda bi, s, pt: (bi, 0, 0)))
        args.append(kw["cq"])
    else:
        in_specs.append(pl.BlockSpec((nr, N_BUCKETS), lambda bi, s, pt: (0, 0)))
        args.append(kw["rowtbl"])
    if mode in ("fox", "sel"):
        aux = kw["aux"]
        in_specs += [pl.BlockSpec((1, aux.shape[1], npg * pw), lambda bi, s, pt: (bi, 0, s)),
                     pl.BlockSpec((1, aux.shape[1], PAGE), lambda bi, s, pt: (bi, 0, (n_pages * pw) // PAGE))]
        args += [aux, aux]
    in_specs += [pl.BlockSpec((1,) + pshape, (lambda bi, s, pt, p=p: (pt[bi * n_pages + s * npg + p], 0, 0)))
                 for p in range(npg)]
    args += [pool] * npg
    in_specs.append(pl.BlockSpec((1,) + tail.shape[1:], lambda bi, s, pt: (bi, 0, 0)))
    args.append(tail)
    scratch = [pltpu.VMEM((nr, 1), F32), pltpu.VMEM((nr, 1), F32), pltpu.VMEM((nr, accw), F32)]
    if mode == "diffx":
        scratch.append(pltpu.VMEM((nr, npg * pw), F32))
    grid_spec = pltpu.PrefetchScalarGridSpec(
        num_scalar_prefetch=1, grid=(b, n_steps), in_specs=in_specs,
        out_specs=pl.BlockSpec((1, nr, accw), lambda bi, s, pt: (bi, 0, 0)),
        scratch_shapes=scratch)
    return pl.pallas_call(
        functools.partial(_decode_kernel, mode=mode, npg=npg, n_pages=n_pages, pw=pw, kpos0=kpos0, qpos0=qpos0,
                          feat_major=feat_major),
        name="decode_" + mode,
        out_shape=jax.ShapeDtypeStruct((b, nr, accw), F32), grid_spec=grid_spec,
        compiler_params=_params(("parallel", "arbitrary")),
    )(pt_flat, *args)


def _nsa_combine_kernel(oc_ref, os_ref, ow_ref, gate_ref, o_ref):
    gt = jax.nn.sigmoid(gate_ref[...])
    o = gt[:, 0:1] * oc_ref[...] + gt[:, 1:2] * os_ref[...] + gt[:, 2:3] * ow_ref[...]
    o_ref[...] = o.astype(o_ref.dtype)


def _nsa_combine(oc, os_, ow, gate):
    return pl.pallas_call(
        _nsa_combine_kernel, name="nsa_combine",
        out_shape=jax.ShapeDtypeStruct(oc.shape, BF16),
    )(oc, os_, ow, gate)


def _cumsum_kernel(pt_ref, bf_ref, cin_ref, *refs, npg, apply_logsig):
    pages = refs[:npg]
    lf_ref, c_ref, carry_ref = refs[npg:]

    @pl.when(pl.program_id(1) == 0)
    def _():
        carry_ref[...] = cin_ref[0]

    tri = (lax.broadcasted_iota(I32, (PAGE, PAGE), 0) <= lax.broadcasted_iota(I32, (PAGE, PAGE), 1)).astype(BF16)
    car = carry_ref[...]
    h = pages[0].shape[1]
    group = max(1, PAGE // h)
    for p0 in range(0, npg, group):
        xs = []
        for p in range(p0, min(p0 + group, npg)):
            x = pages[p][0]
            if apply_logsig:
                z = x + bf_ref[...]
                x = jnp.minimum(z, 0.0) - jnp.log1p(jnp.exp(-jnp.abs(z)))
            lf_ref[0, :, p * PAGE:(p + 1) * PAGE] = x
            xs.append(x)
        xcat = xs[0] if len(xs) == 1 else jnp.concatenate(xs, 0)
        loc = None
        for part in _split3(xcat):
            t = _dot(part, tri)
            loc = t if loc is None else loc + t
        for u, p in enumerate(range(p0, min(p0 + group, npg))):
            loc_p = loc[u * h:(u + 1) * h]
            c_ref[0, :, p * PAGE:(p + 1) * PAGE] = loc_p + car
            car = car + loc_p[:, PAGE - 1:PAGE]
    carry_ref[...] = car


def _cumsum_pages(pool, pt_flat, b, n_pages, b_f, carry_in, apply_logsig, npg=32):
    h = pool.shape[1]
    npg = math.gcd(npg, n_pages)
    in_specs = [pl.BlockSpec((h, 1), lambda bi, s, pt: (0, 0)),
                pl.BlockSpec((1, h, 1), lambda bi, s, pt: (bi, 0, 0))]
    in_specs += [pl.BlockSpec((1, h, PAGE), (lambda bi, s, pt, p=p: (pt[bi * n_pages + s * npg + p], 0, 0)))
                 for p in range(npg)]
    o_spec = pl.BlockSpec((1, h, npg * PAGE), lambda bi, s, pt: (bi, 0, s))
    out = jax.ShapeDtypeStruct((b, h, n_pages * PAGE), F32)
    grid_spec = pltpu.PrefetchScalarGridSpec(
        num_scalar_prefetch=1, grid=(b, n_pages // npg), in_specs=in_specs, out_specs=(o_spec, o_spec),
        scratch_shapes=[pltpu.VMEM((h, 1), F32)])
    return pl.pallas_call(
        functools.partial(_cumsum_kernel, npg=npg, apply_logsig=apply_logsig), name="cumsum_pages",
        out_shape=(out, out), grid_spec=grid_spec,
        compiler_params=_params(("parallel", "arbitrary")),
    )(pt_flat, b_f.reshape(h, 1), carry_in, *([pool] * npg))


def _router_kernel(x_ref, w_ref, o_ref):
    x = x_ref[...]
    w = w_ref[...]
    xh = x.astype(BF16)
    xl = (x - xh.astype(F32)).astype(BF16)
    wh = w.astype(BF16)
    wl = (w - wh.astype(F32)).astype(BF16)
    logits = _dot(xh, wh) + _dot(xl, wh) + _dot(xh, wl)
    lane = lax.broadcasted_iota(I32, logits.shape, 1)
    big = logits.shape[1]
    lg = jnp.where(lane < N_EXPERTS, logits, -jnp.inf)
    m1 = jnp.max(lg, -1, keepdims=True)
    i1 = jnp.min(jnp.where(lg == m1, lane, big), -1, keepdims=True)
    lg2 = jnp.where(lane == i1, -jnp.inf, lg)
    m2 = jnp.max(lg2, -1, keepdims=True)
    i2 = jnp.min(jnp.where(lg2 == m2, lane, big), -1, keepdims=True)
    e2 = jnp.exp(m2 - m1)
    g1 = 1.0 / (1.0 + e2)
    g2 = e2 / (1.0 + e2)
    out = jnp.where(lane == 0, i1.astype(F32), jnp.where(lane == 1, i2.astype(F32),
                    jnp.where(lane == 2, g1, jnp.where(lane == 3, g2, 0.0))))
    o_ref[...] = out


def _router(x, w_router):
    n, dm = x.shape
    tm = _row_tile(n)
    wp = jnp.zeros((dm, 128), F32).at[:, :N_EXPERTS].set(w_router)
    return pl.pallas_call(
        _router_kernel, name="router",
        out_shape=jax.ShapeDtypeStruct((n, 128), F32),
        grid=(n // tm,),
        in_specs=[pl.BlockSpec((tm, dm), lambda i: (i, 0)), _const_spec((dm, 128))],
        out_specs=pl.BlockSpec((tm, 128), lambda i: (i, 0)),
        compiler_params=_params(("parallel",)),
    )(x, wp)


def _row_copy(src_hbm, src_row, dst_ref, r, sem):
    return pltpu.make_async_copy(src_hbm.at[pl.ds(src_row, 1)], dst_ref.at[pl.ds(r, 1)], sem)


def _gather_start(idx_ref, n_rows, src_hbm, dst_ref, sem):
    for r in range(n_rows):
        _row_copy(src_hbm, idx_ref[0, 0, r], dst_ref, r, sem).start()


def _gather_wait(n_rows, src_hbm, dst_ref, sem):
    for r in range(n_rows):
        _row_copy(src_hbm, 0, dst_ref, r, sem).wait()


def _expert_kernel(be_ref, nb_ref, cur_ref, nxt_ref, x_hbm, wg_ref, wu_ref, wd_ref, o_ref, xbuf, sem):
    i = pl.program_id(0)
    n_used = nb_ref[0]
    slot = lax.bitwise_and(i, 1)

    @pl.when((i == 0) & (n_used > 0))
    def _():
        _gather_start(cur_ref, MOE_BLOCK, x_hbm, xbuf.at[0], sem.at[0])

    @pl.when(i + 1 < n_used)
    def _():
        _gather_start(nxt_ref, MOE_BLOCK, x_hbm, xbuf.at[1 - slot], sem.at[1 - slot])

    @pl.when(i < n_used)
    def _():
        _gather_wait(MOE_BLOCK, x_hbm, xbuf.at[slot], sem.at[slot])
        xb = xbuf[slot].astype(BF16)
        h = jax.nn.silu(_dot(xb, wg_ref[0])) * _dot(xb, wu_ref[0])
        o_ref[...] = _dot(h.astype(BF16), wd_ref[0])

    @pl.when(i >= n_used)
    def _():
        o_ref[...] = jnp.zeros(o_ref.shape, o_ref.dtype)


def _experts(x, src_rows, block_e, n_used, wg, wu, wd):
    nb = src_rows.shape[0]
    src_rows = src_rows.reshape(nb, 1, MOE_BLOCK)
    dm = x.shape[1]
    dff = wg.shape[2]
    grid_spec = pltpu.PrefetchScalarGridSpec(
        num_scalar_prefetch=2, grid=(nb,),
        in_specs=[pl.BlockSpec((1, 1, MOE_BLOCK), lambda i, be, nu: (i, 0, 0), memory_space=pltpu.SMEM),
                  pl.BlockSpec((1, 1, MOE_BLOCK), lambda i, be, nu: (jnp.minimum(i + 1, nb - 1), 0, 0),
                               memory_space=pltpu.SMEM),
                  pl.BlockSpec(memory_space=pl.ANY),
                  pl.BlockSpec((1, dm, dff), lambda i, be, nu: (be[i], 0, 0)),
                  pl.BlockSpec((1, dm, dff), lambda i, be, nu: (be[i], 0, 0)),
                  pl.BlockSpec((1, dff, dm), lambda i, be, nu: (be[i], 0, 0))],
        out_specs=pl.BlockSpec((MOE_BLOCK, dm), lambda i, be, nu: (i, 0)),
        scratch_shapes=[pltpu.VMEM((2, MOE_BLOCK, dm), F32), pltpu.SemaphoreType.DMA((2,))])
    return pl.pallas_call(
        _expert_kernel, name="experts",
        out_shape=jax.ShapeDtypeStruct((nb * MOE_BLOCK, dm), F32), grid_spec=grid_spec,
        compiler_params=_params(("arbitrary",)),
    )(block_e, n_used, src_rows, src_rows, x, wg, wu, wd)


def _combine_ln_kernel(cur_ref, nxt_ref, y_hbm, x_ref, route_ref, g_ref, b_ref, o_ref, ybuf, sem):
    i = pl.program_id(0)
    tm = x_ref.shape[0]
    slot = lax.bitwise_and(i, 1)

    @pl.when(i == 0)
    def _():
        _gather_start(cur_ref, TOP_K * tm, y_hbm, ybuf.at[0], sem.at[0])

    @pl.when(i + 1 < pl.num_programs(0))
    def _():
        _gather_start(nxt_ref, TOP_K * tm, y_hbm, ybuf.at[1 - slot], sem.at[1 - slot])

    _gather_wait(TOP_K * tm, y_hbm, ybuf.at[slot], sem.at[slot])
    gate = route_ref[...]
    f = gate[:, 2:3] * ybuf[slot, 0:tm] + gate[:, 3:4] * ybuf[slot, tm:2 * tm]
    o_ref[...] = _layer_norm(ALPHA * x_ref[...] + f, g_ref[...], b_ref[...])


def _combine_ln(x, route, y_sorted, pos, g, b):
    n, dm = x.shape
    tm = pos.shape[1] // TOP_K
    nt = n // tm
    pos = pos.reshape(nt, 1, TOP_K * tm)
    return pl.pallas_call(
        _combine_ln_kernel, name="combine_ln",
        out_shape=jax.ShapeDtypeStruct((n, dm), F32),
        grid=(nt,),
        in_specs=[pl.BlockSpec((1, 1, TOP_K * tm), lambda i: (i, 0, 0), memory_space=pltpu.SMEM),
                  pl.BlockSpec((1, 1, TOP_K * tm), lambda i: (jnp.minimum(i + 1, nt - 1), 0, 0),
                               memory_space=pltpu.SMEM),
                  pl.BlockSpec(memory_space=pl.ANY),
                  pl.BlockSpec((tm, dm), lambda i: (i, 0)), pl.BlockSpec((tm, 128), lambda i: (i, 0)),
                  _const_spec((1, dm)), _const_spec((1, dm))],
        out_specs=pl.BlockSpec((tm, dm), lambda i: (i, 0)),
        scratch_shapes=[pltpu.VMEM((2, TOP_K * tm, dm), F32), pltpu.SemaphoreType.DMA((2,))],
        compiler_params=_params(("arbitrary",)),
    )(pos, pos, y_sorted, x, route, g.reshape(1, dm), b.reshape(1, dm))


def _moe_ln(x, w_router, wg, wu, wd, g, b):
    n, dm = x.shape
    route = _router(x, w_router)
    top_e = route[:, 0:2].astype(I32)
    n_assign = n * TOP_K
    flat_e = top_e.reshape(-1)
    order = jnp.argsort(flat_e)
    e_sorted = flat_e[order]
    counts = jnp.bincount(flat_e, length=N_EXPERTS)
    padded = (counts + MOE_BLOCK - 1) // MOE_BLOCK * MOE_BLOCK
    start = jnp.cumsum(counts) - counts
    pad_end = jnp.cumsum(padded)
    pad_start = pad_end - padded
    dest = (pad_start[e_sorted] + jnp.arange(n_assign) - start[e_sorted]).astype(I32)
    n_blocks = -(-n_assign // MOE_BLOCK) + N_EXPERTS
    block_e = jnp.minimum(jnp.searchsorted(pad_end, jnp.arange(n_blocks) * MOE_BLOCK, side="right"),
                          N_EXPERTS - 1).astype(I32)
    n_used = (pad_end[-1] // MOE_BLOCK).astype(I32).reshape(1)
    src_rows = jnp.zeros((n_blocks * MOE_BLOCK,), I32).at[dest].set((order // TOP_K).astype(I32))
    y_sorted = _experts(x, src_rows.reshape(n_blocks, MOE_BLOCK), block_e, n_used,
                        wg.astype(BF16), wu.astype(BF16), wd.astype(BF16))
    tm = min(_row_tile(n), MOE_BLOCK)
    pos_of = jnp.zeros((n_assign,), I32).at[order].set(dest).reshape(n // tm, tm, TOP_K)
    pos = pos_of.transpose(0, 2, 1).reshape(n // tm, TOP_K * tm)
    return _combine_ln(x, route, y_sorted, pos, g, b)


C0_QDA, C0_DIFF, C0_QNS, C0_CMP, C0_SEL, C0_WIN, C0_GATE, C0 = 0, 512, 1536, 2048, 2304, 2560, 2816, 2944
C1_Q, C1_KV, C1_F, C1 = 0, 1024, 1536, 1664
FOX_DK = 128


def _l0_weight(w_in):
    k_off, v_off = 512, 1024
    cols = list(range(0, 512))
    for h in range(DA_HEADS):
        cols += list(range(k_off + 128 * h, k_off + 128 * (h + 1)))
        cols += list(range(v_off + 128 * h, v_off + 128 * (h + 1)))
    cols += list(range(1536, 2840))
    w = w_in[:, np.asarray(cols)]
    return jnp.pad(w, ((0, 0), (0, C0 - w.shape[1]))).astype(BF16)


def _pad_rows(x, rows):
    return jnp.pad(x, ((0, 0), (0, rows - x.shape[1])) + ((0, 0),) * (x.ndim - 2))


def _fox_augment(q, k, c):
    b, t = c.shape[:2]

    def split3(x):
        def top(v):
            return lax.bitcast_convert_type(lax.bitcast_convert_type(v, jnp.uint32) & jnp.uint32(0xFFFF0000), F32)
        hi = top(x)
        mid = top(x - hi)
        lo = top(x - hi - mid)
        return jnp.stack([hi.astype(BF16), mid.astype(BF16), lo.astype(BF16)], -1)

    c3 = split3(c)
    nc3 = split3(-c).reshape(b, t, FOX_KV_HEADS, FOX_R * 3)
    slot = jnp.asarray(np.repeat(np.eye(FOX_R), 3, axis=1), BF16)
    slot = jnp.broadcast_to(jnp.tile(slot, (FOX_KV_HEADS, 1))[None, None], (b, t, FOX_HEADS, 3 * FOX_R))
    pad_q = jnp.zeros((b, t, FOX_HEADS, FOX_DK - HEAD_DIM - 3 - 3 * FOX_R), BF16)
    qa = jnp.concatenate([q * jnp.asarray(SCALE, BF16), c3, slot, pad_q], -1)
    ones = jnp.ones((b, t, FOX_KV_HEADS, 3), BF16)
    pad_k = jnp.zeros((b, t, FOX_KV_HEADS, FOX_DK - HEAD_DIM - 3 - 3 * FOX_R), BF16)
    ka = jnp.concatenate([k, ones, nc3, pad_k], -1)
    return qa.transpose(0, 2, 1, 3), ka.transpose(0, 2, 1, 3)


def kernel(x_prompt, x_sample, cache_diff_kv, cache_nsa_cmp, cache_nsa_sel, state_nsa_win, cache_fox_kv,
           cache_fox_logf, page_table, rel_bias, l0_w_in, l0_w_out, da_lambda, da_norm, nsa_cmp_pos,
           nsa_cmp_w1, nsa_cmp_w2, ffn_w_gate, ffn_w_up, ffn_w_down, l1_w_in, fox_b_f, l1_w_out,
           moe_router, moe_w_gate, moe_w_up, moe_w_down, ln_g, ln_b):
    b, t, dm = x_prompt.shape
    bs, ts, _ = x_sample.shape
    n_pages = page_table.shape[1]
    n_phys = cache_diff_kv.shape[0]
    past = n_pages * PAGE
    np_, ns = b * t, bs * ts
    pt_flat = page_table.reshape(-1).astype(I32)
    x0 = jnp.concatenate([x_prompt.reshape(np_, dm), x_sample.reshape(ns, dm)], 0)
    g2 = NSA_GROUPS

    p0, p0b = _proj(x0, _l0_weight(l0_w_in))
    bt_c = _bias_tiles(rel_bias, 4, False, 0, DA_HEADS + NSA_HEADS)
    bt_w = _bias_tiles(rel_bias, 6, True, DA_HEADS, NSA_HEADS)
    rel_t = rel_bias.T

    vt_da = p0b[:np_, C0_DIFF:C0_DIFF + 1024].reshape(b, t, DA_HEADS, 4 * HEAD_DIM)[..., 2 * HEAD_DIM:]
    o_da_p = _diff_prompt(p0b, vt_da.transpose(0, 2, 3, 1), rel_bias, da_lambda, da_norm, bt_c, b, t,
                          C0_QDA // 128, C0_DIFF // 128)

    q_ns_p = p0b[:np_, C0_QNS:C0_QNS + 512].reshape(b, t, NSA_HEADS, HEAD_DIM).transpose(0, 2, 1, 3)

    def kv_major(col):
        kv = p0b[:np_, col:col + 256].reshape(b, t, 2, g2, HEAD_DIM)
        return kv[:, :, 0].transpose(0, 2, 1, 3), kv[:, :, 1].transpose(0, 2, 3, 1)

    wbig = _cmp_weight(nsa_cmp_w1)
    part_p = _rowmm(p0[:np_, C0_CMP:C0_CMP + 256].reshape(np_ // CMP_STRIDE, CMP_STRIDE * 256), wbig)
    kc_p, vc_p = _cmp_finish(part_p.reshape(b, t // CMP_STRIDE, 1024), nsa_cmp_pos, nsa_cmp_w1, nsa_cmp_w2)
    oc_p, memb_p = _cmp_prompt(q_ns_p, kc_p, vc_p, rel_bias)
    ks_p, vst_p = kv_major(C0_SEL)
    os_p = _gqa_prompt("sel", q_ns_p, ks_p, vst_p, rel_bias=rel_bias, bt=bt_c, memb=memb_p)
    kw_p, vwt_p = kv_major(C0_WIN)
    gate_p = p0[:np_, C0_GATE:C0_GATE + 3 * NSA_HEADS].reshape(b, t, g2, 3 * NSA_R).transpose(0, 2, 3, 1)
    o_ns_p = _gqa_prompt("win", q_ns_p, kw_p, vwt_p, bt=bt_w, oc=oc_p.transpose(0, 1, 3, 2), os=os_p, gate=gate_p)

    qpos0 = past
    qd = p0b[np_:, C0_QDA:C0_QDA + 512].reshape(bs, ts, DA_HEADS, 2, HEAD_DIM).transpose(0, 3, 2, 1, 4)
    qm_d = jnp.einsum("bwhqd,ws->bwhqsd", qd, jnp.eye(2, dtype=BF16)) * jnp.asarray(SCALE, BF16)
    qm_d = qm_d.reshape(bs, 2 * DA_HEADS * ts, 2 * HEAD_DIM)
    rows_d = [(w, h, q) for w in range(2) for h in range(DA_HEADS) for q in range(ts)]
    rowq_d = jnp.asarray(np.array([[r[2], r[1]] for r in rows_d], np.int32))
    rowtbl_d = rel_t[np.array([r[1] for r in rows_d])]

    def key_rows(x):
        lead = x.shape[:-2]
        x = x.reshape(lead + (PAGE, DA_HEADS, 2, 2 * HEAD_DIM))
        return jnp.swapaxes(x, -3, -2).reshape(lead + (PAGE * 2 * DA_HEADS, 2 * HEAD_DIM))

    tail_d = key_rows(_pad_rows(p0[np_:, C0_DIFF:C0_DIFF + 1024].reshape(bs, ts, 1024), PAGE))
    acc_d = _decode("diffx", qm_d, rowq_d, key_rows(cache_diff_kv.reshape(n_phys, PAGE, 1024)), pt_flat, tail_d,
                    n_pages, DECODE_PAGES, 0, qpos0, False, rowtbl=rowtbl_d)
    o12 = acc_d.reshape(bs, 2, DA_HEADS * ts, 2 * HEAD_DIM)
    o_da_s = _diff_finish(o12[:, 0].reshape(bs * DA_HEADS * ts, 128), o12[:, 1].reshape(bs * DA_HEADS * ts, 128),
                          da_lambda, da_norm)
    o_da_s = o_da_s.reshape(bs, DA_HEADS, ts, 128).transpose(0, 2, 1, 3).reshape(ns, 512)

    part_s = _paged_cmp(cache_nsa_cmp.transpose(0, 2, 3, 4, 1).reshape(n_phys, 4 * HEAD_DIM, PAGE), pt_flat, wbig)
    l_tot = past + ts
    if l_tot // CMP_STRIDE > past // CMP_STRIDE:
        raise NotImplementedError("new rows completing a compression chunk")
    kc_s, vc_s = _cmp_finish(part_s.reshape(bs, past // CMP_STRIDE, 1024), nsa_cmp_pos, nsa_cmp_w1, nsa_cmp_w2)
    nbs_s = -(-l_tot // SEL_BLOCK)
    mb_s = -(-nbs_s // 128) * 128
    qn = p0b[np_:, C0_QNS:C0_QNS + 512].reshape(bs, ts, g2, NSA_R, HEAD_DIM) * jnp.asarray(SCALE, BF16)
    q_cmp = _pad_rows(qn.reshape(bs, ts, NSA_HEADS, HEAD_DIM), 8).transpose(0, 2, 1, 3)
    q_cmp = q_cmp.reshape(bs, NSA_HEADS * 8, HEAD_DIM)
    rowtbl_c = rel_t[DA_HEADS + np.repeat(np.arange(NSA_HEADS), 8)]
    oc_s, memb_s = _cmp_sample(q_cmp, rowtbl_c, kc_s, vc_s, qpos0, nbs_s, mb_s)
    oc_s = oc_s.reshape(bs, NSA_HEADS, 8, HEAD_DIM)[:, :, :ts].transpose(0, 2, 1, 3)
    eye_g = jnp.eye(g2, dtype=BF16)
    qm_n = jnp.einsum("bqgrd,gs->bqgrsd", qn, eye_g).reshape(bs, ts * NSA_HEADS, g2 * HEAD_DIM)
    rows_n = [(q, g, r) for q in range(ts) for g in range(g2) for r in range(NSA_R)]
    rowq_n = jnp.asarray(np.array([[r[0], r[1] * NSA_R + r[2]] for r in rows_n], np.int32))
    rowtbl_n = rel_t[DA_HEADS + np.array([r[1] * NSA_R + r[2] for r in rows_n])]
    memb_rows = memb_s.reshape(bs, g2, 8, mb_s)[:, :, :ts, :nbs_s].transpose(0, 2, 1, 3)
    memb_rows = jnp.broadcast_to(memb_rows[:, :, :, None, :, None], (bs, ts, g2, NSA_R, nbs_s, SEL_BLOCK))
    memb_rows = memb_rows.reshape(bs, ts * NSA_HEADS, nbs_s * SEL_BLOCK)
    memb_rows = jnp.where(memb_rows > 0.5, rowtbl_n[None, :, N_BUCKETS - 1:N_BUCKETS], NEG_INF)
    memb_rows = jnp.pad(memb_rows, ((0, 0), (0, 0), (0, past + PAGE - nbs_s * SEL_BLOCK)), constant_values=NEG_INF)

    def feat_tail(col, width):
        return _pad_rows(p0[np_:, col:col + width].reshape(bs, ts, width), PAGE).transpose(0, 2, 1)

    pool_sel = cache_nsa_sel.transpose(0, 2, 3, 4, 1).reshape(n_phys, 4 * HEAD_DIM, PAGE)
    acc_sel = _decode("sel", qm_n, rowq_n, pool_sel, pt_flat, feat_tail(C0_SEL, 256),
                      n_pages, DECODE_PAGES, 0, qpos0, True, rowtbl=rowtbl_n, aux=memb_rows)
    w_buf = state_nsa_win.shape[1]
    pool_win = state_nsa_win.transpose(0, 2, 3, 4, 1).reshape(bs, 4 * HEAD_DIM, w_buf)
    acc_win = _decode("win", qm_n, rowq_n, pool_win, jnp.arange(bs, dtype=I32), feat_tail(C0_WIN, 256),
                      1, 1, qpos0 - w_buf, qpos0, True, rowtbl=rowtbl_n)

    def pick_group(acc):
        a = acc.reshape(bs, ts, g2, NSA_R, g2, HEAD_DIM)
        return jnp.stack([a[:, :, g, :, g] for g in range(g2)], 2)

    gate_s = p0[np_:, C0_GATE:C0_GATE + 3 * NSA_HEADS].reshape(ns * NSA_HEADS, 3)
    o_ns_s = _nsa_combine(oc_s.reshape(ns * NSA_HEADS, HEAD_DIM), pick_group(acc_sel).reshape(ns * NSA_HEADS, HEAD_DIM),
                          pick_group(acc_win).reshape(ns * NSA_HEADS, HEAD_DIM), gate_s).reshape(ns, 512)

    o_da = jnp.concatenate([o_da_p, o_da_s], 0)
    o_ns = jnp.concatenate([o_ns_p, o_ns_s], 0)
    x1 = _mix_ln([o_da, o_ns], l0_w_out.astype(BF16), x0, ln_g[0], ln_b[0])
    x2 = _ffn_ln(x1, ffn_w_gate.astype(BF16), ffn_w_up.astype(BF16), ffn_w_down.astype(BF16), ln_g[1], ln_b[1])

    w1p = jnp.pad(l1_w_in, ((0, 0), (0, C1 - l1_w_in.shape[1]))).astype(BF16)
    p1, p1b = _proj(x2, w1p)
    fl_p = p1[:np_, C1_F:C1_F + FOX_HEADS].reshape(np_ // PAGE, PAGE, FOX_HEADS).transpose(0, 2, 1)
    lft_p, ct_p = _cumsum_pages(fl_p, jnp.arange(np_ // PAGE, dtype=I32), b, t // PAGE, fox_b_f,
                                jnp.zeros((b, FOX_HEADS, 1), F32), True)
    q_fx = p1b[:np_, C1_Q:C1_Q + 1024].reshape(b, t, FOX_HEADS, HEAD_DIM)
    kv_fx = p1b[:np_, C1_KV:C1_KV + 512].reshape(b, t, 2, FOX_KV_HEADS, HEAD_DIM)
    qa_p, ka_p = _fox_augment(q_fx, kv_fx[:, :, 0], ct_p.transpose(0, 2, 1))
    o_fx_p = _gqa_prompt("fox", qa_p, ka_p, kv_fx[:, :, 1].transpose(0, 2, 3, 1))

    _, ct_past = _cumsum_pages(cache_fox_logf.transpose(0, 2, 1), pt_flat, bs, n_pages, fox_b_f,
                               jnp.zeros((bs, FOX_HEADS, 1), F32), False)
    f_new = _pad_rows(p1[np_:, C1_F:C1_F + FOX_HEADS].reshape(bs, ts, FOX_HEADS), PAGE).transpose(0, 2, 1)
    lft_s, ct_new = _cumsum_pages(f_new, jnp.arange(bs, dtype=I32), bs, 1, fox_b_f, ct_past[:, :, past - 1:past], True)
    ck_s = jnp.concatenate([ct_past, ct_new], 2)
    cq_s = ct_new[:, :, :ts].transpose(0, 2, 1).reshape(bs, ts * FOX_HEADS, 1)
    qf = p1b[np_:, C1_Q:C1_Q + 1024].reshape(bs, ts, FOX_KV_HEADS, FOX_R, HEAD_DIM) * jnp.asarray(SCALE, BF16)
    qm_f = jnp.einsum("bqgrd,gs->bqgrsd", qf, jnp.eye(FOX_KV_HEADS, dtype=BF16))
    qm_f = qm_f.reshape(bs, ts * FOX_HEADS, FOX_KV_HEADS * HEAD_DIM)
    rowq_f = jnp.asarray(np.stack([np.repeat(np.arange(ts), FOX_HEADS), np.tile(np.arange(FOX_HEADS), ts)],
                                  1).astype(np.int32))
    tail_f = _pad_rows(p1[np_:, C1_KV:C1_KV + 512].reshape(bs, ts, 512), PAGE).transpose(0, 2, 1)
    pool_fox = cache_fox_kv.transpose(0, 2, 3, 4, 1).reshape(n_phys, 2 * FOX_KV_HEADS * HEAD_DIM, PAGE)
    acc_f = _decode("fox", qm_f, rowq_f, pool_fox, pt_flat, tail_f, n_pages, DECODE_PAGES, 0, qpos0, True, cq=cq_s, aux=ck_s)
    acc_f = acc_f.reshape(bs, ts, FOX_KV_HEADS, FOX_R, FOX_KV_HEADS, HEAD_DIM)
    o_fx_s = jnp.stack([acc_f[:, :, g, :, g] for g in range(FOX_KV_HEADS)], 2)
    o_fx = jnp.concatenate([o_fx_p, o_fx_s.reshape(ns, 1024).astype(BF16)], 0)
    x3 = _mix_ln([o_fx], l1_w_out.astype(BF16), x2, ln_g[2], ln_b[2])

    y = _moe_ln(x3, moe_router, moe_w_gate, moe_w_up, moe_w_down, ln_g[3], ln_b[3])

    diff_all = p0[:, C0_DIFF:C0_DIFF + 1024]
    cmp_all = p0[:, C0_CMP:C0_CMP + 256]
    sel_all = p0[:, C0_SEL:C0_SEL + 256]
    win_all = p0[:, C0_WIN:C0_WIN + 256]
    fkv_all = p1[:, C1_KV:C1_KV + 512]
    keep_p = min(WINDOW, t)
    win_p_out = win_all[:np_].reshape(b, t, 2, g2, HEAD_DIM)[:, t - keep_p:]
    full_win_s = jnp.concatenate([state_nsa_win, win_all[np_:].reshape(bs, ts, 2, g2, HEAD_DIM)], 1)
    keep_s = min(WINDOW, full_win_s.shape[1])
    return (y[:np_].reshape(b, t, dm), y[np_:].reshape(bs, ts, dm),
            diff_all[:np_].reshape(b, t, DA_HEADS, 4 * HEAD_DIM), diff_all[np_:].reshape(bs, ts, DA_HEADS, 4 * HEAD_DIM),
            cmp_all[:np_].reshape(b, t, 2, g2, HEAD_DIM), cmp_all[np_:].reshape(bs, ts, 2, g2, HEAD_DIM),
            sel_all[:np_].reshape(b, t, 2, g2, HEAD_DIM), sel_all[np_:].reshape(bs, ts, 2, g2, HEAD_DIM),
            win_p_out, full_win_s[:, full_win_s.shape[1] - keep_s:],
            fkv_all[:np_].reshape(b, t, 2, FOX_KV_HEADS, HEAD_DIM), fkv_all[np_:].reshape(bs, ts, 2, FOX_KV_HEADS, HEAD_DIM),
            lft_p.transpose(0, 2, 1), lft_s[:, :, :ts].transpose(0, 2, 1))
```

```python
import functools
import math

import numpy as np
import jax
import jax.numpy as jnp
from jax import lax
from jax.experimental import pallas as pl
from jax.experimental.pallas import tpu as pltpu

F32, BF16, I32 = jnp.float32, jnp.bfloat16, jnp.int32

HEAD_DIM = 64
DA_HEADS = 4
NSA_HEADS = 8
NSA_GROUPS = 2
NSA_R = NSA_HEADS // NSA_GROUPS
CMP_BLOCK = 32
CMP_STRIDE = 16
CMP_HIDDEN = 2 * HEAD_DIM
SEL_BLOCK = 64
SEL_TOPK = 16
WINDOW = 512
FOX_HEADS = 16
FOX_KV_HEADS = 4
FOX_R = FOX_HEADS // FOX_KV_HEADS
N_BUCKETS = 32
MAX_DISTANCE = 128
N_EXPERTS = 8
TOP_K = 2
MOE_BLOCK = 128
LN_EPS = 1e-5
RMS_EPS = 1e-5
DEPTH = 2
ALPHA = (2.0 * DEPTH) ** 0.25
DA_LAMBDA_INIT = 0.8 - 0.6 * math.exp(-0.3 * 0)
NEG_INF = -1e30
FORCED_BONUS = 1e4
SCALE = HEAD_DIM ** -0.5

PAGE = 128
BQ = 128
BK = 256
DECODE_PAGES = 16
FAR_GROUP = 8
VMEM_LIMIT = 56 * 1024 * 1024


def _t5_thresholds():
    d = np.arange(0, 4 * MAX_DISTANCE)
    df = np.maximum(d, 1).astype(np.float32)
    max_exact = N_BUCKETS // 2
    large = max_exact + (np.log(df / np.float32(max_exact)) / np.float32(math.log(MAX_DISTANCE / max_exact))
                         * np.float32(N_BUCKETS - max_exact)).astype(np.int32)
    bucket = np.where(d < max_exact, d, np.minimum(large, N_BUCKETS - 1))
    return [int(np.argmax(bucket >= b)) for b in range(1, N_BUCKETS)]


T5_THR = _t5_thresholds()
T5_FAR = T5_THR[-1]


def _t5_bias(d, val):
    out = jnp.zeros(d.shape, F32) + val(0)
    for b in range(1, N_BUCKETS):
        out = jnp.where(d >= T5_THR[b - 1], val(b), out)
    return out


def _dot(a, b):
    return jnp.dot(a, b, preferred_element_type=F32)


def _dot_nt(a, b):
    return lax.dot_general(a, b, (((1,), (1,)), ((), ())), preferred_element_type=F32)


def _split3(x):
    hi = x.astype(BF16)
    r1 = x - hi.astype(F32)
    mid = r1.astype(BF16)
    lo = (r1 - mid.astype(F32)).astype(BF16)
    return hi, mid, lo


def _layer_norm(z, g, b):
    mu = jnp.mean(z, -1, keepdims=True)
    zc = z - mu
    var = jnp.mean(zc * zc, -1, keepdims=True)
    return zc * lax.rsqrt(var + LN_EPS) * g + b


def _online_update(s, accum, m_ref, l_ref, acc_ref):
    m_prev = m_ref[...]
    m_new = jnp.maximum(m_prev, jnp.max(s, -1, keepdims=True))
    alpha = jnp.exp(m_prev - m_new)
    p = jnp.exp(s - m_new)
    l_ref[...] = alpha * l_ref[...] + jnp.sum(p, -1, keepdims=True)
    acc_ref[...] = alpha * acc_ref[...] + accum(p)
    m_ref[...] = m_new


def _online_update_t(ss, vts, m_ref, l_ref, acc_ref):
    m_prev = m_ref[...]
    m_new = m_prev
    for s in ss:
        m_new = jnp.maximum(m_new, jnp.max(s, 0, keepdims=True))
    alpha = jnp.exp(m_prev - m_new)
    l_new = alpha * l_ref[...]
    acc_new = alpha * acc_ref[...]
    for s, vt in zip(ss, vts):
        p = jnp.exp(s - m_new)
        l_new = l_new + jnp.sum(p, 0, keepdims=True)
        acc_new = acc_new + _dot(vt, p.astype(BF16))
    l_ref[...] = l_new
    acc_ref[...] = acc_new
    m_ref[...] = m_new


def _init_state(m_ref, l_ref, acc_ref):
    m_ref[...] = jnp.full(m_ref.shape, NEG_INF, F32)
    l_ref[...] = jnp.zeros(l_ref.shape, F32)
    acc_ref[...] = jnp.zeros(acc_ref.shape, F32)


def _row_tile(n):
    for t in (512, 384, 256, 128, 64, 32, 16, 8):
        if n % t == 0:
            return t
    raise ValueError(f"row count {n} is not a multiple of 8")


def _params(sem):
    return pltpu.CompilerParams(dimension_semantics=sem, vmem_limit_bytes=VMEM_LIMIT)


def _const_spec(shape):
    nd = len(shape)
    return pl.BlockSpec(shape, lambda *a: (0,) * nd, pipeline_mode=pl.Buffered(1))


def _proj_kernel(x_ref, w_ref, o_ref, ob_ref):
    y = _dot(x_ref[...].astype(BF16), w_ref[...])
    o_ref[...] = y
    ob_ref[...] = y.astype(BF16)


def _proj(x, w):
    n, k = x.shape
    c = w.shape[1]
    tm = _row_tile(n)
    return pl.pallas_call(
        _proj_kernel, name="proj",
        out_shape=(jax.ShapeDtypeStruct((n, c), F32), jax.ShapeDtypeStruct((n, c), BF16)),
        grid=(n // tm,),
        in_specs=[pl.BlockSpec((tm, k), lambda i: (i, 0)), _const_spec((k, c))],
        out_specs=(pl.BlockSpec((tm, c), lambda i: (i, 0)), pl.BlockSpec((tm, c), lambda i: (i, 0))),
        compiler_params=_params(("parallel",)),
    )(x, w)


def _rowmm_kernel(x_ref, w_ref, o_ref):
    o_ref[...] = _dot(x_ref[...].astype(BF16), w_ref[...])


def _rowmm(x, w):
    n, k = x.shape
    c = w.shape[1]
    tm = _row_tile(n)
    tm = min(tm, 256)
    return pl.pallas_call(
        _rowmm_kernel, name="rowmm",
        out_shape=jax.ShapeDtypeStruct((n, c), F32),
        grid=(n // tm,),
        in_specs=[pl.BlockSpec((tm, k), lambda i: (i, 0)), _const_spec((k, c))],
        out_specs=pl.BlockSpec((tm, c), lambda i: (i, 0)),
        compiler_params=_params(("parallel",)),
    )(x, w)


def _mix_ln_kernel(*refs, n_in):
    a_refs = refs[:n_in]
    w_ref, x_ref, g_ref, b_ref, y_ref = refs[n_in:]
    acc = None
    off = 0
    for a in a_refs:
        k = a.shape[1]
        t = _dot(a[...], w_ref[off:off + k, :])
        acc = t if acc is None else acc + t
        off += k
    y_ref[...] = _layer_norm(ALPHA * x_ref[...] + acc, g_ref[...], b_ref[...])


def _mix_ln(a_list, w, x, g, b):
    n, dm = x.shape
    tm = _row_tile(n)
    in_specs = [pl.BlockSpec((tm, a.shape[1]), lambda i: (i, 0)) for a in a_list]
    in_specs += [_const_spec(w.shape), pl.BlockSpec((tm, dm), lambda i: (i, 0)),
                 _const_spec((1, dm)), _const_spec((1, dm))]
    return pl.pallas_call(
        functools.partial(_mix_ln_kernel, n_in=len(a_list)), name="mix_ln",
        out_shape=jax.ShapeDtypeStruct((n, dm), F32),
        grid=(n // tm,), in_specs=in_specs,
        out_specs=pl.BlockSpec((tm, dm), lambda i: (i, 0)),
        compiler_params=_params(("parallel",)),
    )(*a_list, w, x, g.reshape(1, dm), b.reshape(1, dm))


def _ffn_ln_kernel(x_ref, wg_ref, wu_ref, wd_ref, g_ref, b_ref, y_ref):
    x = x_ref[...]
    xb = x.astype(BF16)
    h = jax.nn.silu(_dot(xb, wg_ref[...])) * _dot(xb, wu_ref[...])
    f = _dot(h.astype(BF16), wd_ref[...])
    y_ref[...] = _layer_norm(ALPHA * x + f, g_ref[...], b_ref[...])


def _ffn_ln(x, wg, wu, wd, g, b):
    n, dm = x.shape
    tm = min(_row_tile(n), 384)
    return pl.pallas_call(
        _ffn_ln_kernel, name="ffn_ln",
        out_shape=jax.ShapeDtypeStruct((n, dm), F32),
        grid=(n // tm,),
        in_specs=[pl.BlockSpec((tm, dm), lambda i: (i, 0)), _const_spec(wg.shape), _const_spec(wu.shape),
                  _const_spec(wd.shape), _const_spec((1, dm)), _const_spec((1, dm))],
        out_specs=pl.BlockSpec((tm, dm), lambda i: (i, 0)),
        compiler_params=_params(("parallel",)),
    )(x, wg, wu, wd, g.reshape(1, dm), b.reshape(1, dm))


def _bias_tiles_kernel(rel_ref, o_ref, *, nt, window, head0):
    h = pl.program_id(0) + head0
    kk = lax.broadcasted_iota(I32, (BK, BQ), 0)
    qq = lax.broadcasted_iota(I32, (BK, BQ), 1)
    for t in range(nt):
        d = t * BQ + qq - kk
        bias = _t5_bias(d, lambda b: rel_ref[b, h])
        bad = d < 0
        if window:
            bad = bad | (d >= WINDOW)
        o_ref[0, t] = jnp.where(bad, NEG_INF, bias)


def _bias_tiles(rel_bias, nt, window, head0, nheads):
    return pl.pallas_call(
        functools.partial(_bias_tiles_kernel, nt=nt, window=window, head0=head0), name="bias_tiles",
        out_shape=jax.ShapeDtypeStruct((nheads, nt, BK, BQ), F32),
        grid=(nheads,),
        in_specs=[pl.BlockSpec(memory_space=pltpu.SMEM)],
        out_specs=pl.BlockSpec((1, nt, BK, BQ), lambda h: (h, 0, 0, 0)),
        compiler_params=_params(("parallel",)),
    )(rel_bias)


def _da_lambda(lp):
    return (jnp.exp(jnp.sum(lp[0:1] * lp[1:2], -1, keepdims=True))
            - jnp.exp(jnp.sum(lp[2:3] * lp[3:4], -1, keepdims=True)) + DA_LAMBDA_INIT)


def _to_rows(xt):
    eye = (lax.broadcasted_iota(I32, (BQ, BQ), 0) == lax.broadcasted_iota(I32, (BQ, BQ), 1)).astype(BF16)
    return _dot_nt(eye, xt)


def _flash_kernel(*refs, mode):
    if mode == "diff":
        rel_ref, lam_ref, nw_ref, bt_ref, q_ref, k_ref, vt_ref, o_ref, m_ref, l_ref, acc_ref = refs
    elif mode == "fox":
        q_ref, k_ref, vt_ref, o_ref, m_ref, l_ref, acc_ref = refs
    elif mode == "sel":
        rel_ref, bt_ref, q_ref, k_ref, vt_ref, memb_ref, o_ref, m_ref, l_ref, acc_ref = refs
    else:
        bt_ref, q_ref, k_ref, vt_ref, oc_ref, os_ref, gate_ref, o_ref, m_ref, l_ref, acc_ref = refs
    g = pl.program_id(1)
    i = pl.program_id(2)
    if mode == "diff":
        jd = i
        part_t0 = (0, 1, 0, 1)
    else:
        jd = lax.shift_right_logical(i, 1)
        t0 = lax.bitwise_and(i, 1)
    if mode == "diff":
        nr = 4
        q = q_ref[0, 0]
        row = lax.broadcasted_iota(I32, q.shape, 0)
        zero = jnp.zeros_like(q)
        qs = jnp.concatenate([jnp.where(row < HEAD_DIM, q, zero), jnp.where(row >= HEAD_DIM, q, zero)], 1)
        qs = qs * jnp.asarray(SCALE, BF16)
    else:
        nr = q_ref.shape[1]
        qs = jnp.concatenate([q_ref[0, r] for r in range(nr)], 1)
        if mode != "fox":
            qs = qs * jnp.asarray(SCALE, BF16)
    _init_state(m_ref, l_ref, acc_ref)

    def scores(j, back, diag):
        off = pl.multiple_of(j * BK, BK)
        if mode == "diff":
            kt = k_ref[pl.ds(off, BK), :]
        else:
            kt = k_ref[0, 0, pl.ds(off, BK), :]
        s = _dot(kt, qs)
        if mode == "fox":
            if diag:
                kk = lax.broadcasted_iota(I32, (BK, BQ), 0)
                qq = lax.broadcasted_iota(I32, (BK, BQ), 1)
                ok = (t0 * BQ + qq - kk) >= 0
                s = jnp.where(jnp.concatenate([ok] * nr, 1), s, NEG_INF)
        else:
            extra = None
            if mode == "sel":
                nbs = memb_ref.shape[2]
                kk = lax.broadcasted_iota(I32, (BK, nbs), 0)
                jb = lax.broadcasted_iota(I32, (BK, nbs), 1)
                e = (jb == j * (BK // SEL_BLOCK) + lax.shift_right_logical(kk, 6)).astype(BF16)
                extra = (_dot(e, memb_ref[0, 0]) - 1.0) * (-NEG_INF)
            parts = []
            for r in range(nr):
                if back is None:
                    br = rel_ref[N_BUCKETS - 1, g if mode == "diff" else DA_HEADS + g * nr + r]
                elif mode == "diff":
                    br = bt_ref[0, part_t0[r] + 2 * back]
                else:
                    br = bt_ref[r, t0 + 2 * back]
                if extra is not None:
                    br = br + extra
                parts.append(s[:, r * BQ:(r + 1) * BQ] + br)
            s = jnp.concatenate(parts, 1)
        return s, vt_ref[0, 0, :, pl.ds(off, BK)]

    def tiles(spec):
        ss, vts = zip(*[scores(*a) for a in spec])
        _online_update_t(ss, vts, m_ref, l_ref, acc_ref)

    if mode != "win":
        n_far = jd if mode == "fox" else jnp.maximum(jd - 1, 0)

        def far_body(jg, c):
            tiles([(FAR_GROUP * jg + u, None, False) for u in range(FAR_GROUP)])
            return c

        lax.fori_loop(0, lax.shift_right_logical(n_far, FAR_GROUP.bit_length() - 1), far_body, 0)
        base = lax.bitwise_and(n_far, -FAR_GROUP)
        width = FAR_GROUP // 2
        while width >= 1:
            @pl.when(lax.bitwise_and(n_far, width) != 0)
            def _(base=base, width=width):
                tiles([(base + u, None, False) for u in range(width)])

            base = base + lax.bitwise_and(n_far, width)
            width //= 2
    n_near = 1 if mode == "fox" else (3 if mode == "win" else 2)
    for cnt in range(1, n_near + 1):
        cond = (jd >= cnt - 1) if cnt == n_near else (jd == cnt - 1)

        @pl.when(cond)
        def _(cnt=cnt):
            tiles([(jd - a, a, a == 0) for a in reversed(range(cnt))])

    ot = acc_ref[...] / l_ref[...]
    if mode == "diff":
        a = ot[:, :2 * BQ] - _da_lambda(lam_ref[...]) * ot[:, 2 * BQ:]
        r = a * lax.rsqrt(jnp.mean(a * a, 0, keepdims=True) + RMS_EPS) * nw_ref[...] * (1.0 - DA_LAMBDA_INIT)
        r = r.astype(BF16)
        o_ref[...] = jnp.concatenate([_to_rows(r[:, :BQ]), _to_rows(r[:, BQ:])], 0).astype(o_ref.dtype)
    elif mode == "fox":
        o_ref[...] = jnp.concatenate([_to_rows(ot[:, r * BQ:(r + 1) * BQ].astype(BF16)) for r in range(nr)],
                                     1).astype(o_ref.dtype)
    elif mode == "sel":
        for r in range(nr):
            o_ref[0, r] = ot[:, r * BQ:(r + 1) * BQ]
    else:
        gt = jax.nn.sigmoid(gate_ref[0, 0])
        parts = []
        for r in range(nr):
            comb = (gt[3 * r:3 * r + 1] * oc_ref[0, r] + gt[3 * r + 1:3 * r + 2] * os_ref[0, r]
                    + gt[3 * r + 2:3 * r + 3] * ot[:, r * BQ:(r + 1) * BQ])
            parts.append(_to_rows(comb.astype(BF16)))
        o_ref[...] = jnp.concatenate(parts, 1).astype(o_ref.dtype)


def _flash_scratch(nr, dv):
    return [pltpu.VMEM((1, nr * BQ), F32), pltpu.VMEM((1, nr * BQ), F32), pltpu.VMEM((dv, nr * BQ), F32)]


def _diff_prompt(p0b, qt, vt, rel_bias, da_lambda, da_norm, bt, b, t, col_k):
    nq = t // BK
    dv = 2 * HEAD_DIM
    return pl.pallas_call(
        functools.partial(_flash_kernel, mode="diff"), name="diff_prompt",
        out_shape=jax.ShapeDtypeStruct((b * t, DA_HEADS * dv), BF16),
        grid=(b, DA_HEADS, nq),
        in_specs=[pl.BlockSpec(memory_space=pltpu.SMEM),
                  _const_spec((4, HEAD_DIM)), _const_spec((dv, 1)),
                  pl.BlockSpec((1, 4, BK, BQ), lambda bi, h, i: (h, 0, 0, 0)),
                  pl.BlockSpec((1, 1, dv, BK), lambda bi, h, i: (bi, h, 0, i)),
                  pl.BlockSpec((t, 128), lambda bi, h, i: (bi, col_k + 2 * h)),
                  pl.BlockSpec((1, 1, dv, t), lambda bi, h, i: (bi, h, 0, 0))],
        out_specs=pl.BlockSpec((BK, dv), lambda bi, h, i: (bi * nq + i, h)),
        scratch_shapes=_flash_scratch(4, dv),
        compiler_params=_params(("parallel", "parallel", "arbitrary")),
    )(rel_bias, da_lambda, da_norm.reshape(-1, 1), bt, qt, p0b, vt)


def _gqa_prompt(mode, q, k, vt, **kw):
    b, hq, dk, t = q.shape
    hk = k.shape[1]
    nr = hq // hk
    nq = t // BQ
    q_spec = pl.BlockSpec((1, nr, dk, BQ), lambda bi, g, i: (bi, g, 0, i))
    k_spec = pl.BlockSpec((1, 1, t, dk), lambda bi, g, i: (bi, g, 0, 0))
    vt_spec = pl.BlockSpec((1, 1, HEAD_DIM, t), lambda bi, g, i: (bi, g, 0, 0))
    ot_spec = pl.BlockSpec((1, nr, HEAD_DIM, BQ), lambda bi, g, i: (bi, g, 0, i))
    smem = pl.BlockSpec(memory_space=pltpu.SMEM)
    row_out = pl.BlockSpec((BQ, nr * HEAD_DIM), lambda bi, g, i: (bi * nq + i, g))
    if mode == "fox":
        in_specs = [q_spec, k_spec, vt_spec]
        args = (q, k, vt)
        out_shape = jax.ShapeDtypeStruct((b * t, hq * HEAD_DIM), BF16)
        out_spec = row_out
    elif mode == "sel":
        nbs = kw["memb"].shape[2]
        in_specs = [smem, pl.BlockSpec((nr, 4, BK, BQ), lambda bi, g, i: (1 + g, 0, 0, 0)),
                    q_spec, k_spec, vt_spec,
                    pl.BlockSpec((1, 1, nbs, BQ), lambda bi, g, i: (bi, g, 0, i))]
        args = (kw["rel_bias"], kw["bt"], q, k, vt, kw["memb"])
        out_shape = jax.ShapeDtypeStruct((b, hq, HEAD_DIM, t), F32)
        out_spec = ot_spec
    else:
        in_specs = [pl.BlockSpec((nr, 6, BK, BQ), lambda bi, g, i: (g, 0, 0, 0)),
                    q_spec, k_spec, vt_spec, ot_spec, ot_spec,
                    pl.BlockSpec((1, 1, 3 * nr, BQ), lambda bi, g, i: (bi, g, 0, i))]
        args = (kw["bt"], q, k, vt, kw["oc"], kw["os"], kw["gate"])
        out_shape = jax.ShapeDtypeStruct((b * t, hq * HEAD_DIM), BF16)
        out_spec = row_out
    return pl.pallas_call(
        functools.partial(_flash_kernel, mode=mode), name="gqa_" + mode,
        out_shape=out_shape, grid=(b, hk, nq), in_specs=in_specs, out_specs=out_spec,
        scratch_shapes=_flash_scratch(nr, HEAD_DIM),
        compiler_params=_params(("parallel", "parallel", "arbitrary")),
    )(*args)


def _diff_finish_kernel(o1_ref, o2_ref, lam_ref, nw_ref, o_ref):
    a = o1_ref[...] - _da_lambda(lam_ref[...]) * o2_ref[...]
    r = a * lax.rsqrt(jnp.mean(a * a, -1, keepdims=True) + RMS_EPS) * nw_ref[...] * (1.0 - DA_LAMBDA_INIT)
    o_ref[...] = r.astype(o_ref.dtype)


def _diff_finish(o1, o2, da_lambda, da_norm):
    n, w = o1.shape
    return pl.pallas_call(
        _diff_finish_kernel, name="diff_finish",
        out_shape=jax.ShapeDtypeStruct((n, w), BF16),
    )(o1, o2, da_lambda, da_norm.reshape(1, -1))


def _cmp_weight(w1):
    w1r = w1.reshape(2, 2, CMP_STRIDE, HEAD_DIM, CMP_HIDDEN)
    w1c = w1r[jnp.array([0, 0, 1, 1])]
    wb = jnp.einsum("chrdn,ce->rcdehn", w1c, jnp.eye(4, dtype=w1.dtype))
    return wb.reshape(CMP_STRIDE * 4 * HEAD_DIM, 4 * 2 * CMP_HIDDEN).astype(BF16)


def _cmp_finish_kernel(part_ref, pos_ref, w1_ref, w2_ref, kc_ref, vc_ref):
    nch = part_ref.shape[1]
    row = lax.broadcasted_iota(I32, (nch, HEAD_DIM), 0)
    for kv in range(2):
        posw = _dot(pos_ref[kv], w1_ref[kv])[0:1]
        for g in range(NSA_GROUPS):
            c = kv * NSA_GROUPS + g
            a = part_ref[0, :, 256 * c:256 * c + 128]
            bm = part_ref[0, :, 256 * c + 128:256 * c + 256]
            hid = a + pltpu.roll(bm, nch - 1, 0) + posw
            y = _dot(jax.nn.gelu(hid).astype(BF16), w2_ref[kv])
            y = jnp.where(row < nch - 1, y, 0.0).astype(BF16)
            if kv == 0:
                kc_ref[0, g] = y
            else:
                vc_ref[0, g] = y


def _cmp_finish(part, cmp_pos, w1, w2):
    b, nch, _ = part.shape
    pos8 = jnp.broadcast_to(cmp_pos.reshape(2, 1, CMP_BLOCK * HEAD_DIM), (2, 8, CMP_BLOCK * HEAD_DIM)).astype(BF16)
    out = jax.ShapeDtypeStruct((b, NSA_GROUPS, nch, HEAD_DIM), BF16)
    o_spec = pl.BlockSpec((1, NSA_GROUPS, nch, HEAD_DIM), lambda bi: (bi, 0, 0, 0))
    return pl.pallas_call(
        _cmp_finish_kernel, name="cmp_finish",
        out_shape=(out, out), grid=(b,),
        in_specs=[pl.BlockSpec((1, nch, 1024), lambda bi: (bi, 0, 0)), _const_spec(pos8.shape),
                  _const_spec(w1.shape), _const_spec(w2.shape)],
        out_specs=(o_spec, o_spec),
        compiler_params=_params(("parallel",)),
    )(part, pos8, w1.astype(BF16), w2.astype(BF16))


def _paged_cmp_kernel(pt_ref, *refs, npg):
    pages = refs[:npg]
    w_ref, o_ref = refs[npg:]
    nchunk = PAGE // CMP_STRIDE
    feat = pages[0].shape[1]
    ri = lax.broadcasted_iota(I32, (PAGE, PAGE), 0)
    ci = lax.broadcasted_iota(I32, (PAGE, PAGE), 1)
    perm = (ci == CMP_STRIDE * lax.bitwise_and(ri, nchunk - 1) + lax.shift_right_logical(ri, 3)).astype(BF16)
    xps = [_dot_nt(perm, pg[0].astype(BF16)) for pg in pages]
    acc = None
    for r in range(CMP_STRIDE):
        lhs = jnp.concatenate([xp[nchunk * r:nchunk * (r + 1)] for xp in xps], 0).astype(BF16)
        t = _dot(lhs, w_ref[feat * r:feat * (r + 1), :])
        acc = t if acc is None else acc + t
    o_ref[...] = acc


def _paged_cmp(pool, pt_flat, w, npg=16):
    n = pt_flat.shape[0]
    _, f, _ = pool.shape
    c = w.shape[1]
    npg = math.gcd(npg, n)
    nchunk = PAGE // CMP_STRIDE
    specs = [pl.BlockSpec((1, f, PAGE), (lambda s, pt, p=p: (pt[s * npg + p], 0, 0))) for p in range(npg)]
    grid_spec = pltpu.PrefetchScalarGridSpec(
        num_scalar_prefetch=1, grid=(n // npg,),
        in_specs=specs + [pl.BlockSpec(w.shape, lambda s, pt: (0, 0), pipeline_mode=pl.Buffered(1))],
        out_specs=pl.BlockSpec((npg * nchunk, c), lambda s, pt: (s, 0)))
    return pl.pallas_call(
        functools.partial(_paged_cmp_kernel, npg=npg), name="paged_cmp",
        out_shape=jax.ShapeDtypeStruct((n * nchunk, c), F32), grid_spec=grid_spec,
        compiler_params=_params(("parallel",)),
    )(pt_flat, *([pool] * npg), w)


def _overlap_t(nbs, ncp):
    jb = lax.broadcasted_iota(I32, (nbs, ncp), 0) * SEL_BLOCK
    ci = lax.broadcasted_iota(I32, (nbs, ncp), 1) * CMP_STRIDE
    return ((ci < jb + SEL_BLOCK) & (ci + CMP_BLOCK > jb)).astype(BF16)


def _cmp_prompt_kernel(rel_ref, q_ref, kc_ref, vc_ref, oc_ref, memb_ref, *, nbs, k_eff):
    g = pl.program_id(1)
    i = pl.program_id(2)
    nr = q_ref.shape[1]
    ncp = kc_ref.shape[2]
    qs = q_ref[0].reshape(nr * BQ, HEAD_DIM) * jnp.asarray(SCALE, BF16)
    s = _dot_nt(qs, kc_ref[0, 0])
    pos = i * BQ + lax.broadcasted_iota(I32, (BQ, ncp), 0)
    d = pos - (lax.broadcasted_iota(I32, (BQ, ncp), 1) * CMP_STRIDE + CMP_BLOCK - 1)
    ok = d >= 0
    okf = ok.astype(F32)
    p_parts = []
    for r in range(nr):
        h = DA_HEADS + g * nr + r
        sr = jnp.where(ok, s[r * BQ:(r + 1) * BQ] + _t5_bias(d, lambda b: rel_ref[b, h]), NEG_INF)
        e = jnp.exp(sr - jnp.max(sr, -1, keepdims=True))
        p_parts.append(e / jnp.sum(e, -1, keepdims=True) * okf)
    p = jnp.concatenate(p_parts, 0)
    oc_ref[0] = _dot(p.astype(BF16), vc_ref[0, 0]).reshape(nr, BQ, HEAD_DIM)
    psum = p_parts[0]
    for r in range(1, nr):
        psum = psum + p_parts[r]
    ot = _overlap_t(nbs, ncp)
    imp = None
    for part in _split3(psum):
        t = _dot_nt(ot, part)
        imp = t if imp is None else imp + t
    jrow = lax.broadcasted_iota(I32, (nbs, BQ), 0)
    cur = lax.shift_right_logical(i * BQ + lax.broadcasted_iota(I32, (nbs, BQ), 1), 6)
    valid = jrow <= cur
    forced = valid & ((jrow == 0) | (jrow >= cur - 1))
    score = jnp.where(valid, imp + jnp.where(forced, FORCED_BONUS, 0.0), -1.0)
    rank = jnp.zeros((nbs, BQ), I32)
    for ii in range(nbs):
        row = score[ii:ii + 1, :]
        ahead = (row > score) | ((row == score) & (jrow > ii))
        rank = rank + ahead.astype(I32)
    memb_ref[0, 0] = ((rank < k_eff) & (score >= 0.0)).astype(memb_ref.dtype)


def _cmp_prompt(q, kc, vc, rel_bias):
    b, hq, t, _ = q.shape
    nr = hq // NSA_GROUPS
    nq = t // BQ
    ncp = kc.shape[2]
    nbs = -(-t // SEL_BLOCK)
    q_spec = pl.BlockSpec((1, nr, BQ, HEAD_DIM), lambda bi, g, i: (bi, g, i, 0))
    c_spec = pl.BlockSpec((1, 1, ncp, HEAD_DIM), lambda bi, g, i: (bi, g, 0, 0))
    return pl.pallas_call(
        functools.partial(_cmp_prompt_kernel, nbs=nbs, k_eff=min(SEL_TOPK, nbs)), name="cmp_prompt",
        out_shape=(jax.ShapeDtypeStruct((b, hq, t, HEAD_DIM), F32),
                   jax.ShapeDtypeStruct((b, NSA_GROUPS, nbs, t), BF16)),
        grid=(b, NSA_GROUPS, nq),
        in_specs=[pl.BlockSpec(memory_space=pltpu.SMEM), q_spec, c_spec, c_spec],
        out_specs=(q_spec, pl.BlockSpec((1, 1, nbs, BQ), lambda bi, g, i: (bi, g, 0, i))),
        compiler_params=_params(("parallel", "parallel", "parallel")),
    )(rel_bias, q, kc, vc)


def _cmp_sample_kernel(q_ref, rowtbl_ref, kc_ref, vc_ref, oc_ref, memb_ref, *, qpos0, nbs, k_eff):
    ncp = kc_ref.shape[2]
    mb = memb_ref.shape[2]
    rows = NSA_R * 8
    qi = lax.bitwise_and(lax.broadcasted_iota(I32, (rows, ncp), 0), 7)
    d = qpos0 + qi - (lax.broadcasted_iota(I32, (rows, ncp), 1) * CMP_STRIDE + CMP_BLOCK - 1)
    ok = d >= 0
    jb = lax.broadcasted_iota(I32, (ncp, mb), 1) * SEL_BLOCK
    ci = lax.broadcasted_iota(I32, (ncp, mb), 0) * CMP_STRIDE
    ov = ((ci < jb + SEL_BLOCK) & (ci + CMP_BLOCK > jb)).astype(BF16)
    jl = lax.broadcasted_iota(I32, (8, mb), 1)
    cur = lax.shift_right_logical(qpos0 + lax.broadcasted_iota(I32, (8, mb), 0), 6)
    valid = (jl <= cur) & (jl < nbs)
    forced = valid & ((jl == 0) | (jl >= cur - 1))
    for g in range(NSA_GROUPS):
        s = _dot_nt(q_ref[0, g * rows:(g + 1) * rows], kc_ref[0, g])
        tbl = rowtbl_ref[g * rows:(g + 1) * rows]
        sm = jnp.where(ok, s + _t5_bias(d, lambda b: tbl[:, b:b + 1]), NEG_INF)
        e = jnp.exp(sm - jnp.max(sm, -1, keepdims=True))
        p = e / jnp.sum(e, -1, keepdims=True) * ok.astype(F32)
        oc_ref[0, g * rows:(g + 1) * rows] = _dot(p.astype(BF16), vc_ref[0, g])
        psum = p[0:8]
        for r in range(1, NSA_R):
            psum = psum + p[8 * r:8 * r + 8]
        imp = None
        for part in _split3(psum):
            t = _dot(part, ov)
            imp = t if imp is None else imp + t
        score = jnp.where(valid, imp + jnp.where(forced, FORCED_BONUS, 0.0), -1.0)
        score = jnp.where(jl < nbs, score, -2.0)
        rank = jnp.zeros((8, mb), I32)
        for ii in range(nbs):
            col = score[:, ii:ii + 1]
            ahead = (col > score) | ((col == score) & (jl > ii))
            rank = rank + ahead.astype(I32)
        memb_ref[0, g * 8:(g + 1) * 8] = ((rank < k_eff) & (score >= 0.0)).astype(F32)


def _cmp_sample(qrows, rowtbl, kc, vc, qpos0, nbs, mb):
    b = qrows.shape[0]
    ncp = kc.shape[2]
    nrows = NSA_GROUPS * NSA_R * 8
    c_spec = pl.BlockSpec((1, NSA_GROUPS, ncp, HEAD_DIM), lambda bi: (bi, 0, 0, 0))
    return pl.pallas_call(
        functools.partial(_cmp_sample_kernel, qpos0=qpos0, nbs=nbs, k_eff=min(SEL_TOPK, nbs)),
        name="cmp_sample",
        out_shape=(jax.ShapeDtypeStruct((b, nrows, HEAD_DIM), F32),
                   jax.ShapeDtypeStruct((b, NSA_GROUPS * 8, mb), F32)),
        grid=(b,),
        in_specs=[pl.BlockSpec((1, nrows, HEAD_DIM), lambda bi: (bi, 0, 0)), _const_spec(rowtbl.shape),
                  c_spec, c_spec],
        out_specs=(pl.BlockSpec((1, nrows, HEAD_DIM), lambda bi: (bi, 0, 0)),
                   pl.BlockSpec((1, NSA_GROUPS * 8, mb), lambda bi: (bi, 0, 0))),
        compiler_params=_params(("parallel",)),
    )(qrows, rowtbl, kc, vc)


def _decode_kernel(pt_ref, *refs, mode, npg, n_pages, pw, kpos0, qpos0, feat_major):
    q_ref, rowq_ref = refs[0], refs[1]
    k = 2
    if mode == "fox":
        cq_ref, aux_ref, auxtail_ref = refs[k:k + 3]
        k += 3
    else:
        rowtbl_ref = refs[k]
        k += 1
        if mode == "sel":
            aux_ref, auxtail_ref = refs[k:k + 2]
            k += 2
    pages = refs[k:k + npg]
    tail_ref, o_ref, m_ref, l_ref, acc_ref = refs[k + npg:k + npg + 5]
    step = pl.program_id(1)
    nr = q_ref.shape[1]
    q = q_ref[0]
    rowpos = qpos0 + rowq_ref[:, 0:1]
    key_shift = 3 if mode == "diffx" else 0

    @pl.when(step == 0)
    def _():
        _init_state(m_ref, l_ref, acc_ref)
        if mode == "diffx":
            far_ref = refs[k + npg + 5]
            col = lax.broadcasted_iota(I32, far_ref.shape, 1)
            far_ref[...] = jnp.where(lax.bitwise_and(col, 7) == rowq_ref[:, 1:2],
                                     rowtbl_ref[:, N_BUCKETS - 1:N_BUCKETS], NEG_INF)

    def process(page_list, kp0, aux, is_tail):
        kbs = [pg.astype(BF16) for pg in page_list]
        if feat_major:
            half = kbs[0].shape[0] // 2
            ss = [_dot(q, kb[:half]) for kb in kbs]
            widths = [kb.shape[1] for kb in kbs]
        else:
            ss = [_dot_nt(q, kb) for kb in kbs]
            widths = [kb.shape[0] for kb in kbs]
        s = ss[0] if len(ss) == 1 else jnp.concatenate(ss, 1)
        n = s.shape[1]

        def distance():
            col = lax.broadcasted_iota(I32, (nr, n), 1)
            return col, rowpos - (kp0 + lax.shift_right_logical(col, key_shift))

        def biased(sv):
            col, d = distance()
            tbl = rowtbl_ref[...]
            bad = d < 0
            if mode == "win":
                bad = bad | (d >= WINDOW)
            if mode == "sel":
                bad = bad | (aux < 0.5 * NEG_INF)
            if mode == "diffx":
                bad = bad | (lax.bitwise_and(col, 7) != rowq_ref[:, 1:2])
            return jnp.where(bad, NEG_INF, sv + _t5_bias(d, lambda b: tbl[:, b:b + 1]))

        if mode == "fox":
            s = s + (cq_ref[0] - jnp.concatenate([aux] * (nr // FOX_HEADS), 0))
            if is_tail:
                s = jnp.where(distance()[1] < 0, NEG_INF, s)
        elif is_tail or mode == "win":
            s = biased(s)
        else:
            near = qpos0 - (kp0 + (n >> key_shift) - 1) < T5_FAR
            far_bias = refs[k + npg + 5][...] if mode == "diffx" else aux
            s = lax.cond(near, lambda: biased(s), lambda: s + far_bias)

        def accum(p):
            out = None
            off = 0
            for kb, w in zip(kbs, widths):
                pp = p[:, off:off + w]
                if mode == "diffx":
                    pp = pltpu.roll(pp, 4, 1)
                pp = pp.astype(BF16)
                t = _dot_nt(pp, kb[half:]) if feat_major else _dot(pp, kb)
                out = t if out is None else out + t
                off += w
            return out

        _online_update(s, accum, m_ref, l_ref, acc_ref)

    keys_pp = pw >> key_shift
    process([pg[0] for pg in pages], kpos0 + step * (npg * keys_pp),
            aux_ref[0] if mode in ("fox", "sel") else None, False)

    @pl.when(step == pl.num_programs(1) - 1)
    def _():
        process([tail_ref[0]], qpos0, auxtail_ref[0] if mode in ("fox", "sel") else None, True)
        o_ref[0] = acc_ref[...] / l_ref[...]


def _decode(mode, qm, rowq, pool, pt_flat, tail, n_pages, npg, kpos0, qpos0, feat_major, **kw):
    b, nr, _ = qm.shape
    npg = math.gcd(npg, n_pages)
    n_steps = n_pages // npg
    pshape = pool.shape[1:]
    pw = pshape[1] if feat_major else pshape[0]
    accw = pshape[0] // 2 if feat_major else pshape[1]
    in_specs = [pl.BlockSpec((1,) + qm.shape[1:], lambda bi, s, pt: (bi, 0, 0)),
                pl.BlockSpec((nr, 2), lambda bi, s, pt: (0, 0))]
    args = [qm, rowq]
    if mode == "fox":
        in_specs.append(pl.BlockSpec((1, nr, 1), lambda bi, s, pt: (bi, 0, 0)))
        args.append(kw["cq"])
    else:
        in_specs.append(pl.BlockSpec((nr, N_BUCKETS), lambda bi, s, pt: (0, 0)))
        args.append(kw["rowtbl"])
    if mode in ("fox", "sel"):
        aux = kw["aux"]
        in_specs += [pl.BlockSpec((1, aux.shape[1], npg * pw), lambda bi, s, pt: (bi, 0, s)),
                     pl.BlockSpec((1, aux.shape[1], PAGE), lambda bi, s, pt: (bi, 0, (n_pages * pw) // PAGE))]
        args += [aux, aux]
    in_specs += [pl.BlockSpec((1,) + pshape, (lambda bi, s, pt, p=p: (pt[bi * n_pages + s * npg + p], 0, 0)))
                 for p in range(npg)]
    args += [pool] * npg
    in_specs.append(pl.BlockSpec((1,) + tail.shape[1:], lambda bi, s, pt: (bi, 0, 0)))
    args.append(tail)
    scratch = [pltpu.VMEM((nr, 1), F32), pltpu.VMEM((nr, 1), F32), pltpu.VMEM((nr, accw), F32)]
    if mode == "diffx":
        scratch.append(pltpu.VMEM((nr, npg * pw), F32))
    grid_spec = pltpu.PrefetchScalarGridSpec(
        num_scalar_prefetch=1, grid=(b, n_steps), in_specs=in_specs,
        out_specs=pl.BlockSpec((1, nr, accw), lambda bi, s, pt: (bi, 0, 0)),
        scratch_shapes=scratch)
    return pl.pallas_call(
        functools.partial(_decode_kernel, mode=mode, npg=npg, n_pages=n_pages, pw=pw, kpos0=kpos0, qpos0=qpos0,
                          feat_major=feat_major),
        name="decode_" + mode,
        out_shape=jax.ShapeDtypeStruct((b, nr, accw), F32), grid_spec=grid_spec,
        compiler_params=_params(("parallel", "arbitrary")),
    )(pt_flat, *args)


def _nsa_combine_kernel(oc_ref, os_ref, ow_ref, gate_ref, o_ref):
    gt = jax.nn.sigmoid(gate_ref[...])
    o = gt[:, 0:1] * oc_ref[...] + gt[:, 1:2] * os_ref[...] + gt[:, 2:3] * ow_ref[...]
    o_ref[...] = o.astype(o_ref.dtype)


def _nsa_combine(oc, os_, ow, gate):
    return pl.pallas_call(
        _nsa_combine_kernel, name="nsa_combine",
        out_shape=jax.ShapeDtypeStruct(oc.shape, BF16),
    )(oc, os_, ow, gate)


def _cumsum_kernel(pt_ref, bf_ref, cin_ref, *refs, npg, apply_logsig):
    pages = refs[:npg]
    lf_ref, c_ref, carry_ref = refs[npg:]

    @pl.when(pl.program_id(1) == 0)
    def _():
        carry_ref[...] = cin_ref[0]

    tri = (lax.broadcasted_iota(I32, (PAGE, PAGE), 0) <= lax.broadcasted_iota(I32, (PAGE, PAGE), 1)).astype(BF16)
    car = carry_ref[...]
    h = pages[0].shape[1]
    group = max(1, PAGE // h)
    for p0 in range(0, npg, group):
        xs = []
        for p in range(p0, min(p0 + group, npg)):
            x = pages[p][0]
            if apply_logsig:
                z = x + bf_ref[...]
                x = jnp.minimum(z, 0.0) - jnp.log1p(jnp.exp(-jnp.abs(z)))
            lf_ref[0, :, p * PAGE:(p + 1) * PAGE] = x
            xs.append(x)
        xcat = xs[0] if len(xs) == 1 else jnp.concatenate(xs, 0)
        loc = None
        for part in _split3(xcat):
            t = _dot(part, tri)
            loc = t if loc is None else loc + t
        for u, p in enumerate(range(p0, min(p0 + group, npg))):
            loc_p = loc[u * h:(u + 1) * h]
            c_ref[0, :, p * PAGE:(p + 1) * PAGE] = loc_p + car
            car = car + loc_p[:, PAGE - 1:PAGE]
    carry_ref[...] = car


def _cumsum_pages(pool, pt_flat, b, n_pages, b_f, carry_in, apply_logsig, npg=32):
    h = pool.shape[1]
    npg = math.gcd(npg, n_pages)
    in_specs = [pl.BlockSpec((h, 1), lambda bi, s, pt: (0, 0)),
                pl.BlockSpec((1, h, 1), lambda bi, s, pt: (bi, 0, 0))]
    in_specs += [pl.BlockSpec((1, h, PAGE), (lambda bi, s, pt, p=p: (pt[bi * n_pages + s * npg + p], 0, 0)))
                 for p in range(npg)]
    o_spec = pl.BlockSpec((1, h, npg * PAGE), lambda bi, s, pt: (bi, 0, s))
    out = jax.ShapeDtypeStruct((b, h, n_pages * PAGE), F32)
    grid_spec = pltpu.PrefetchScalarGridSpec(
        num_scalar_prefetch=1, grid=(b, n_pages // npg), in_specs=in_specs, out_specs=(o_spec, o_spec),
        scratch_shapes=[pltpu.VMEM((h, 1), F32)])
    return pl.pallas_call(
        functools.partial(_cumsum_kernel, npg=npg, apply_logsig=apply_logsig), name="cumsum_pages",
        out_shape=(out, out), grid_spec=grid_spec,
        compiler_params=_params(("parallel", "arbitrary")),
    )(pt_flat, b_f.reshape(h, 1), carry_in, *([pool] * npg))


def _router_kernel(x_ref, w_ref, o_ref):
    x = x_ref[...]
    w = w_ref[...]
    xh = x.astype(BF16)
    xl = (x - xh.astype(F32)).astype(BF16)
    wh = w.astype(BF16)
    wl = (w - wh.astype(F32)).astype(BF16)
    logits = _dot(xh, wh) + _dot(xl, wh) + _dot(xh, wl)
    lane = lax.broadcasted_iota(I32, logits.shape, 1)
    big = logits.shape[1]
    lg = jnp.where(lane < N_EXPERTS, logits, -jnp.inf)
    m1 = jnp.max(lg, -1, keepdims=True)
    i1 = jnp.min(jnp.where(lg == m1, lane, big), -1, keepdims=True)
    lg2 = jnp.where(lane == i1, -jnp.inf, lg)
    m2 = jnp.max(lg2, -1, keepdims=True)
    i2 = jnp.min(jnp.where(lg2 == m2, lane, big), -1, keepdims=True)
    e2 = jnp.exp(m2 - m1)
    g1 = 1.0 / (1.0 + e2)
    g2 = e2 / (1.0 + e2)
    out = jnp.where(lane == 0, i1.astype(F32), jnp.where(lane == 1, i2.astype(F32),
                    jnp.where(lane == 2, g1, jnp.where(lane == 3, g2, 0.0))))
    o_ref[...] = out


def _router(x, w_router):
    n, dm = x.shape
    tm = _row_tile(n)
    wp = jnp.zeros((dm, 128), F32).at[:, :N_EXPERTS].set(w_router)
    return pl.pallas_call(
        _router_kernel, name="router",
        out_shape=jax.ShapeDtypeStruct((n, 128), F32),
        grid=(n // tm,),
        in_specs=[pl.BlockSpec((tm, dm), lambda i: (i, 0)), _const_spec((dm, 128))],
        out_specs=pl.BlockSpec((tm, 128), lambda i: (i, 0)),
        compiler_params=_params(("parallel",)),
    )(x, wp)


def _row_copy(src_hbm, src_row, dst_ref, r, sem):
    return pltpu.make_async_copy(src_hbm.at[pl.ds(src_row, 1)], dst_ref.at[pl.ds(r, 1)], sem)


def _gather_start(idx_ref, n_rows, src_hbm, dst_ref, sem):
    for r in range(n_rows):
        _row_copy(src_hbm, idx_ref[0, 0, r], dst_ref, r, sem).start()


def _gather_wait(n_rows, src_hbm, dst_ref, sem):
    for r in range(n_rows):
        _row_copy(src_hbm, 0, dst_ref, r, sem).wait()


def _expert_kernel(be_ref, nb_ref, cur_ref, nxt_ref, x_hbm, wg_ref, wu_ref, wd_ref, o_ref, xbuf, sem):
    i = pl.program_id(0)
    n_used = nb_ref[0]
    slot = lax.bitwise_and(i, 1)

    @pl.when((i == 0) & (n_used > 0))
    def _():
        _gather_start(cur_ref, MOE_BLOCK, x_hbm, xbuf.at[0], sem.at[0])

    @pl.when(i + 1 < n_used)
    def _():
        _gather_start(nxt_ref, MOE_BLOCK, x_hbm, xbuf.at[1 - slot], sem.at[1 - slot])

    @pl.when(i < n_used)
    def _():
        _gather_wait(MOE_BLOCK, x_hbm, xbuf.at[slot], sem.at[slot])
        xb = xbuf[slot].astype(BF16)
        h = jax.nn.silu(_dot(xb, wg_ref[0])) * _dot(xb, wu_ref[0])
        o_ref[...] = _dot(h.astype(BF16), wd_ref[0])

    @pl.when(i >= n_used)
    def _():
        o_ref[...] = jnp.zeros(o_ref.shape, o_ref.dtype)


def _experts(x, src_rows, block_e, n_used, wg, wu, wd):
    nb = src_rows.shape[0]
    src_rows = src_rows.reshape(nb, 1, MOE_BLOCK)
    dm = x.shape[1]
    dff = wg.shape[2]
    grid_spec = pltpu.PrefetchScalarGridSpec(
        num_scalar_prefetch=2, grid=(nb,),
        in_specs=[pl.BlockSpec((1, 1, MOE_BLOCK), lambda i, be, nu: (i, 0, 0), memory_space=pltpu.SMEM),
                  pl.BlockSpec((1, 1, MOE_BLOCK), lambda i, be, nu: (jnp.minimum(i + 1, nb - 1), 0, 0),
                               memory_space=pltpu.SMEM),
                  pl.BlockSpec(memory_space=pl.ANY),
                  pl.BlockSpec((1, dm, dff), lambda i, be, nu: (be[i], 0, 0)),
                  pl.BlockSpec((1, dm, dff), lambda i, be, nu: (be[i], 0, 0)),
                  pl.BlockSpec((1, dff, dm), lambda i, be, nu: (be[i], 0, 0))],
        out_specs=pl.BlockSpec((MOE_BLOCK, dm), lambda i, be, nu: (i, 0)),
        scratch_shapes=[pltpu.VMEM((2, MOE_BLOCK, dm), F32), pltpu.SemaphoreType.DMA((2,))])
    return pl.pallas_call(
        _expert_kernel, name="experts",
        out_shape=jax.ShapeDtypeStruct((nb * MOE_BLOCK, dm), F32), grid_spec=grid_spec,
        compiler_params=_params(("arbitrary",)),
    )(block_e, n_used, src_rows, src_rows, x, wg, wu, wd)


def _combine_ln_kernel(cur_ref, nxt_ref, y_hbm, x_ref, route_ref, g_ref, b_ref, o_ref, ybuf, sem):
    i = pl.program_id(0)
    tm = x_ref.shape[0]
    slot = lax.bitwise_and(i, 1)

    @pl.when(i == 0)
    def _():
        _gather_start(cur_ref, TOP_K * tm, y_hbm, ybuf.at[0], sem.at[0])

    @pl.when(i + 1 < pl.num_programs(0))
    def _():
        _gather_start(nxt_ref, TOP_K * tm, y_hbm, ybuf.at[1 - slot], sem.at[1 - slot])

    _gather_wait(TOP_K * tm, y_hbm, ybuf.at[slot], sem.at[slot])
    gate = route_ref[...]
    f = gate[:, 2:3] * ybuf[slot, 0:tm] + gate[:, 3:4] * ybuf[slot, tm:2 * tm]
    o_ref[...] = _layer_norm(ALPHA * x_ref[...] + f, g_ref[...], b_ref[...])


def _combine_ln(x, route, y_sorted, pos, g, b):
    n, dm = x.shape
    tm = pos.shape[1] // TOP_K
    nt = n // tm
    pos = pos.reshape(nt, 1, TOP_K * tm)
    return pl.pallas_call(
        _combine_ln_kernel, name="combine_ln",
        out_shape=jax.ShapeDtypeStruct((n, dm), F32),
        grid=(nt,),
        in_specs=[pl.BlockSpec((1, 1, TOP_K * tm), lambda i: (i, 0, 0), memory_space=pltpu.SMEM),
                  pl.BlockSpec((1, 1, TOP_K * tm), lambda i: (jnp.minimum(i + 1, nt - 1), 0, 0),
                               memory_space=pltpu.SMEM),
                  pl.BlockSpec(memory_space=pl.ANY),
                  pl.BlockSpec((tm, dm), lambda i: (i, 0)), pl.BlockSpec((tm, 128), lambda i: (i, 0)),
                  _const_spec((1, dm)), _const_spec((1, dm))],
        out_specs=pl.BlockSpec((tm, dm), lambda i: (i, 0)),
        scratch_shapes=[pltpu.VMEM((2, TOP_K * tm, dm), F32), pltpu.SemaphoreType.DMA((2,))],
        compiler_params=_params(("arbitrary",)),
    )(pos, pos, y_sorted, x, route, g.reshape(1, dm), b.reshape(1, dm))


def _moe_ln(x, w_router, wg, wu, wd, g, b):
    n, dm = x.shape
    route = _router(x, w_router)
    top_e = route[:, 0:2].astype(I32)
    n_assign = n * TOP_K
    flat_e = top_e.reshape(-1)
    order = jnp.argsort(flat_e)
    e_sorted = flat_e[order]
    counts = jnp.bincount(flat_e, length=N_EXPERTS)
    padded = (counts + MOE_BLOCK - 1) // MOE_BLOCK * MOE_BLOCK
    start = jnp.cumsum(counts) - counts
    pad_end = jnp.cumsum(padded)
    pad_start = pad_end - padded
    dest = (pad_start[e_sorted] + jnp.arange(n_assign) - start[e_sorted]).astype(I32)
    n_blocks = -(-n_assign // MOE_BLOCK) + N_EXPERTS
    block_e = jnp.minimum(jnp.searchsorted(pad_end, jnp.arange(n_blocks) * MOE_BLOCK, side="right"),
                          N_EXPERTS - 1).astype(I32)
    n_used = (pad_end[-1] // MOE_BLOCK).astype(I32).reshape(1)
    src_rows = jnp.zeros((n_blocks * MOE_BLOCK,), I32).at[dest].set((order // TOP_K).astype(I32))
    y_sorted = _experts(x, src_rows.reshape(n_blocks, MOE_BLOCK), block_e, n_used,
                        wg.astype(BF16), wu.astype(BF16), wd.astype(BF16))
    tm = min(_row_tile(n), MOE_BLOCK)
    pos_of = jnp.zeros((n_assign,), I32).at[order].set(dest).reshape(n // tm, tm, TOP_K)
    pos = pos_of.transpose(0, 2, 1).reshape(n // tm, TOP_K * tm)
    return _combine_ln(x, route, y_sorted, pos, g, b)


C0_QDA, C0_DIFF, C0_QNS, C0_CMP, C0_SEL, C0_WIN, C0_GATE, C0 = 0, 512, 1536, 2048, 2304, 2560, 2816, 2944
C1_Q, C1_KV, C1_F, C1 = 0, 1024, 1536, 1664
FOX_DK = 128


def _l0_weight(w_in):
    k_off, v_off = 512, 1024
    cols = list(range(0, 512))
    for h in range(DA_HEADS):
        cols += list(range(k_off + 128 * h, k_off + 128 * (h + 1)))
        cols += list(range(v_off + 128 * h, v_off + 128 * (h + 1)))
    cols += list(range(1536, 2840))
    w = w_in[:, np.asarray(cols)]
    return jnp.pad(w, ((0, 0), (0, C0 - w.shape[1]))).astype(BF16)


def _pad_rows(x, rows):
    return jnp.pad(x, ((0, 0), (0, rows - x.shape[1])) + ((0, 0),) * (x.ndim - 2))


def _fox_augment(q, k, c):
    b, t = c.shape[:2]

    def split3(x):
        def top(v):
            return lax.bitcast_convert_type(lax.bitcast_convert_type(v, jnp.uint32) & jnp.uint32(0xFFFF0000), F32)
        hi = top(x)
        mid = top(x - hi)
        lo = top(x - hi - mid)
        return jnp.stack([hi.astype(BF16), mid.astype(BF16), lo.astype(BF16)], -1)

    c3 = split3(c)
    nc3 = split3(-c).reshape(b, t, FOX_KV_HEADS, FOX_R * 3)
    slot = jnp.asarray(np.repeat(np.eye(FOX_R), 3, axis=1), BF16)
    slot = jnp.broadcast_to(jnp.tile(slot, (FOX_KV_HEADS, 1))[None, None], (b, t, FOX_HEADS, 3 * FOX_R))
    pad_q = jnp.zeros((b, t, FOX_HEADS, FOX_DK - HEAD_DIM - 3 - 3 * FOX_R), BF16)
    qa = jnp.concatenate([q * jnp.asarray(SCALE, BF16), c3, slot, pad_q], -1)
    ones = jnp.ones((b, t, FOX_KV_HEADS, 3), BF16)
    pad_k = jnp.zeros((b, t, FOX_KV_HEADS, FOX_DK - HEAD_DIM - 3 - 3 * FOX_R), BF16)
    ka = jnp.concatenate([k, ones, nc3, pad_k], -1)
    return qa.transpose(0, 2, 3, 1), ka.transpose(0, 2, 1, 3)


def kernel(x_prompt, x_sample, cache_diff_kv, cache_nsa_cmp, cache_nsa_sel, state_nsa_win, cache_fox_kv,
           cache_fox_logf, page_table, rel_bias, l0_w_in, l0_w_out, da_lambda, da_norm, nsa_cmp_pos,
           nsa_cmp_w1, nsa_cmp_w2, ffn_w_gate, ffn_w_up, ffn_w_down, l1_w_in, fox_b_f, l1_w_out,
           moe_router, moe_w_gate, moe_w_up, moe_w_down, ln_g, ln_b):
    b, t, dm = x_prompt.shape
    bs, ts, _ = x_sample.shape
    n_pages = page_table.shape[1]
    n_phys = cache_diff_kv.shape[0]
    past = n_pages * PAGE
    np_, ns = b * t, bs * ts
    pt_flat = page_table.reshape(-1).astype(I32)
    x0 = jnp.concatenate([x_prompt.reshape(np_, dm), x_sample.reshape(ns, dm)], 0)
    g2 = NSA_GROUPS

    p0, p0b = _proj(x0, _l0_weight(l0_w_in))
    bt_c = _bias_tiles(rel_bias, 4, False, 0, DA_HEADS + NSA_HEADS)
    bt_w = _bias_tiles(rel_bias, 6, True, DA_HEADS, NSA_HEADS)
    rel_t = rel_bias.T

    vt_da = p0b[:np_, C0_DIFF:C0_DIFF + 1024].reshape(b, t, DA_HEADS, 4 * HEAD_DIM)[..., 2 * HEAD_DIM:]
    qt_da = p0b[:np_, C0_QDA:C0_QDA + 512].reshape(b, t, DA_HEADS, 2 * HEAD_DIM).transpose(0, 2, 3, 1)
    o_da_p = _diff_prompt(p0b, qt_da, vt_da.transpose(0, 2, 3, 1), rel_bias, da_lambda, da_norm, bt_c, b, t,
                          C0_DIFF // 128)

    q_ns_p = p0b[:np_, C0_QNS:C0_QNS + 512].reshape(b, t, NSA_HEADS, HEAD_DIM).transpose(0, 2, 1, 3)

    def kv_major(col):
        kv = p0b[:np_, col:col + 256].reshape(b, t, 2, g2, HEAD_DIM)
        return kv[:, :, 0].transpose(0, 2, 1, 3), kv[:, :, 1].transpose(0, 2, 3, 1)

    wbig = _cmp_weight(nsa_cmp_w1)
    part_p = _rowmm(p0[:np_, C0_CMP:C0_CMP + 256].reshape(np_ // CMP_STRIDE, CMP_STRIDE * 256), wbig)
    kc_p, vc_p = _cmp_finish(part_p.reshape(b, t // CMP_STRIDE, 1024), nsa_cmp_pos, nsa_cmp_w1, nsa_cmp_w2)
    oc_p, memb_p = _cmp_prompt(q_ns_p, kc_p, vc_p, rel_bias)
    ks_p, vst_p = kv_major(C0_SEL)
    qt_ns_p = q_ns_p.transpose(0, 1, 3, 2)
    os_p = _gqa_prompt("sel", qt_ns_p, ks_p, vst_p, rel_bias=rel_bias, bt=bt_c, memb=memb_p)
    kw_p, vwt_p = kv_major(C0_WIN)
    gate_p = p0[:np_, C0_GATE:C0_GATE + 3 * NSA_HEADS].reshape(b, t, g2, 3 * NSA_R).transpose(0, 2, 3, 1)
    o_ns_p = _gqa_prompt("win", qt_ns_p, kw_p, vwt_p, bt=bt_w, oc=oc_p.transpose(0, 1, 3, 2), os=os_p, gate=gate_p)

    qpos0 = past
    qd = p0b[np_:, C0_QDA:C0_QDA + 512].reshape(bs, ts, DA_HEADS, 2, HEAD_DIM).transpose(0, 3, 2, 1, 4)
    qm_d = jnp.einsum("bwhqd,ws->bwhqsd", qd, jnp.eye(2, dtype=BF16)) * jnp.asarray(SCALE, BF16)
    qm_d = qm_d.reshape(bs, 2 * DA_HEADS * ts, 2 * HEAD_DIM)
    rows_d = [(w, h, q) for w in range(2) for h in range(DA_HEADS) for q in range(ts)]
    rowq_d = jnp.asarray(np.array([[r[2], r[1]] for r in rows_d], np.int32))
    rowtbl_d = rel_t[np.array([r[1] for r in rows_d])]

    def key_rows(x):
        lead = x.shape[:-2]
        x = x.reshape(lead + (PAGE, DA_HEADS, 2, 2 * HEAD_DIM))
        return jnp.swapaxes(x, -3, -2).reshape(lead + (PAGE * 2 * DA_HEADS, 2 * HEAD_DIM))

    tail_d = key_rows(_pad_rows(p0[np_:, C0_DIFF:C0_DIFF + 1024].reshape(bs, ts, 1024), PAGE))
    acc_d = _decode("diffx", qm_d, rowq_d, key_rows(cache_diff_kv.reshape(n_phys, PAGE, 1024)), pt_flat, tail_d,
                    n_pages, DECODE_PAGES, 0, qpos0, False, rowtbl=rowtbl_d)
    o12 = acc_d.reshape(bs, 2, DA_HEADS * ts, 2 * HEAD_DIM)
    o_da_s = _diff_finish(o12[:, 0].reshape(bs * DA_HEADS * ts, 128), o12[:, 1].reshape(bs * DA_HEADS * ts, 128),
                          da_lambda, da_norm)
    o_da_s = o_da_s.reshape(bs, DA_HEADS, ts, 128).transpose(0, 2, 1, 3).reshape(ns, 512)

    part_s = _paged_cmp(cache_nsa_cmp.transpose(0, 2, 3, 4, 1).reshape(n_phys, 4 * HEAD_DIM, PAGE), pt_flat, wbig)
    l_tot = past + ts
    if l_tot // CMP_STRIDE > past // CMP_STRIDE:
        raise NotImplementedError("new rows completing a compression chunk")
    kc_s, vc_s = _cmp_finish(part_s.reshape(bs, past // CMP_STRIDE, 1024), nsa_cmp_pos, nsa_cmp_w1, nsa_cmp_w2)
    nbs_s = -(-l_tot // SEL_BLOCK)
    mb_s = -(-nbs_s // 128) * 128
    qn = p0b[np_:, C0_QNS:C0_QNS + 512].reshape(bs, ts, g2, NSA_R, HEAD_DIM) * jnp.asarray(SCALE, BF16)
    q_cmp = _pad_rows(qn.reshape(bs, ts, NSA_HEADS, HEAD_DIM), 8).transpose(0, 2, 1, 3)
    q_cmp = q_cmp.reshape(bs, NSA_HEADS * 8, HEAD_DIM)
    rowtbl_c = rel_t[DA_HEADS + np.repeat(np.arange(NSA_HEADS), 8)]
    oc_s, memb_s = _cmp_sample(q_cmp, rowtbl_c, kc_s, vc_s, qpos0, nbs_s, mb_s)
    oc_s = oc_s.reshape(bs, NSA_HEADS, 8, HEAD_DIM)[:, :, :ts].transpose(0, 2, 1, 3)
    eye_g = jnp.eye(g2, dtype=BF16)
    qm_n = jnp.einsum("bqgrd,gs->bqgrsd", qn, eye_g).reshape(bs, ts * NSA_HEADS, g2 * HEAD_DIM)
    rows_n = [(q, g, r) for q in range(ts) for g in range(g2) for r in range(NSA_R)]
    rowq_n = jnp.asarray(np.array([[r[0], r[1] * NSA_R + r[2]] for r in rows_n], np.int32))
    rowtbl_n = rel_t[DA_HEADS + np.array([r[1] * NSA_R + r[2] for r in rows_n])]
    memb_rows = memb_s.reshape(bs, g2, 8, mb_s)[:, :, :ts, :nbs_s].transpose(0, 2, 1, 3)
    memb_rows = jnp.broadcast_to(memb_rows[:, :, :, None, :, None], (bs, ts, g2, NSA_R, nbs_s, SEL_BLOCK))
    memb_rows = memb_rows.reshape(bs, ts * NSA_HEADS, nbs_s * SEL_BLOCK)
    memb_rows = jnp.where(memb_rows > 0.5, rowtbl_n[None, :, N_BUCKETS - 1:N_BUCKETS], NEG_INF)
    memb_rows = jnp.pad(memb_rows, ((0, 0), (0, 0), (0, past + PAGE - nbs_s * SEL_BLOCK)), constant_values=NEG_INF)

    def feat_tail(col, width):
        return _pad_rows(p0[np_:, col:col + width].reshape(bs, ts, width), PAGE).transpose(0, 2, 1)

    pool_sel = cache_nsa_sel.transpose(0, 2, 3, 4, 1).reshape(n_phys, 4 * HEAD_DIM, PAGE)
    acc_sel = _decode("sel", qm_n, rowq_n, pool_sel, pt_flat, feat_tail(C0_SEL, 256),
                      n_pages, DECODE_PAGES, 0, qpos0, True, rowtbl=rowtbl_n, aux=memb_rows)
    w_buf = state_nsa_win.shape[1]
    pool_win = state_nsa_win.transpose(0, 2, 3, 4, 1).reshape(bs, 4 * HEAD_DIM, w_buf)
    acc_win = _decode("win", qm_n, rowq_n, pool_win, jnp.arange(bs, dtype=I32), feat_tail(C0_WIN, 256),
                      1, 1, qpos0 - w_buf, qpos0, True, rowtbl=rowtbl_n)

    def pick_group(acc):
        a = acc.reshape(bs, ts, g2, NSA_R, g2, HEAD_DIM)
        return jnp.stack([a[:, :, g, :, g] for g in range(g2)], 2)

    gate_s = p0[np_:, C0_GATE:C0_GATE + 3 * NSA_HEADS].reshape(ns * NSA_HEADS, 3)
    o_ns_s = _nsa_combine(oc_s.reshape(ns * NSA_HEADS, HEAD_DIM), pick_group(acc_sel).reshape(ns * NSA_HEADS, HEAD_DIM),
                          pick_group(acc_win).reshape(ns * NSA_HEADS, HEAD_DIM), gate_s).reshape(ns, 512)

    o_da = jnp.concatenate([o_da_p, o_da_s], 0)
    o_ns = jnp.concatenate([o_ns_p, o_ns_s], 0)
    x1 = _mix_ln([o_da, o_ns], l0_w_out.astype(BF16), x0, ln_g[0], ln_b[0])
    x2 = _ffn_ln(x1, ffn_w_gate.astype(BF16), ffn_w_up.astype(BF16), ffn_w_down.astype(BF16), ln_g[1], ln_b[1])

    w1p = jnp.pad(l1_w_in, ((0, 0), (0, C1 - l1_w_in.shape[1]))).astype(BF16)
    p1, p1b = _proj(x2, w1p)
    fl_p = p1[:np_, C1_F:C1_F + FOX_HEADS].reshape(np_ // PAGE, PAGE, FOX_HEADS).transpose(0, 2, 1)
    lft_p, ct_p = _cumsum_pages(fl_p, jnp.arange(np_ // PAGE, dtype=I32), b, t // PAGE, fox_b_f,
                                jnp.zeros((b, FOX_HEADS, 1), F32), True)
    q_fx = p1b[:np_, C1_Q:C1_Q + 1024].reshape(b, t, FOX_HEADS, HEAD_DIM)
    kv_fx = p1b[:np_, C1_KV:C1_KV + 512].reshape(b, t, 2, FOX_KV_HEADS, HEAD_DIM)
    qa_p, ka_p = _fox_augment(q_fx, kv_fx[:, :, 0], ct_p.transpose(0, 2, 1))
    o_fx_p = _gqa_prompt("fox", qa_p, ka_p, kv_fx[:, :, 1].transpose(0, 2, 3, 1))

    _, ct_past = _cumsum_pages(cache_fox_logf.transpose(0, 2, 1), pt_flat, bs, n_pages, fox_b_f,
                               jnp.zeros((bs, FOX_HEADS, 1), F32), False)
    f_new = _pad_rows(p1[np_:, C1_F:C1_F + FOX_HEADS].reshape(bs, ts, FOX_HEADS), PAGE).transpose(0, 2, 1)
    lft_s, ct_new = _cumsum_pages(f_new, jnp.arange(bs, dtype=I32), bs, 1, fox_b_f, ct_past[:, :, past - 1:past], True)
    ck_s = jnp.concatenate([ct_past, ct_new], 2)
    cq_s = ct_new[:, :, :ts].transpose(0, 2, 1).reshape(bs, ts * FOX_HEADS, 1)
    qf = p1b[np_:, C1_Q:C1_Q + 1024].reshape(bs, ts, FOX_KV_HEADS, FOX_R, HEAD_DIM) * jnp.asarray(SCALE, BF16)
    qm_f = jnp.einsum("bqgrd,gs->bqgrsd", qf, jnp.eye(FOX_KV_HEADS, dtype=BF16))
    qm_f = qm_f.reshape(bs, ts * FOX_HEADS, FOX_KV_HEADS * HEAD_DIM)
    rowq_f = jnp.asarray(np.stack([np.repeat(np.arange(ts), FOX_HEADS), np.tile(np.arange(FOX_HEADS), ts)],
                                  1).astype(np.int32))
    tail_f = _pad_rows(p1[np_:, C1_KV:C1_KV + 512].reshape(bs, ts, 512), PAGE).transpose(0, 2, 1)
    pool_fox = cache_fox_kv.transpose(0, 2, 3, 4, 1).reshape(n_phys, 2 * FOX_KV_HEADS * HEAD_DIM, PAGE)
    acc_f = _decode("fox", qm_f, rowq_f, pool_fox, pt_flat, tail_f, n_pages, DECODE_PAGES, 0, qpos0, True, cq=cq_s, aux=ck_s)
    acc_f = acc_f.reshape(bs, ts, FOX_KV_HEADS, FOX_R, FOX_KV_HEADS, HEAD_DIM)
    o_fx_s = jnp.stack([acc_f[:, :, g, :, g] for g in range(FOX_KV_HEADS)], 2)
    o_fx = jnp.concatenate([o_fx_p, o_fx_s.reshape(ns, 1024).astype(BF16)], 0)
    x3 = _mix_ln([o_fx], l1_w_out.astype(BF16), x2, ln_g[2], ln_b[2])

    y = _moe_ln(x3, moe_router, moe_w_gate, moe_w_up, moe_w_down, ln_g[3], ln_b[3])

    diff_all = p0[:, C0_DIFF:C0_DIFF + 1024]
    cmp_all = p0[:, C0_CMP:C0_CMP + 256]
    sel_all = p0[:, C0_SEL:C0_SEL + 256]
    win_all = p0[:, C0_WIN:C0_WIN + 256]
    fkv_all = p1[:, C1_KV:C1_KV + 512]
    keep_p = min(WINDOW, t)
    win_p_out = win_all[:np_].reshape(b, t, 2, g2, HEAD_DIM)[:, t - keep_p:]
    full_win_s = jnp.concatenate([state_nsa_win, win_all[np_:].reshape(bs, ts, 2, g2, HEAD_DIM)], 1)
    keep_s = min(WINDOW, full_win_s.shape[1])
    return (y[:np_].reshape(b, t, dm), y[np_:].reshape(bs, ts, dm),
            diff_all[:np_].reshape(b, t, DA_HEADS, 4 * HEAD_DIM), diff_all[np_:].reshape(bs, ts, DA_HEADS, 4 * HEAD_DIM),
            cmp_all[:np_].reshape(b, t, 2, g2, HEAD_DIM), cmp_all[np_:].reshape(bs, ts, 2, g2, HEAD_DIM),
            sel_all[:np_].reshape(b, t, 2, g2, HEAD_DIM), sel_all[np_:].reshape(bs, ts, 2, g2, HEAD_DIM),
            win_p_out, full_win_s[:, full_win_s.shape[1] - keep_s:],
            fkv_all[:np_].reshape(b, t, 2, FOX_KV_HEADS, HEAD_DIM), fkv_all[np_:].reshape(bs, ts, 2, FOX_KV_HEADS, HEAD_DIM),
            lft_p.transpose(0, 2, 1), lft_s[:, :, :ts].transpose(0, 2, 1))
```

```python
import functools
import math

import numpy as np
import jax
import jax.numpy as jnp
from jax import lax
from jax.experimental import pallas as pl
from jax.experimental.pallas import tpu as pltpu

F32, BF16, I32 = jnp.float32, jnp.bfloat16, jnp.int32

HEAD_DIM = 64
DA_HEADS = 4
NSA_HEADS = 8
NSA_GROUPS = 2
NSA_R = NSA_HEADS // NSA_GROUPS
CMP_BLOCK = 32
CMP_STRIDE = 16
CMP_HIDDEN = 2 * HEAD_DIM
SEL_BLOCK = 64
SEL_TOPK = 16
WINDOW = 512
FOX_HEADS = 16
FOX_KV_HEADS = 4
FOX_R = FOX_HEADS // FOX_KV_HEADS
N_BUCKETS = 32
MAX_DISTANCE = 128
N_EXPERTS = 8
TOP_K = 2
MOE_BLOCK = 128
LN_EPS = 1e-5
RMS_EPS = 1e-5
DEPTH = 2
ALPHA = (2.0 * DEPTH) ** 0.25
DA_LAMBDA_INIT = 0.8 - 0.6 * math.exp(-0.3 * 0)
NEG_INF = -1e30
FORCED_BONUS = 1e4
SCALE = HEAD_DIM ** -0.5

PAGE = 128
BQ = 128
BK = 256
DECODE_PAGES = 16
FAR_GROUP = 8
VMEM_LIMIT = 56 * 1024 * 1024


def _t5_thresholds():
    d = np.arange(0, 4 * MAX_DISTANCE)
    df = np.maximum(d, 1).astype(np.float32)
    max_exact = N_BUCKETS // 2
    large = max_exact + (np.log(df / np.float32(max_exact)) / np.float32(math.log(MAX_DISTANCE / max_exact))
                         * np.float32(N_BUCKETS - max_exact)).astype(np.int32)
    bucket = np.where(d < max_exact, d, np.minimum(large, N_BUCKETS - 1))
    return [int(np.argmax(bucket >= b)) for b in range(1, N_BUCKETS)]


T5_THR = _t5_thresholds()
T5_FAR = T5_THR[-1]


def _t5_bias(d, val):
    out = jnp.zeros(d.shape, F32) + val(0)
    for b in range(1, N_BUCKETS):
        out = jnp.where(d >= T5_THR[b - 1], val(b), out)
    return out


def _dot(a, b):
    return jnp.dot(a, b, preferred_element_type=F32)


def _dot_nt(a, b):
    return lax.dot_general(a, b, (((1,), (1,)), ((), ())), preferred_element_type=F32)


def _split3(x):
    hi = x.astype(BF16)
    r1 = x - hi.astype(F32)
    mid = r1.astype(BF16)
    lo = (r1 - mid.astype(F32)).astype(BF16)
    return hi, mid, lo


def _layer_norm(z, g, b):
    mu = jnp.mean(z, -1, keepdims=True)
    zc = z - mu
    var = jnp.mean(zc * zc, -1, keepdims=True)
    return zc * lax.rsqrt(var + LN_EPS) * g + b


def _online_update(s, accum, m_ref, l_ref, acc_ref):
    m_prev = m_ref[...]
    m_new = jnp.maximum(m_prev, jnp.max(s, -1, keepdims=True))
    alpha = jnp.exp(m_prev - m_new)
    p = jnp.exp(s - m_new)
    l_ref[...] = alpha * l_ref[...] + jnp.sum(p, -1, keepdims=True)
    acc_ref[...] = alpha * acc_ref[...] + accum(p)
    m_ref[...] = m_new


def _online_update_t(ss, vts, m_ref, l_ref, acc_ref):
    m_prev = m_ref[...]
    m_new = m_prev
    for s in ss:
        m_new = jnp.maximum(m_new, jnp.max(s, 0, keepdims=True))
    alpha = jnp.exp(m_prev - m_new)
    l_new = alpha * l_ref[...]
    acc_new = alpha * acc_ref[...]
    for s, vt in zip(ss, vts):
        p = jnp.exp(s - m_new)
        l_new = l_new + jnp.sum(p, 0, keepdims=True)
        acc_new = acc_new + _dot(vt, p.astype(BF16))
    l_ref[...] = l_new
    acc_ref[...] = acc_new
    m_ref[...] = m_new


def _init_state(m_ref, l_ref, acc_ref):
    m_ref[...] = jnp.full(m_ref.shape, NEG_INF, F32)
    l_ref[...] = jnp.zeros(l_ref.shape, F32)
    acc_ref[...] = jnp.zeros(acc_ref.shape, F32)


def _row_tile(n):
    for t in (512, 384, 256, 128, 64, 32, 16, 8):
        if n % t == 0:
            return t
    raise ValueError(f"row count {n} is not a multiple of 8")


def _params(sem):
    return pltpu.CompilerParams(dimension_semantics=sem, vmem_limit_bytes=VMEM_LIMIT)


def _const_spec(shape):
    nd = len(shape)
    return pl.BlockSpec(shape, lambda *a: (0,) * nd, pipeline_mode=pl.Buffered(1))


def _proj_kernel(x_ref, w_ref, o_ref, ob_ref):
    y = _dot(x_ref[...].astype(BF16), w_ref[...])
    o_ref[...] = y
    ob_ref[...] = y.astype(BF16)


def _proj(x, w):
    n, k = x.shape
    c = w.shape[1]
    tm = _row_tile(n)
    return pl.pallas_call(
        _proj_kernel, name="proj",
        out_shape=(jax.ShapeDtypeStruct((n, c), F32), jax.ShapeDtypeStruct((n, c), BF16)),
        grid=(n // tm,),
        in_specs=[pl.BlockSpec((tm, k), lambda i: (i, 0)), _const_spec((k, c))],
        out_specs=(pl.BlockSpec((tm, c), lambda i: (i, 0)), pl.BlockSpec((tm, c), lambda i: (i, 0))),
        compiler_params=_params(("parallel",)),
    )(x, w)


def _rowmm_kernel(x_ref, w_ref, o_ref):
    o_ref[...] = _dot(x_ref[...].astype(BF16), w_ref[...])


def _rowmm(x, w):
    n, k = x.shape
    c = w.shape[1]
    tm = _row_tile(n)
    tm = min(tm, 256)
    return pl.pallas_call(
        _rowmm_kernel, name="rowmm",
        out_shape=jax.ShapeDtypeStruct((n, c), F32),
        grid=(n // tm,),
        in_specs=[pl.BlockSpec((tm, k), lambda i: (i, 0)), _const_spec((k, c))],
        out_specs=pl.BlockSpec((tm, c), lambda i: (i, 0)),
        compiler_params=_params(("parallel",)),
    )(x, w)


def _mix_ln_kernel(*refs, n_in):
    a_refs = refs[:n_in]
    w_ref, x_ref, g_ref, b_ref, y_ref = refs[n_in:]
    acc = None
    off = 0
    for a in a_refs:
        k = a.shape[1]
        t = _dot(a[...], w_ref[off:off + k, :])
        acc = t if acc is None else acc + t
        off += k
    y_ref[...] = _layer_norm(ALPHA * x_ref[...] + acc, g_ref[...], b_ref[...])


def _mix_ln(a_list, w, x, g, b):
    n, dm = x.shape
    tm = _row_tile(n)
    in_specs = [pl.BlockSpec((tm, a.shape[1]), lambda i: (i, 0)) for a in a_list]
    in_specs += [_const_spec(w.shape), pl.BlockSpec((tm, dm), lambda i: (i, 0)),
                 _const_spec((1, dm)), _const_spec((1, dm))]
    return pl.pallas_call(
        functools.partial(_mix_ln_kernel, n_in=len(a_list)), name="mix_ln",
        out_shape=jax.ShapeDtypeStruct((n, dm), F32),
        grid=(n // tm,), in_specs=in_specs,
        out_specs=pl.BlockSpec((tm, dm), lambda i: (i, 0)),
        compiler_params=_params(("parallel",)),
    )(*a_list, w, x, g.reshape(1, dm), b.reshape(1, dm))


def _ffn_ln_kernel(x_ref, wg_ref, wu_ref, wd_ref, g_ref, b_ref, y_ref):
    x = x_ref[...]
    xb = x.astype(BF16)
    h = jax.nn.silu(_dot(xb, wg_ref[...])) * _dot(xb, wu_ref[...])
    f = _dot(h.astype(BF16), wd_ref[...])
    y_ref[...] = _layer_norm(ALPHA * x + f, g_ref[...], b_ref[...])


def _ffn_ln(x, wg, wu, wd, g, b):
    n, dm = x.shape
    tm = min(_row_tile(n), 384)
    return pl.pallas_call(
        _ffn_ln_kernel, name="ffn_ln",
        out_shape=jax.ShapeDtypeStruct((n, dm), F32),
        grid=(n // tm,),
        in_specs=[pl.BlockSpec((tm, dm), lambda i: (i, 0)), _const_spec(wg.shape), _const_spec(wu.shape),
                  _const_spec(wd.shape), _const_spec((1, dm)), _const_spec((1, dm))],
        out_specs=pl.BlockSpec((tm, dm), lambda i: (i, 0)),
        compiler_params=_params(("parallel",)),
    )(x, wg, wu, wd, g.reshape(1, dm), b.reshape(1, dm))


def _bias_tiles_kernel(rel_ref, o_ref, *, nt, window, head0):
    h = pl.program_id(0) + head0
    kk = lax.broadcasted_iota(I32, (BK, BQ), 0)
    qq = lax.broadcasted_iota(I32, (BK, BQ), 1)
    for t in range(nt):
        d = t * BQ + qq - kk
        bias = _t5_bias(d, lambda b: rel_ref[b, h])
        bad = d < 0
        if window:
            bad = bad | (d >= WINDOW)
        o_ref[0, t] = jnp.where(bad, NEG_INF, bias)


def _bias_tiles(rel_bias, nt, window, head0, nheads):
    return pl.pallas_call(
        functools.partial(_bias_tiles_kernel, nt=nt, window=window, head0=head0), name="bias_tiles",
        out_shape=jax.ShapeDtypeStruct((nheads, nt, BK, BQ), F32),
        grid=(nheads,),
        in_specs=[pl.BlockSpec(memory_space=pltpu.SMEM)],
        out_specs=pl.BlockSpec((1, nt, BK, BQ), lambda h: (h, 0, 0, 0)),
        compiler_params=_params(("parallel",)),
    )(rel_bias)


def _da_lambda(lp):
    return (jnp.exp(jnp.sum(lp[0:1] * lp[1:2], -1, keepdims=True))
            - jnp.exp(jnp.sum(lp[2:3] * lp[3:4], -1, keepdims=True)) + DA_LAMBDA_INIT)


def _to_rows(xt):
    eye = (lax.broadcasted_iota(I32, (BQ, BQ), 0) == lax.broadcasted_iota(I32, (BQ, BQ), 1)).astype(BF16)
    return _dot_nt(eye, xt)


def _flash_kernel(*refs, mode):
    if mode == "diff":
        rel_ref, lam_ref, nw_ref, bt_ref, q_ref, k_ref, vt_ref, o_ref, m_ref, l_ref, acc_ref = refs
    elif mode == "fox":
        q_ref, k_ref, vt_ref, o_ref, m_ref, l_ref, acc_ref = refs
    elif mode == "sel":
        rel_ref, bt_ref, q_ref, k_ref, vt_ref, memb_ref, o_ref, m_ref, l_ref, acc_ref = refs
    else:
        bt_ref, q_ref, k_ref, vt_ref, oc_ref, os_ref, gate_ref, o_ref, m_ref, l_ref, acc_ref = refs
    g = pl.program_id(1)
    i = pl.program_id(2)
    if mode == "diff":
        jd = i
        part_t0 = (0, 1, 0, 1)
    else:
        jd = lax.shift_right_logical(i, 1)
        t0 = lax.bitwise_and(i, 1)
    if mode == "diff":
        nr = 4
        q = q_ref[0, 0]
        row = lax.broadcasted_iota(I32, q.shape, 0)
        zero = jnp.zeros_like(q)
        qs = jnp.concatenate([jnp.where(row < HEAD_DIM, q, zero), jnp.where(row >= HEAD_DIM, q, zero)], 1)
        qs = qs * jnp.asarray(SCALE, BF16)
    else:
        nr = q_ref.shape[1]
        qs = jnp.concatenate([q_ref[0, r] for r in range(nr)], 1)
        if mode != "fox":
            qs = qs * jnp.asarray(SCALE, BF16)
    _init_state(m_ref, l_ref, acc_ref)

    def scores(j, back, diag):
        off = pl.multiple_of(j * BK, BK)
        if mode == "diff":
            kt = k_ref[pl.ds(off, BK), :]
        else:
            kt = k_ref[0, 0, pl.ds(off, BK), :]
        s = _dot(kt, qs)
        if mode == "fox":
            if diag:
                kk = lax.broadcasted_iota(I32, (BK, BQ), 0)
                qq = lax.broadcasted_iota(I32, (BK, BQ), 1)
                ok = (t0 * BQ + qq - kk) >= 0
                s = jnp.where(jnp.concatenate([ok] * nr, 1), s, NEG_INF)
        else:
            extra = None
            if mode == "sel":
                nbs = memb_ref.shape[2]
                kk = lax.broadcasted_iota(I32, (BK, nbs), 0)
                jb = lax.broadcasted_iota(I32, (BK, nbs), 1)
                e = (jb == j * (BK // SEL_BLOCK) + lax.shift_right_logical(kk, 6)).astype(BF16)
                extra = (_dot(e, memb_ref[0, 0]) - 1.0) * (-NEG_INF)
            parts = []
            for r in range(nr):
                if back is None:
                    br = rel_ref[N_BUCKETS - 1, g if mode == "diff" else DA_HEADS + g * nr + r]
                elif mode == "diff":
                    br = bt_ref[0, part_t0[r] + 2 * back]
                else:
                    br = bt_ref[r, t0 + 2 * back]
                if extra is not None:
                    br = br + extra
                parts.append(s[:, r * BQ:(r + 1) * BQ] + br)
            s = jnp.concatenate(parts, 1)
        return s, vt_ref[0, 0, :, pl.ds(off, BK)]

    def tiles(spec):
        ss, vts = zip(*[scores(*a) for a in spec])
        _online_update_t(ss, vts, m_ref, l_ref, acc_ref)

    if mode != "win":
        n_far = jd if mode == "fox" else jnp.maximum(jd - 1, 0)

        def far_body(jg, c):
            tiles([(FAR_GROUP * jg + u, None, False) for u in range(FAR_GROUP)])
            return c

        lax.fori_loop(0, lax.shift_right_logical(n_far, FAR_GROUP.bit_length() - 1), far_body, 0)
        base = lax.bitwise_and(n_far, -FAR_GROUP)
        width = FAR_GROUP // 2
        while width >= 1:
            @pl.when(lax.bitwise_and(n_far, width) != 0)
            def _(base=base, width=width):
                tiles([(base + u, None, False) for u in range(width)])

            base = base + lax.bitwise_and(n_far, width)
            width //= 2
    n_near = 1 if mode == "fox" else (3 if mode == "win" else 2)
    for cnt in range(1, n_near + 1):
        cond = (jd >= cnt - 1) if cnt == n_near else (jd == cnt - 1)

        @pl.when(cond)
        def _(cnt=cnt):
            tiles([(jd - a, a, a == 0) for a in reversed(range(cnt))])

    ot = acc_ref[...] / l_ref[...]
    if mode == "diff":
        a = ot[:, :2 * BQ] - _da_lambda(lam_ref[...]) * ot[:, 2 * BQ:]
        r = a * lax.rsqrt(jnp.mean(a * a, 0, keepdims=True) + RMS_EPS) * nw_ref[...] * (1.0 - DA_LAMBDA_INIT)
        r = r.astype(BF16)
        o_ref[...] = jnp.concatenate([_to_rows(r[:, :BQ]), _to_rows(r[:, BQ:])], 0).astype(o_ref.dtype)
    elif mode == "fox":
        o_ref[...] = jnp.concatenate([_to_rows(ot[:, r * BQ:(r + 1) * BQ].astype(BF16)) for r in range(nr)],
                                     1).astype(o_ref.dtype)
    elif mode == "sel":
        for r in range(nr):
            o_ref[0, r] = ot[:, r * BQ:(r + 1) * BQ]
    else:
        gt = jax.nn.sigmoid(gate_ref[0, 0])
        parts = []
        for r in range(nr):
            comb = (gt[3 * r:3 * r + 1] * oc_ref[0, r] + gt[3 * r + 1:3 * r + 2] * os_ref[0, r]
                    + gt[3 * r + 2:3 * r + 3] * ot[:, r * BQ:(r + 1) * BQ])
            parts.append(_to_rows(comb.astype(BF16)))
        o_ref[...] = jnp.concatenate(parts, 1).astype(o_ref.dtype)


def _flash_scratch(nr, dv):
    return [pltpu.VMEM((1, nr * BQ), F32), pltpu.VMEM((1, nr * BQ), F32), pltpu.VMEM((dv, nr * BQ), F32)]


def _diff_prompt(p0b, qt, vt, rel_bias, da_lambda, da_norm, bt, b, t, col_k):
    nq = t // BK
    dv = 2 * HEAD_DIM
    return pl.pallas_call(
        functools.partial(_flash_kernel, mode="diff"), name="diff_prompt",
        out_shape=jax.ShapeDtypeStruct((b * t, DA_HEADS * dv), BF16),
        grid=(b, DA_HEADS, nq),
        in_specs=[pl.BlockSpec(memory_space=pltpu.SMEM),
                  _const_spec((4, HEAD_DIM)), _const_spec((dv, 1)),
                  pl.BlockSpec((1, 4, BK, BQ), lambda bi, h, i: (h, 0, 0, 0)),
                  pl.BlockSpec((1, 1, dv, BK), lambda bi, h, i: (bi, h, 0, i)),
                  pl.BlockSpec((t, 128), lambda bi, h, i: (bi, col_k + 2 * h)),
                  pl.BlockSpec((1, 1, dv, t), lambda bi, h, i: (bi, h, 0, 0))],
        out_specs=pl.BlockSpec((BK, dv), lambda bi, h, i: (bi * nq + i, h)),
        scratch_shapes=_flash_scratch(4, dv),
        compiler_params=_params(("parallel", "parallel", "arbitrary")),
    )(rel_bias, da_lambda, da_norm.reshape(-1, 1), bt, qt, p0b, vt)


def _gqa_prompt(mode, q, k, vt, **kw):
    b, hq, dk, t = q.shape
    hk = k.shape[1]
    nr = hq // hk
    nq = t // BQ
    q_spec = pl.BlockSpec((1, nr, dk, BQ), lambda bi, g, i: (bi, g, 0, i))
    k_spec = pl.BlockSpec((1, 1, t, dk), lambda bi, g, i: (bi, g, 0, 0))
    vt_spec = pl.BlockSpec((1, 1, HEAD_DIM, t), lambda bi, g, i: (bi, g, 0, 0))
    ot_spec = pl.BlockSpec((1, nr, HEAD_DIM, BQ), lambda bi, g, i: (bi, g, 0, i))
    smem = pl.BlockSpec(memory_space=pltpu.SMEM)
    row_out = pl.BlockSpec((BQ, nr * HEAD_DIM), lambda bi, g, i: (bi * nq + i, g))
    if mode == "fox":
        in_specs = [q_spec, k_spec, vt_spec]
        args = (q, k, vt)
        out_shape = jax.ShapeDtypeStruct((b * t, hq * HEAD_DIM), BF16)
        out_spec = row_out
    elif mode == "sel":
        nbs = kw["memb"].shape[2]
        in_specs = [smem, pl.BlockSpec((nr, 4, BK, BQ), lambda bi, g, i: (1 + g, 0, 0, 0)),
                    q_spec, k_spec, vt_spec,
                    pl.BlockSpec((1, 1, nbs, BQ), lambda bi, g, i: (bi, g, 0, i))]
        args = (kw["rel_bias"], kw["bt"], q, k, vt, kw["memb"])
        out_shape = jax.ShapeDtypeStruct((b, hq, HEAD_DIM, t), F32)
        out_spec = ot_spec
    else:
        in_specs = [pl.BlockSpec((nr, 6, BK, BQ), lambda bi, g, i: (g, 0, 0, 0)),
                    q_spec, k_spec, vt_spec, ot_spec, ot_spec,
                    pl.BlockSpec((1, 1, 3 * nr, BQ), lambda bi, g, i: (bi, g, 0, i))]
        args = (kw["bt"], q, k, vt, kw["oc"], kw["os"], kw["gate"])
        out_shape = jax.ShapeDtypeStruct((b * t, hq * HEAD_DIM), BF16)
        out_spec = row_out
    return pl.pallas_call(
        functools.partial(_flash_kernel, mode=mode), name="gqa_" + mode,
        out_shape=out_shape, grid=(b, hk, nq), in_specs=in_specs, out_specs=out_spec,
        scratch_shapes=_flash_scratch(nr, HEAD_DIM),
        compiler_params=_params(("parallel", "parallel", "arbitrary")),
    )(*args)


def _diff_finish_kernel(o1_ref, o2_ref, lam_ref, nw_ref, o_ref):
    a = o1_ref[...] - _da_lambda(lam_ref[...]) * o2_ref[...]
    r = a * lax.rsqrt(jnp.mean(a * a, -1, keepdims=True) + RMS_EPS) * nw_ref[...] * (1.0 - DA_LAMBDA_INIT)
    o_ref[...] = r.astype(o_ref.dtype)


def _diff_finish(o1, o2, da_lambda, da_norm):
    n, w = o1.shape
    return pl.pallas_call(
        _diff_finish_kernel, name="diff_finish",
        out_shape=jax.ShapeDtypeStruct((n, w), BF16),
    )(o1, o2, da_lambda, da_norm.reshape(1, -1))


def _cmp_weight(w1):
    w1r = w1.reshape(2, 2, CMP_STRIDE, HEAD_DIM, CMP_HIDDEN)
    w1c = w1r[jnp.array([0, 0, 1, 1])]
    wb = jnp.einsum("chrdn,ce->rcdehn", w1c, jnp.eye(4, dtype=w1.dtype))
    return wb.reshape(CMP_STRIDE * 4 * HEAD_DIM, 4 * 2 * CMP_HIDDEN).astype(BF16)


def _cmp_finish_kernel(part_ref, pos_ref, w1_ref, w2_ref, kc_ref, vc_ref):
    nch = part_ref.shape[1]
    row = lax.broadcasted_iota(I32, (nch, HEAD_DIM), 0)
    for kv in range(2):
        posw = _dot(pos_ref[kv], w1_ref[kv])[0:1]
        for g in range(NSA_GROUPS):
            c = kv * NSA_GROUPS + g
            a = part_ref[0, :, 256 * c:256 * c + 128]
            bm = part_ref[0, :, 256 * c + 128:256 * c + 256]
            hid = a + pltpu.roll(bm, nch - 1, 0) + posw
            y = _dot(jax.nn.gelu(hid).astype(BF16), w2_ref[kv])
            y = jnp.where(row < nch - 1, y, 0.0).astype(BF16)
            if kv == 0:
                kc_ref[0, g] = y
            else:
                vc_ref[0, g] = y


def _cmp_finish(part, cmp_pos, w1, w2):
    b, nch, _ = part.shape
    pos8 = jnp.broadcast_to(cmp_pos.reshape(2, 1, CMP_BLOCK * HEAD_DIM), (2, 8, CMP_BLOCK * HEAD_DIM)).astype(BF16)
    out = jax.ShapeDtypeStruct((b, NSA_GROUPS, nch, HEAD_DIM), BF16)
    o_spec = pl.BlockSpec((1, NSA_GROUPS, nch, HEAD_DIM), lambda bi: (bi, 0, 0, 0))
    return pl.pallas_call(
        _cmp_finish_kernel, name="cmp_finish",
        out_shape=(out, out), grid=(b,),
        in_specs=[pl.BlockSpec((1, nch, 1024), lambda bi: (bi, 0, 0)), _const_spec(pos8.shape),
                  _const_spec(w1.shape), _const_spec(w2.shape)],
        out_specs=(o_spec, o_spec),
        compiler_params=_params(("parallel",)),
    )(part, pos8, w1.astype(BF16), w2.astype(BF16))


def _paged_cmp_kernel(pt_ref, *refs, npg):
    pages = refs[:npg]
    w_ref, o_ref = refs[npg:]
    nchunk = PAGE // CMP_STRIDE
    feat = pages[0].shape[1]
    ri = lax.broadcasted_iota(I32, (PAGE, PAGE), 0)
    ci = lax.broadcasted_iota(I32, (PAGE, PAGE), 1)
    perm = (ci == CMP_STRIDE * lax.bitwise_and(ri, nchunk - 1) + lax.shift_right_logical(ri, 3)).astype(BF16)
    xps = [_dot_nt(perm, pg[0].astype(BF16)) for pg in pages]
    acc = None
    for r in range(CMP_STRIDE):
        lhs = jnp.concatenate([xp[nchunk * r:nchunk * (r + 1)] for xp in xps], 0).astype(BF16)
        t = _dot(lhs, w_ref[feat * r:feat * (r + 1), :])
        acc = t if acc is None else acc + t
    o_ref[...] = acc


def _paged_cmp(pool, pt_flat, w, npg=16):
    n = pt_flat.shape[0]
    _, f, _ = pool.shape
    c = w.shape[1]
    npg = math.gcd(npg, n)
    nchunk = PAGE // CMP_STRIDE
    specs = [pl.BlockSpec((1, f, PAGE), (lambda s, pt, p=p: (pt[s * npg + p], 0, 0))) for p in range(npg)]
    grid_spec = pltpu.PrefetchScalarGridSpec(
        num_scalar_prefetch=1, grid=(n // npg,),
        in_specs=specs + [pl.BlockSpec(w.shape, lambda s, pt: (0, 0), pipeline_mode=pl.Buffered(1))],
        out_specs=pl.BlockSpec((npg * nchunk, c), lambda s, pt: (s, 0)))
    return pl.pallas_call(
        functools.partial(_paged_cmp_kernel, npg=npg), name="paged_cmp",
        out_shape=jax.ShapeDtypeStruct((n * nchunk, c), F32), grid_spec=grid_spec,
        compiler_params=_params(("parallel",)),
    )(pt_flat, *([pool] * npg), w)


def _overlap_t(nbs, ncp):
    jb = lax.broadcasted_iota(I32, (nbs, ncp), 0) * SEL_BLOCK
    ci = lax.broadcasted_iota(I32, (nbs, ncp), 1) * CMP_STRIDE
    return ((ci < jb + SEL_BLOCK) & (ci + CMP_BLOCK > jb)).astype(BF16)


def _cmp_prompt_kernel(rel_ref, q_ref, kc_ref, vc_ref, oc_ref, memb_ref, *, nbs, k_eff):
    g = pl.program_id(1)
    i = pl.program_id(2)
    nr = q_ref.shape[1]
    ncp = kc_ref.shape[2]
    qs = q_ref[0].reshape(nr * BQ, HEAD_DIM) * jnp.asarray(SCALE, BF16)
    s = _dot_nt(qs, kc_ref[0, 0])
    pos = i * BQ + lax.broadcasted_iota(I32, (BQ, ncp), 0)
    d = pos - (lax.broadcasted_iota(I32, (BQ, ncp), 1) * CMP_STRIDE + CMP_BLOCK - 1)
    ok = d >= 0
    okf = ok.astype(F32)
    p_parts = []
    for r in range(nr):
        h = DA_HEADS + g * nr + r
        sr = jnp.where(ok, s[r * BQ:(r + 1) * BQ] + _t5_bias(d, lambda b: rel_ref[b, h]), NEG_INF)
        e = jnp.exp(sr - jnp.max(sr, -1, keepdims=True))
        p_parts.append(e / jnp.sum(e, -1, keepdims=True) * okf)
    p = jnp.concatenate(p_parts, 0)
    oc_ref[0] = _dot(p.astype(BF16), vc_ref[0, 0]).reshape(nr, BQ, HEAD_DIM)
    psum = p_parts[0]
    for r in range(1, nr):
        psum = psum + p_parts[r]
    ot = _overlap_t(nbs, ncp)
    imp = None
    for part in _split3(psum):
        t = _dot_nt(ot, part)
        imp = t if imp is None else imp + t
    jrow = lax.broadcasted_iota(I32, (nbs, BQ), 0)
    cur = lax.shift_right_logical(i * BQ + lax.broadcasted_iota(I32, (nbs, BQ), 1), 6)
    valid = jrow <= cur
    forced = valid & ((jrow == 0) | (jrow >= cur - 1))
    score = jnp.where(valid, imp + jnp.where(forced, FORCED_BONUS, 0.0), -1.0)
    rank = jnp.zeros((nbs, BQ), I32)
    for ii in range(nbs):
        row = score[ii:ii + 1, :]
        ahead = (row > score) | ((row == score) & (jrow > ii))
        rank = rank + ahead.astype(I32)
    memb_ref[0, 0] = ((rank < k_eff) & (score >= 0.0)).astype(memb_ref.dtype)


def _cmp_prompt(q, kc, vc, rel_bias):
    b, hq, t, _ = q.shape
    nr = hq // NSA_GROUPS
    nq = t // BQ
    ncp = kc.shape[2]
    nbs = -(-t // SEL_BLOCK)
    q_spec = pl.BlockSpec((1, nr, BQ, HEAD_DIM), lambda bi, g, i: (bi, g, i, 0))
    c_spec = pl.BlockSpec((1, 1, ncp, HEAD_DIM), lambda bi, g, i: (bi, g, 0, 0))
    return pl.pallas_call(
        functools.partial(_cmp_prompt_kernel, nbs=nbs, k_eff=min(SEL_TOPK, nbs)), name="cmp_prompt",
        out_shape=(jax.ShapeDtypeStruct((b, hq, t, HEAD_DIM), F32),
                   jax.ShapeDtypeStruct((b, NSA_GROUPS, nbs, t), BF16)),
        grid=(b, NSA_GROUPS, nq),
        in_specs=[pl.BlockSpec(memory_space=pltpu.SMEM), q_spec, c_spec, c_spec],
        out_specs=(q_spec, pl.BlockSpec((1, 1, nbs, BQ), lambda bi, g, i: (bi, g, 0, i))),
        compiler_params=_params(("parallel", "parallel", "parallel")),
    )(rel_bias, q, kc, vc)


def _cmp_sample_kernel(q_ref, rowtbl_ref, kc_ref, vc_ref, oc_ref, memb_ref, *, qpos0, nbs, k_eff):
    ncp = kc_ref.shape[2]
    mb = memb_ref.shape[2]
    rows = NSA_R * 8
    qi = lax.bitwise_and(lax.broadcasted_iota(I32, (rows, ncp), 0), 7)
    d = qpos0 + qi - (lax.broadcasted_iota(I32, (rows, ncp), 1) * CMP_STRIDE + CMP_BLOCK - 1)
    ok = d >= 0
    jb = lax.broadcasted_iota(I32, (ncp, mb), 1) * SEL_BLOCK
    ci = lax.broadcasted_iota(I32, (ncp, mb), 0) * CMP_STRIDE
    ov = ((ci < jb + SEL_BLOCK) & (ci + CMP_BLOCK > jb)).astype(BF16)
    jl = lax.broadcasted_iota(I32, (8, mb), 1)
    cur = lax.shift_right_logical(qpos0 + lax.broadcasted_iota(I32, (8, mb), 0), 6)
    valid = (jl <= cur) & (jl < nbs)
    forced = valid & ((jl == 0) | (jl >= cur - 1))
    for g in range(NSA_GROUPS):
        s = _dot_nt(q_ref[0, g * rows:(g + 1) * rows], kc_ref[0, g])
        tbl = rowtbl_ref[g * rows:(g + 1) * rows]
        sm = jnp.where(ok, s + _t5_bias(d, lambda b: tbl[:, b:b + 1]), NEG_INF)
        e = jnp.exp(sm - jnp.max(sm, -1, keepdims=True))
        p = e / jnp.sum(e, -1, keepdims=True) * ok.astype(F32)
        oc_ref[0, g * rows:(g + 1) * rows] = _dot(p.astype(BF16), vc_ref[0, g])
        psum = p[0:8]
        for r in range(1, NSA_R):
            psum = psum + p[8 * r:8 * r + 8]
        imp = None
        for part in _split3(psum):
            t = _dot(part, ov)
            imp = t if imp is None else imp + t
        score = jnp.where(valid, imp + jnp.where(forced, FORCED_BONUS, 0.0), -1.0)
        score = jnp.where(jl < nbs, score, -2.0)
        rank = jnp.zeros((8, mb), I32)
        for ii in range(nbs):
            col = score[:, ii:ii + 1]
            ahead = (col > score) | ((col == score) & (jl > ii))
            rank = rank + ahead.astype(I32)
        memb_ref[0, g * 8:(g + 1) * 8] = ((rank < k_eff) & (score >= 0.0)).astype(F32)


def _cmp_sample(qrows, rowtbl, kc, vc, qpos0, nbs, mb):
    b = qrows.shape[0]
    ncp = kc.shape[2]
    nrows = NSA_GROUPS * NSA_R * 8
    c_spec = pl.BlockSpec((1, NSA_GROUPS, ncp, HEAD_DIM), lambda bi: (bi, 0, 0, 0))
    return pl.pallas_call(
        functools.partial(_cmp_sample_kernel, qpos0=qpos0, nbs=nbs, k_eff=min(SEL_TOPK, nbs)),
        name="cmp_sample",
        out_shape=(jax.ShapeDtypeStruct((b, nrows, HEAD_DIM), F32),
                   jax.ShapeDtypeStruct((b, NSA_GROUPS * 8, mb), F32)),
        grid=(b,),
        in_specs=[pl.BlockSpec((1, nrows, HEAD_DIM), lambda bi: (bi, 0, 0)), _const_spec(rowtbl.shape),
                  c_spec, c_spec],
        out_specs=(pl.BlockSpec((1, nrows, HEAD_DIM), lambda bi: (bi, 0, 0)),
                   pl.BlockSpec((1, NSA_GROUPS * 8, mb), lambda bi: (bi, 0, 0))),
        compiler_params=_params(("parallel",)),
    )(qrows, rowtbl, kc, vc)


def _decode_kernel(pt_ref, *refs, mode, npg, n_pages, pw, kpos0, qpos0, feat_major):
    q_ref, rowq_ref = refs[0], refs[1]
    k = 2
    if mode == "fox":
        cq_ref, aux_ref, auxtail_ref = refs[k:k + 3]
        k += 3
    else:
        rowtbl_ref = refs[k]
        k += 1
        if mode == "sel":
            aux_ref, auxtail_ref = refs[k:k + 2]
            k += 2
    pages = refs[k:k + npg]
    tail_ref, o_ref, m_ref, l_ref, acc_ref = refs[k + npg:k + npg + 5]
    step = pl.program_id(1)
    nr = q_ref.shape[1]
    q = q_ref[0]
    rowpos = qpos0 + rowq_ref[:, 0:1]
    key_shift = 3 if mode == "diffx" else 0

    @pl.when(step == 0)
    def _():
        _init_state(m_ref, l_ref, acc_ref)
        if mode == "diffx":
            far_ref = refs[k + npg + 5]
            col = lax.broadcasted_iota(I32, far_ref.shape, 1)
            far_ref[...] = jnp.where(lax.bitwise_and(col, 7) == rowq_ref[:, 1:2],
                                     rowtbl_ref[:, N_BUCKETS - 1:N_BUCKETS], NEG_INF)

    def process(page_list, kp0, aux, is_tail):
        kbs = [pg.astype(BF16) for pg in page_list]
        if feat_major:
            half = kbs[0].shape[0] // 2
            ss = [_dot(q, kb[:half]) for kb in kbs]
            widths = [kb.shape[1] for kb in kbs]
        else:
            ss = [_dot_nt(q, kb) for kb in kbs]
            widths = [kb.shape[0] for kb in kbs]
        s = ss[0] if len(ss) == 1 else jnp.concatenate(ss, 1)
        n = s.shape[1]

        def distance():
            col = lax.broadcasted_iota(I32, (nr, n), 1)
            return col, rowpos - (kp0 + lax.shift_right_logical(col, key_shift))

        def biased(sv):
            col, d = distance()
            tbl = rowtbl_ref[...]
            bad = d < 0
            if mode == "win":
                bad = bad | (d >= WINDOW)
            if mode == "sel":
                bad = bad | (aux < 0.5 * NEG_INF)
            if mode == "diffx":
                bad = bad | (lax.bitwise_and(col, 7) != rowq_ref[:, 1:2])
            return jnp.where(bad, NEG_INF, sv + _t5_bias(d, lambda b: tbl[:, b:b + 1]))

        if mode == "fox":
            s = s + (cq_ref[0] - jnp.concatenate([aux] * (nr // FOX_HEADS), 0))
            if is_tail:
                s = jnp.where(distance()[1] < 0, NEG_INF, s)
        elif is_tail or mode == "win":
            s = biased(s)
        else:
            near = qpos0 - (kp0 + (n >> key_shift) - 1) < T5_FAR
            far_bias = refs[k + npg + 5][...] if mode == "diffx" else aux
            s = lax.cond(near, lambda: biased(s), lambda: s + far_bias)

        def accum(p):
            out = None
            off = 0
            for kb, w in zip(kbs, widths):
                pp = p[:, off:off + w]
                if mode == "diffx":
                    pp = pltpu.roll(pp, 4, 1)
                pp = pp.astype(BF16)
                t = _dot_nt(pp, kb[half:]) if feat_major else _dot(pp, kb)
                out = t if out is None else out + t
                off += w
            return out

        _online_update(s, accum, m_ref, l_ref, acc_ref)

    keys_pp = pw >> key_shift
    process([pg[0] for pg in pages], kpos0 + step * (npg * keys_pp),
            aux_ref[0] if mode in ("fox", "sel") else None, False)

    @pl.when(step == pl.num_programs(1) - 1)
    def _():
        process([tail_ref[0]], qpos0, auxtail_ref[0] if mode in ("fox", "sel") else None, True)
        o_ref[0] = acc_ref[...] / l_ref[...]


def _decode(mode, qm, rowq, pool, pt_flat, tail, n_pages, npg, kpos0, qpos0, feat_major, **kw):
    b, nr, _ = qm.shape
    npg = math.gcd(npg, n_pages)
    n_steps = n_pages // npg
    pshape = pool.shape[1:]
    pw = pshape[1] if feat_major else pshape[0]
    accw = pshape[0] // 2 if feat_major else pshape[1]
    in_specs = [pl.BlockSpec((1,) + qm.shape[1:], lambda bi, s, pt: (bi, 0, 0)),
                pl.BlockSpec((nr, 2), lambda bi, s, pt: (0, 0))]
    args = [qm, rowq]
    if mode == "fox":
        in_specs.append(pl.BlockSpec((1, nr, 1), lambda bi, s, pt: (bi, 0, 0)))
        args.append(kw["cq"])
    else:
        in_specs.append(pl.BlockSpec((nr, N_BUCKETS), lambda bi, s, pt: (0, 0)))
        args.append(kw["rowtbl"])
    if mode in ("fox", "sel"):
        aux = kw["aux"]
        in_specs += [pl.BlockSpec((1, aux.shape[1], npg * pw), lambda bi, s, pt: (bi, 0, s)),
                     pl.BlockSpec((1, aux.shape[1], PAGE), lambda bi, s, pt: (bi, 0, (n_pages * pw) // PAGE))]
        args += [aux, aux]
    in_specs += [pl.BlockSpec((1,) + pshape, (lambda bi, s, pt, p=p: (pt[bi * n_pages + s * npg + p], 0, 0)))
                 for p in range(npg)]
    args += [pool] * npg
    in_specs.append(pl.BlockSpec((1,) + tail.shape[1:], lambda bi, s, pt: (bi, 0, 0)))
    args.append(tail)
    scratch = [pltpu.VMEM((nr, 1), F32), pltpu.VMEM((nr, 1), F32), pltpu.VMEM((nr, accw), F32)]
    if mode == "diffx":
        scratch.append(pltpu.VMEM((nr, npg * pw), F32))
    grid_spec = pltpu.PrefetchScalarGridSpec(
        num_scalar_prefetch=1, grid=(b, n_steps), in_specs=in_specs,
        out_specs=pl.BlockSpec((1, nr, accw), lambda bi, s, pt: (bi, 0, 0)),
        scratch_shapes=scratch)
    return pl.pallas_call(
        functools.partial(_decode_kernel, mode=mode, npg=npg, n_pages=n_pages, pw=pw, kpos0=kpos0, qpos0=qpos0,
                          feat_major=feat_major),
        name="decode_" + mode,
        out_shape=jax.ShapeDtypeStruct((b, nr, accw), F32), grid_spec=grid_spec,
        compiler_params=_params(("parallel", "arbitrary")),
    )(pt_flat, *args)


def _nsa_combine_kernel(oc_ref, os_ref, ow_ref, gate_ref, o_ref):
    gt = jax.nn.sigmoid(gate_ref[...])
    o = gt[:, 0:1] * oc_ref[...] + gt[:, 1:2] * os_ref[...] + gt[:, 2:3] * ow_ref[...]
    o_ref[...] = o.astype(o_ref.dtype)


def _nsa_combine(oc, os_, ow, gate):
    return pl.pallas_call(
        _nsa_combine_kernel, name="nsa_combine",
        out_shape=jax.ShapeDtypeStruct(oc.shape, BF16),
    )(oc, os_, ow, gate)


def _cumsum_kernel(pt_ref, bf_ref, cin_ref, *refs, npg, apply_logsig):
    pages = refs[:npg]
    lf_ref, c_ref, carry_ref = refs[npg:]

    @pl.when(pl.program_id(1) == 0)
    def _():
        carry_ref[...] = cin_ref[0]

    tri = (lax.broadcasted_iota(I32, (PAGE, PAGE), 0) <= lax.broadcasted_iota(I32, (PAGE, PAGE), 1)).astype(BF16)
    car = carry_ref[...]
    h = pages[0].shape[1]
    group = max(1, PAGE // h)
    for p0 in range(0, npg, group):
        xs = []
        for p in range(p0, min(p0 + group, npg)):
            x = pages[p][0]
            if apply_logsig:
                z = x + bf_ref[...]
                x = jnp.minimum(z, 0.0) - jnp.log1p(jnp.exp(-jnp.abs(z)))
            lf_ref[0, :, p * PAGE:(p + 1) * PAGE] = x
            xs.append(x)
        xcat = xs[0] if len(xs) == 1 else jnp.concatenate(xs, 0)
        loc = None
        for part in _split3(xcat):
            t = _dot(part, tri)
            loc = t if loc is None else loc + t
        for u, p in enumerate(range(p0, min(p0 + group, npg))):
            loc_p = loc[u * h:(u + 1) * h]
            c_ref[0, :, p * PAGE:(p + 1) * PAGE] = loc_p + car
            car = car + loc_p[:, PAGE - 1:PAGE]
    carry_ref[...] = car


def _cumsum_pages(pool, pt_flat, b, n_pages, b_f, carry_in, apply_logsig, npg=32):
    h = pool.shape[1]
    npg = math.gcd(npg, n_pages)
    in_specs = [pl.BlockSpec((h, 1), lambda bi, s, pt: (0, 0)),
                pl.BlockSpec((1, h, 1), lambda bi, s, pt: (bi, 0, 0))]
    in_specs += [pl.BlockSpec((1, h, PAGE), (lambda bi, s, pt, p=p: (pt[bi * n_pages + s * npg + p], 0, 0)))
                 for p in range(npg)]
    o_spec = pl.BlockSpec((1, h, npg * PAGE), lambda bi, s, pt: (bi, 0, s))
    out = jax.ShapeDtypeStruct((b, h, n_pages * PAGE), F32)
    grid_spec = pltpu.PrefetchScalarGridSpec(
        num_scalar_prefetch=1, grid=(b, n_pages // npg), in_specs=in_specs, out_specs=(o_spec, o_spec),
        scratch_shapes=[pltpu.VMEM((h, 1), F32)])
    return pl.pallas_call(
        functools.partial(_cumsum_kernel, npg=npg, apply_logsig=apply_logsig), name="cumsum_pages",
        out_shape=(out, out), grid_spec=grid_spec,
        compiler_params=_params(("parallel", "arbitrary")),
    )(pt_flat, b_f.reshape(h, 1), carry_in, *([pool] * npg))


def _router_kernel(x_ref, w_ref, o_ref):
    x = x_ref[...]
    w = w_ref[...]
    xh = x.astype(BF16)
    xl = (x - xh.astype(F32)).astype(BF16)
    wh = w.astype(BF16)
    wl = (w - wh.astype(F32)).astype(BF16)
    logits = _dot(xh, wh) + _dot(xl, wh) + _dot(xh, wl)
    lane = lax.broadcasted_iota(I32, logits.shape, 1)
    big = logits.shape[1]
    lg = jnp.where(lane < N_EXPERTS, logits, -jnp.inf)
    m1 = jnp.max(lg, -1, keepdims=True)
    i1 = jnp.min(jnp.where(lg == m1, lane, big), -1, keepdims=True)
    lg2 = jnp.where(lane == i1, -jnp.inf, lg)
    m2 = jnp.max(lg2, -1, keepdims=True)
    i2 = jnp.min(jnp.where(lg2 == m2, lane, big), -1, keepdims=True)
    e2 = jnp.exp(m2 - m1)
    g1 = 1.0 / (1.0 + e2)
    g2 = e2 / (1.0 + e2)
    out = jnp.where(lane == 0, i1.astype(F32), jnp.where(lane == 1, i2.astype(F32),
                    jnp.where(lane == 2, g1, jnp.where(lane == 3, g2, 0.0))))
    o_ref[...] = out


def _router(x, w_router):
    n, dm = x.shape
    tm = _row_tile(n)
    wp = jnp.zeros((dm, 128), F32).at[:, :N_EXPERTS].set(w_router)
    return pl.pallas_call(
        _router_kernel, name="router",
        out_shape=jax.ShapeDtypeStruct((n, 128), F32),
        grid=(n // tm,),
        in_specs=[pl.BlockSpec((tm, dm), lambda i: (i, 0)), _const_spec((dm, 128))],
        out_specs=pl.BlockSpec((tm, 128), lambda i: (i, 0)),
        compiler_params=_params(("parallel",)),
    )(x, wp)


def _row_copy(src_hbm, src_row, dst_ref, r, sem):
    return pltpu.make_async_copy(src_hbm.at[pl.ds(src_row, 1)], dst_ref.at[pl.ds(r, 1)], sem)


def _gather_start(idx_ref, n_rows, src_hbm, dst_ref, sem):
    for r in range(n_rows):
        _row_copy(src_hbm, idx_ref[0, 0, r], dst_ref, r, sem).start()


def _gather_wait(n_rows, src_hbm, dst_ref, sem):
    for r in range(n_rows):
        _row_copy(src_hbm, 0, dst_ref, r, sem).wait()


def _expert_kernel(be_ref, nb_ref, cur_ref, nxt_ref, x_hbm, wg_ref, wu_ref, wd_ref, o_ref, xbuf, sem):
    i = pl.program_id(0)
    n_used = nb_ref[0]
    slot = lax.bitwise_and(i, 1)

    @pl.when((i == 0) & (n_used > 0))
    def _():
        _gather_start(cur_ref, MOE_BLOCK, x_hbm, xbuf.at[0], sem.at[0])

    @pl.when(i + 1 < n_used)
    def _():
        _gather_start(nxt_ref, MOE_BLOCK, x_hbm, xbuf.at[1 - slot], sem.at[1 - slot])

    @pl.when(i < n_used)
    def _():
        _gather_wait(MOE_BLOCK, x_hbm, xbuf.at[slot], sem.at[slot])
        xb = xbuf[slot].astype(BF16)
        h = jax.nn.silu(_dot(xb, wg_ref[0])) * _dot(xb, wu_ref[0])
        o_ref[...] = _dot(h.astype(BF16), wd_ref[0])

    @pl.when(i >= n_used)
    def _():
        o_ref[...] = jnp.zeros(o_ref.shape, o_ref.dtype)


def _experts(x, src_rows, block_e, n_used, wg, wu, wd):
    nb = src_rows.shape[0]
    src_rows = src_rows.reshape(nb, 1, MOE_BLOCK)
    dm = x.shape[1]
    dff = wg.shape[2]
    grid_spec = pltpu.PrefetchScalarGridSpec(
        num_scalar_prefetch=2, grid=(nb,),
        in_specs=[pl.BlockSpec((1, 1, MOE_BLOCK), lambda i, be, nu: (i, 0, 0), memory_space=pltpu.SMEM),
                  pl.BlockSpec((1, 1, MOE_BLOCK), lambda i, be, nu: (jnp.minimum(i + 1, nb - 1), 0, 0),
                               memory_space=pltpu.SMEM),
                  pl.BlockSpec(memory_space=pl.ANY),
                  pl.BlockSpec((1, dm, dff), lambda i, be, nu: (be[i], 0, 0)),
                  pl.BlockSpec((1, dm, dff), lambda i, be, nu: (be[i], 0, 0)),
                  pl.BlockSpec((1, dff, dm), lambda i, be, nu: (be[i], 0, 0))],
        out_specs=pl.BlockSpec((MOE_BLOCK, dm), lambda i, be, nu: (i, 0)),
        scratch_shapes=[pltpu.VMEM((2, MOE_BLOCK, dm), F32), pltpu.SemaphoreType.DMA((2,))])
    return pl.pallas_call(
        _expert_kernel, name="experts",
        out_shape=jax.ShapeDtypeStruct((nb * MOE_BLOCK, dm), F32), grid_spec=grid_spec,
        compiler_params=_params(("arbitrary",)),
    )(block_e, n_used, src_rows, src_rows, x, wg, wu, wd)


def _combine_ln_kernel(cur_ref, nxt_ref, y_hbm, x_ref, route_ref, g_ref, b_ref, o_ref, ybuf, sem):
    i = pl.program_id(0)
    tm = x_ref.shape[0]
    slot = lax.bitwise_and(i, 1)

    @pl.when(i == 0)
    def _():
        _gather_start(cur_ref, TOP_K * tm, y_hbm, ybuf.at[0], sem.at[0])

    @pl.when(i + 1 < pl.num_programs(0))
    def _():
        _gather_start(nxt_ref, TOP_K * tm, y_hbm, ybuf.at[1 - slot], sem.at[1 - slot])

    _gather_wait(TOP_K * tm, y_hbm, ybuf.at[slot], sem.at[slot])
    gate = route_ref[...]
    f = gate[:, 2:3] * ybuf[slot, 0:tm] + gate[:, 3:4] * ybuf[slot, tm:2 * tm]
    o_ref[...] = _layer_norm(ALPHA * x_ref[...] + f, g_ref[...], b_ref[...])


def _combine_ln(x, route, y_sorted, pos, g, b):
    n, dm = x.shape
    tm = pos.shape[1] // TOP_K
    nt = n // tm
    pos = pos.reshape(nt, 1, TOP_K * tm)
    return pl.pallas_call(
        _combine_ln_kernel, name="combine_ln",
        out_shape=jax.ShapeDtypeStruct((n, dm), F32),
        grid=(nt,),
        in_specs=[pl.BlockSpec((1, 1, TOP_K * tm), lambda i: (i, 0, 0), memory_space=pltpu.SMEM),
                  pl.BlockSpec((1, 1, TOP_K * tm), lambda i: (jnp.minimum(i + 1, nt - 1), 0, 0),
                               memory_space=pltpu.SMEM),
                  pl.BlockSpec(memory_space=pl.ANY),
                  pl.BlockSpec((tm, dm), lambda i: (i, 0)), pl.BlockSpec((tm, 128), lambda i: (i, 0)),
                  _const_spec((1, dm)), _const_spec((1, dm))],
        out_specs=pl.BlockSpec((tm, dm), lambda i: (i, 0)),
        scratch_shapes=[pltpu.VMEM((2, TOP_K * tm, dm), F32), pltpu.SemaphoreType.DMA((2,))],
        compiler_params=_params(("arbitrary",)),
    )(pos, pos, y_sorted, x, route, g.reshape(1, dm), b.reshape(1, dm))


def _moe_ln(x, w_router, wg, wu, wd, g, b):
    n, dm = x.shape
    route = _router(x, w_router)
    top_e = route[:, 0:2].astype(I32)
    n_assign = n * TOP_K
    flat_e = top_e.reshape(-1)
    order = jnp.argsort(flat_e)
    e_sorted = flat_e[order]
    counts = jnp.bincount(flat_e, length=N_EXPERTS)
    padded = (counts + MOE_BLOCK - 1) // MOE_BLOCK * MOE_BLOCK
    start = jnp.cumsum(counts) - counts
    pad_end = jnp.cumsum(padded)
    pad_start = pad_end - padded
    dest = (pad_start[e_sorted] + jnp.arange(n_assign) - start[e_sorted]).astype(I32)
    n_blocks = -(-n_assign // MOE_BLOCK) + N_EXPERTS
    block_e = jnp.minimum(jnp.searchsorted(pad_end, jnp.arange(n_blocks) * MOE_BLOCK, side="right"),
                          N_EXPERTS - 1).astype(I32)
    n_used = (pad_end[-1] // MOE_BLOCK).astype(I32).reshape(1)
    src_rows = jnp.zeros((n_blocks * MOE_BLOCK,), I32).at[dest].set((order // TOP_K).astype(I32))
    y_sorted = _experts(x, src_rows.reshape(n_blocks, MOE_BLOCK), block_e, n_used,
                        wg.astype(BF16), wu.astype(BF16), wd.astype(BF16))
    tm = min(_row_tile(n), MOE_BLOCK)
    pos_of = jnp.zeros((n_assign,), I32).at[order].set(dest).reshape(n // tm, tm, TOP_K)
    pos = pos_of.transpose(0, 2, 1).reshape(n // tm, TOP_K * tm)
    return _combine_ln(x, route, y_sorted, pos, g, b)


C0_QDA, C0_DIFF, C0_QNS, C0_CMP, C0_SEL, C0_WIN, C0_GATE, C0 = 0, 512, 1536, 2048, 2304, 2560, 2816, 2944
C1_Q, C1_KV, C1_F, C1 = 0, 1024, 1536, 1664
FOX_DK = 128


def _l0_weight(w_in):
    k_off, v_off = 512, 1024
    cols = list(range(0, 512))
    for h in range(DA_HEADS):
        cols += list(range(k_off + 128 * h, k_off + 128 * (h + 1)))
        cols += list(range(v_off + 128 * h, v_off + 128 * (h + 1)))
    cols += list(range(1536, 2840))
    w = w_in[:, np.asarray(cols)]
    return jnp.pad(w, ((0, 0), (0, C0 - w.shape[1]))).astype(BF16)


def _pad_rows(x, rows):
    return jnp.pad(x, ((0, 0), (0, rows - x.shape[1])) + ((0, 0),) * (x.ndim - 2))


def _fox_augment(q, k, c):
    b, t = c.shape[:2]

    def split3(x):
        def top(v):
            return lax.bitcast_convert_type(lax.bitcast_convert_type(v, jnp.uint32) & jnp.uint32(0xFFFF0000), F32)
        hi = top(x)
        mid = top(x - hi)
        lo = top(x - hi - mid)
        return jnp.stack([hi.astype(BF16), mid.astype(BF16), lo.astype(BF16)], -1)

    c3 = split3(c)
    nc3 = split3(-c).reshape(b, t, FOX_KV_HEADS, FOX_R * 3)
    slot = jnp.asarray(np.repeat(np.eye(FOX_R), 3, axis=1), BF16)
    slot = jnp.broadcast_to(jnp.tile(slot, (FOX_KV_HEADS, 1))[None, None], (b, t, FOX_HEADS, 3 * FOX_R))
    pad_q = jnp.zeros((b, t, FOX_HEADS, FOX_DK - HEAD_DIM - 3 - 3 * FOX_R), BF16)
    qa = jnp.concatenate([q * jnp.asarray(SCALE, BF16), c3, slot, pad_q], -1)
    ones = jnp.ones((b, t, FOX_KV_HEADS, 3), BF16)
    pad_k = jnp.zeros((b, t, FOX_KV_HEADS, FOX_DK - HEAD_DIM - 3 - 3 * FOX_R), BF16)
    ka = jnp.concatenate([k, ones, nc3, pad_k], -1)
    return qa.transpose(0, 2, 3, 1), ka.transpose(0, 2, 1, 3)


def kernel(x_prompt, x_sample, cache_diff_kv, cache_nsa_cmp, cache_nsa_sel, state_nsa_win, cache_fox_kv,
           cache_fox_logf, page_table, rel_bias, l0_w_in, l0_w_out, da_lambda, da_norm, nsa_cmp_pos,
           nsa_cmp_w1, nsa_cmp_w2, ffn_w_gate, ffn_w_up, ffn_w_down, l1_w_in, fox_b_f, l1_w_out,
           moe_router, moe_w_gate, moe_w_up, moe_w_down, ln_g, ln_b):
    b, t, dm = x_prompt.shape
    bs, ts, _ = x_sample.shape
    n_pages = page_table.shape[1]
    n_phys = cache_diff_kv.shape[0]
    past = n_pages * PAGE
    np_, ns = b * t, bs * ts
    pt_flat = page_table.reshape(-1).astype(I32)
    x0 = jnp.concatenate([x_prompt.reshape(np_, dm), x_sample.reshape(ns, dm)], 0)
    g2 = NSA_GROUPS

    p0, p0b = _proj(x0, _l0_weight(l0_w_in))
    bt_c = _bias_tiles(rel_bias, 4, False, 0, DA_HEADS + NSA_HEADS)
    bt_w = _bias_tiles(rel_bias, 6, True, DA_HEADS, NSA_HEADS)
    rel_t = rel_bias.T

    vt_da = p0b[:np_, C0_DIFF:C0_DIFF + 1024].reshape(b, t, DA_HEADS, 4 * HEAD_DIM)[..., 2 * HEAD_DIM:]
    qt_da = p0b[:np_, C0_QDA:C0_QDA + 512].reshape(b, t, DA_HEADS, 2 * HEAD_DIM).transpose(0, 2, 3, 1)
    o_da_p = _diff_prompt(p0b, qt_da, vt_da.transpose(0, 2, 3, 1), rel_bias, da_lambda, da_norm, bt_c, b, t,
                          C0_DIFF // 128)

    q_ns_p = p0b[:np_, C0_QNS:C0_QNS + 512].reshape(b, t, NSA_HEADS, HEAD_DIM).transpose(0, 2, 1, 3)

    def kv_major(col):
        kv = p0b[:np_, col:col + 256].reshape(b, t, 2, g2, HEAD_DIM)
        return kv[:, :, 0].transpose(0, 2, 1, 3), kv[:, :, 1].transpose(0, 2, 3, 1)

    wbig = _cmp_weight(nsa_cmp_w1)
    part_p = _rowmm(p0[:np_, C0_CMP:C0_CMP + 256].reshape(np_ // CMP_STRIDE, CMP_STRIDE * 256), wbig)
    kc_p, vc_p = _cmp_finish(part_p.reshape(b, t // CMP_STRIDE, 1024), nsa_cmp_pos, nsa_cmp_w1, nsa_cmp_w2)
    oc_p, memb_p = _cmp_prompt(q_ns_p, kc_p, vc_p, rel_bias)
    ks_p, vst_p = kv_major(C0_SEL)
    qt_ns_p = q_ns_p.transpose(0, 1, 3, 2)
    os_p = _gqa_prompt("sel", qt_ns_p, ks_p, vst_p, rel_bias=rel_bias, bt=bt_c, memb=memb_p)
    kw_p, vwt_p = kv_major(C0_WIN)
    gate_p = p0[:np_, C0_GATE:C0_GATE + 3 * NSA_HEADS].reshape(b, t, g2, 3 * NSA_R).transpose(0, 2, 3, 1)
    o_ns_p = _gqa_prompt("win", qt_ns_p, kw_p, vwt_p, bt=bt_w, oc=oc_p.transpose(0, 1, 3, 2), os=os_p, gate=gate_p)

    qpos0 = past
    qd = p0b[np_:, C0_QDA:C0_QDA + 512].reshape(bs, ts, DA_HEADS, 2, HEAD_DIM).transpose(0, 3, 2, 1, 4)
    qm_d = jnp.einsum("bwhqd,ws->bwhqsd", qd, jnp.eye(2, dtype=BF16)) * jnp.asarray(SCALE, BF16)
    qm_d = qm_d.reshape(bs, 2 * DA_HEADS * ts, 2 * HEAD_DIM)
    rows_d = [(w, h, q) for w in range(2) for h in range(DA_HEADS) for q in range(ts)]
    rowq_d = jnp.asarray(np.array([[r[2], r[1]] for r in rows_d], np.int32))
    rowtbl_d = rel_t[np.array([r[1] for r in rows_d])]

    def key_rows(x):
        lead = x.shape[:-2]
        x = x.reshape(lead + (PAGE, DA_HEADS, 2, 2 * HEAD_DIM))
        return jnp.swapaxes(x, -3, -2).reshape(lead + (PAGE * 2 * DA_HEADS, 2 * HEAD_DIM))

    tail_d = key_rows(_pad_rows(p0[np_:, C0_DIFF:C0_DIFF + 1024].reshape(bs, ts, 1024), PAGE))
    acc_d = _decode("diffx", qm_d, rowq_d, key_rows(cache_diff_kv.reshape(n_phys, PAGE, 1024)), pt_flat, tail_d,
                    n_pages, DECODE_PAGES, 0, qpos0, False, rowtbl=rowtbl_d)
    o12 = acc_d.reshape(bs, 2, DA_HEADS * ts, 2 * HEAD_DIM)
    o_da_s = _diff_finish(o12[:, 0].reshape(bs * DA_HEADS * ts, 128), o12[:, 1].reshape(bs * DA_HEADS * ts, 128),
                          da_lambda, da_norm)
    o_da_s = o_da_s.reshape(bs, DA_HEADS, ts, 128).transpose(0, 2, 1, 3).reshape(ns, 512)

    part_s = _paged_cmp(cache_nsa_cmp.transpose(0, 2, 3, 4, 1).reshape(n_phys, 4 * HEAD_DIM, PAGE), pt_flat, wbig)
    l_tot = past + ts
    if l_tot // CMP_STRIDE > past // CMP_STRIDE:
        raise NotImplementedError("new rows completing a compression chunk")
    kc_s, vc_s = _cmp_finish(part_s.reshape(bs, past // CMP_STRIDE, 1024), nsa_cmp_pos, nsa_cmp_w1, nsa_cmp_w2)
    nbs_s = -(-l_tot // SEL_BLOCK)
    mb_s = -(-nbs_s // 128) * 128
    qn = p0b[np_:, C0_QNS:C0_QNS + 512].reshape(bs, ts, g2, NSA_R, HEAD_DIM) * jnp.asarray(SCALE, BF16)
    q_cmp = _pad_rows(qn.reshape(bs, ts, NSA_HEADS, HEAD_DIM), 8).transpose(0, 2, 1, 3)
    q_cmp = q_cmp.reshape(bs, NSA_HEADS * 8, HEAD_DIM)
    rowtbl_c = rel_t[DA_HEADS + np.repeat(np.arange(NSA_HEADS), 8)]
    oc_s, memb_s = _cmp_sample(q_cmp, rowtbl_c, kc_s, vc_s, qpos0, nbs_s, mb_s)
    oc_s = oc_s.reshape(bs, NSA_HEADS, 8, HEAD_DIM)[:, :, :ts].transpose(0, 2, 1, 3)
    eye_g = jnp.eye(g2, dtype=BF16)
    qm_n = jnp.einsum("bqgrd,gs->bqgrsd", qn, eye_g).reshape(bs, ts * NSA_HEADS, g2 * HEAD_DIM)
    rows_n = [(q, g, r) for q in range(ts) for g in range(g2) for r in range(NSA_R)]
    rowq_n = jnp.asarray(np.array([[r[0], r[1] * NSA_R + r[2]] for r in rows_n], np.int32))
    rowtbl_n = rel_t[DA_HEADS + np.array([r[1] * NSA_R + r[2] for r in rows_n])]
    memb_rows = memb_s.reshape(bs, g2, 8, mb_s)[:, :, :ts, :nbs_s].transpose(0, 2, 1, 3)
    memb_rows = jnp.broadcast_to(memb_rows[:, :, :, None, :, None], (bs, ts, g2, NSA_R, nbs_s, SEL_BLOCK))
    memb_rows = memb_rows.reshape(bs, ts * NSA_HEADS, nbs_s * SEL_BLOCK)
    memb_rows = jnp.where(memb_rows > 0.5, rowtbl_n[None, :, N_BUCKETS - 1:N_BUCKETS], NEG_INF)
    memb_rows = jnp.pad(memb_rows, ((0, 0), (0, 0), (0, past + PAGE - nbs_s * SEL_BLOCK)), constant_values=NEG_INF)

    def feat_tail(col, width):
        return _pad_rows(p0[np_:, col:col + width].reshape(bs, ts, width), PAGE).transpose(0, 2, 1)

    pool_sel = cache_nsa_sel.transpose(0, 2, 3, 4, 1).reshape(n_phys, 4 * HEAD_DIM, PAGE)
    acc_sel = _decode("sel", qm_n, rowq_n, pool_sel, pt_flat, feat_tail(C0_SEL, 256),
                      n_pages, 2 * DECODE_PAGES, 0, qpos0, True, rowtbl=rowtbl_n, aux=memb_rows)
    w_buf = state_nsa_win.shape[1]
    pool_win = state_nsa_win.transpose(0, 2, 3, 4, 1).reshape(bs, 4 * HEAD_DIM, w_buf)
    acc_win = _decode("win", qm_n, rowq_n, pool_win, jnp.arange(bs, dtype=I32), feat_tail(C0_WIN, 256),
                      1, 1, qpos0 - w_buf, qpos0, True, rowtbl=rowtbl_n)

    def pick_group(acc):
        a = acc.reshape(bs, ts, g2, NSA_R, g2, HEAD_DIM)
        return jnp.stack([a[:, :, g, :, g] for g in range(g2)], 2)

    gate_s = p0[np_:, C0_GATE:C0_GATE + 3 * NSA_HEADS].reshape(ns * NSA_HEADS, 3)
    o_ns_s = _nsa_combine(oc_s.reshape(ns * NSA_HEADS, HEAD_DIM), pick_group(acc_sel).reshape(ns * NSA_HEADS, HEAD_DIM),
                          pick_group(acc_win).reshape(ns * NSA_HEADS, HEAD_DIM), gate_s).reshape(ns, 512)

    o_da = jnp.concatenate([o_da_p, o_da_s], 0)
    o_ns = jnp.concatenate([o_ns_p, o_ns_s], 0)
    x1 = _mix_ln([o_da, o_ns], l0_w_out.astype(BF16), x0, ln_g[0], ln_b[0])
    x2 = _ffn_ln(x1, ffn_w_gate.astype(BF16), ffn_w_up.astype(BF16), ffn_w_down.astype(BF16), ln_g[1], ln_b[1])

    w1p = jnp.pad(l1_w_in, ((0, 0), (0, C1 - l1_w_in.shape[1]))).astype(BF16)
    p1, p1b = _proj(x2, w1p)
    fl_p = p1[:np_, C1_F:C1_F + FOX_HEADS].reshape(np_ // PAGE, PAGE, FOX_HEADS).transpose(0, 2, 1)
    lft_p, ct_p = _cumsum_pages(fl_p, jnp.arange(np_ // PAGE, dtype=I32), b, t // PAGE, fox_b_f,
                                jnp.zeros((b, FOX_HEADS, 1), F32), True)
    q_fx = p1b[:np_, C1_Q:C1_Q + 1024].reshape(b, t, FOX_HEADS, HEAD_DIM)
    kv_fx = p1b[:np_, C1_KV:C1_KV + 512].reshape(b, t, 2, FOX_KV_HEADS, HEAD_DIM)
    qa_p, ka_p = _fox_augment(q_fx, kv_fx[:, :, 0], ct_p.transpose(0, 2, 1))
    o_fx_p = _gqa_prompt("fox", qa_p, ka_p, kv_fx[:, :, 1].transpose(0, 2, 3, 1))

    _, ct_past = _cumsum_pages(cache_fox_logf.transpose(0, 2, 1), pt_flat, bs, n_pages, fox_b_f,
                               jnp.zeros((bs, FOX_HEADS, 1), F32), False)
    f_new = _pad_rows(p1[np_:, C1_F:C1_F + FOX_HEADS].reshape(bs, ts, FOX_HEADS), PAGE).transpose(0, 2, 1)
    lft_s, ct_new = _cumsum_pages(f_new, jnp.arange(bs, dtype=I32), bs, 1, fox_b_f, ct_past[:, :, past - 1:past], True)
    ck_s = jnp.concatenate([ct_past, ct_new], 2)
    cq_s = ct_new[:, :, :ts].transpose(0, 2, 1).reshape(bs, ts * FOX_HEADS, 1)
    qf = p1b[np_:, C1_Q:C1_Q + 1024].reshape(bs, ts, FOX_KV_HEADS, FOX_R, HEAD_DIM) * jnp.asarray(SCALE, BF16)
    qm_f = jnp.einsum("bqgrd,gs->bqgrsd", qf, jnp.eye(FOX_KV_HEADS, dtype=BF16))
    qm_f = qm_f.reshape(bs, ts * FOX_HEADS, FOX_KV_HEADS * HEAD_DIM)
    rowq_f = jnp.asarray(np.stack([np.repeat(np.arange(ts), FOX_HEADS), np.tile(np.arange(FOX_HEADS), ts)],
                                  1).astype(np.int32))
    tail_f = _pad_rows(p1[np_:, C1_KV:C1_KV + 512].reshape(bs, ts, 512), PAGE).transpose(0, 2, 1)
    pool_fox = cache_fox_kv.transpose(0, 2, 3, 4, 1).reshape(n_phys, 2 * FOX_KV_HEADS * HEAD_DIM, PAGE)
    acc_f = _decode("fox", qm_f, rowq_f, pool_fox, pt_flat, tail_f, n_pages, 2 * DECODE_PAGES, 0, qpos0, True, cq=cq_s, aux=ck_s)
    acc_f = acc_f.reshape(bs, ts, FOX_KV_HEADS, FOX_R, FOX_KV_HEADS, HEAD_DIM)
    o_fx_s = jnp.stack([acc_f[:, :, g, :, g] for g in range(FOX_KV_HEADS)], 2)
    o_fx = jnp.concatenate([o_fx_p, o_fx_s.reshape(ns, 1024).astype(BF16)], 0)
    x3 = _mix_ln([o_fx], l1_w_out.astype(BF16), x2, ln_g[2], ln_b[2])

    y = _moe_ln(x3, moe_router, moe_w_gate, moe_w_up, moe_w_down, ln_g[3], ln_b[3])

    diff_all = p0[:, C0_DIFF:C0_DIFF + 1024]
    cmp_all = p0[:, C0_CMP:C0_CMP + 256]
    sel_all = p0[:, C0_SEL:C0_SEL + 256]
    win_all = p0[:, C0_WIN:C0_WIN + 256]
    fkv_all = p1[:, C1_KV:C1_KV + 512]
    keep_p = min(WINDOW, t)
    win_p_out = win_all[:np_].reshape(b, t, 2, g2, HEAD_DIM)[:, t - keep_p:]
    full_win_s = jnp.concatenate([state_nsa_win, win_all[np_:].reshape(bs, ts, 2, g2, HEAD_DIM)], 1)
    keep_s = min(WINDOW, full_win_s.shape[1])
    return (y[:np_].reshape(b, t, dm), y[np_:].reshape(bs, ts, dm),
            diff_all[:np_].reshape(b, t, DA_HEADS, 4 * HEAD_DIM), diff_all[np_:].reshape(bs, ts, DA_HEADS, 4 * HEAD_DIM),
            cmp_all[:np_].reshape(b, t, 2, g2, HEAD_DIM), cmp_all[np_:].reshape(bs, ts, 2, g2, HEAD_DIM),
            sel_all[:np_].reshape(b, t, 2, g2, HEAD_DIM), sel_all[np_:].reshape(bs, ts, 2, g2, HEAD_DIM),
            win_p_out, full_win_s[:, full_win_s.shape[1] - keep_s:],
            fkv_all[:np_].reshape(b, t, 2, FOX_KV_HEADS, HEAD_DIM), fkv_all[np_:].reshape(bs, ts, 2, FOX_KV_HEADS, HEAD_DIM),
            lft_p.transpose(0, 2, 1), lft_s[:, :, :ts].transpose(0, 2, 1))
```
